```python
import math
import jax
import jax.numpy as jnp
from jax import lax
import numpy as np

D_MODEL = 1024
BATCH = 8
SEQ = 4096
DEPTH = 2

GRID_W = 64
CTX_LEN = 256
RMS_EPS = 1e-6
ROPE_THETA = 10000.0
Q_BLOCK = 128

POOL_WINDOWS = (2, 4, 8, 16)
POOL_GROUPS = 4
POOL_WIDTH = 512
POOL_GROUP_DIM = POOL_WIDTH // POOL_GROUPS

DIFF_HEADS = 4
DIFF_QK_DIM = 64
DIFF_V_DIM = 2 * DIFF_QK_DIM
DIFF_QK_WIDTH = 2 * DIFF_HEADS * DIFF_QK_DIM
DIFF_WIDTH = DIFF_HEADS * DIFF_V_DIM

MLA_HEADS = 4
MLA_Q_RANK = 512
MLA_KV_RANK = 256
MLA_NOPE_DIM = 128
MLA_ROPE_DIM = 64
MLA_V_DIM = 128
MLA_WIDTH = MLA_HEADS * MLA_V_DIM

HG_HEADS = 4
HG_K_DIM = 128
HG_V_DIM = 128
HG_QK_WIDTH = HG_HEADS * HG_K_DIM
HG_WIDTH = HG_HEADS * HG_V_DIM
HG_CHUNK = 64

D_FF = 2816
CONV_WIDTH = 3

MIX_WIDTH = POOL_WIDTH + DIFF_WIDTH
EVEN_IN = POOL_WIDTH + 2 * DIFF_QK_WIDTH + DIFF_WIDTH
ODD_IN = MLA_Q_RANK + MLA_KV_RANK + MLA_ROPE_DIM + 3 * HG_QK_WIDTH + 2 * HG_WIDTH
N_EVEN = (DEPTH + 1) // 2
N_ODD = DEPTH // 2

kernel_name = "hybrid_pool_diff_mla_hgrn2_dit_block"


def _rmsnorm(x, g):
    xf = x.astype(jnp.float32)
    y = xf * lax.rsqrt(jnp.mean(xf * xf, axis=-1, keepdims=True) + RMS_EPS)
    return (y * g.astype(jnp.float32)).astype(x.dtype)


def _modulation(cond, w, b):
    m = jax.nn.silu(cond) @ w + b
    m = m.reshape(m.shape[:-1] + (6, 1, D_MODEL))
    return jnp.moveaxis(m, -3, 0)


def _modulate(h, shift, scale):
    return h * (1 + scale) + shift


def _axial_rope_tables(n_tokens, rot_dim, dtype):
    rows = n_tokens // GRID_W
    pos_row = jnp.repeat(jnp.arange(rows), GRID_W)
    pos_col = jnp.tile(jnp.arange(GRID_W), rows)
    axis_dim = rot_dim // 2
    inv_freq = ROPE_THETA ** (-jnp.arange(0, axis_dim, 2, dtype=jnp.float32) / axis_dim)
    ang = jnp.stack([pos_row, pos_col], axis=-1).astype(jnp.float32)[..., None] * inv_freq
    return jnp.cos(ang).astype(dtype), jnp.sin(ang).astype(dtype)


def _apply_axial_rope(x, cos, sin):
    r = x.shape[-1]
    xr = x.reshape(x.shape[:-1] + (2, 2, r // 4))
    bshape = (1, cos.shape[0]) + (1,) * (x.ndim - 3) + (2, r // 4)
    cs = cos.reshape(bshape)
    sn = sin.reshape(bshape)
    x1 = xr[..., 0, :]
    x2 = xr[..., 1, :]
    out = jnp.stack([x1 * cs - x2 * sn, x2 * cs + x1 * sn], axis=-2)
    return out.reshape(x.shape)


def _query_blocks(a):
    b, n = a.shape[:2]
    return jnp.moveaxis(a.reshape((b, n // Q_BLOCK, Q_BLOCK) + a.shape[2:]), 1, 0)


def _merge_blocks(o):
    o = jnp.moveaxis(o, 0, 1)
    return o.reshape((o.shape[0], o.shape[1] * o.shape[2]) + o.shape[3:])


def _multiscale_pool(u):
    b, n, _ = u.shape
    ug = u.reshape(b, n, POOL_GROUPS, POOL_GROUP_DIM).astype(jnp.float32)
    csum = jnp.concatenate([jnp.zeros((b, 1, POOL_GROUPS, POOL_GROUP_DIM), jnp.float32),
                            jnp.cumsum(ug, axis=1)], axis=1)
    half = jnp.array(POOL_WINDOWS, jnp.int32) // 2
    t = jnp.arange(n, dtype=jnp.int32)[:, None]
    lo = jnp.clip(t - half, 0, n)
    hi = jnp.clip(t + half, 0, n)
    grp = jnp.arange(POOL_GROUPS, dtype=jnp.int32)[None, :]
    win_sum = csum[:, hi, grp] - csum[:, lo, grp]
    mean = win_sum / (hi - lo).astype(jnp.float32)[None, :, :, None]
    return (mean - ug).astype(u.dtype)


def _diff_attention(q, k, v, lam):
    scale = DIFF_QK_DIM ** -0.5

    def one(qblk):
        bsz = qblk.shape[0]
        s = jnp.einsum('bqhd,bkhd->bhqk', qblk, k).astype(jnp.float32) * scale
        p = jax.nn.softmax(s, axis=-1).reshape(bsz, DIFF_HEADS, 2, Q_BLOCK, -1)
        a = p[:, :, 0] - lam * p[:, :, 1]
        return jnp.einsum('bhqk,bkhe->bqhe', a.astype(v.dtype), v)

    return _merge_blocks(lax.map(one, _query_blocks(q)))


def _mla_attention(q_nope, q_rope, k_nope, k_rope, v):
    scale = (MLA_NOPE_DIM + MLA_ROPE_DIM) ** -0.5

    def one(blk):
        qn, qr = blk
        s = jnp.einsum('bqhd,bkhd->bhqk', qn, k_nope) + jnp.einsum('bqhr,bkr->bhqk', qr, k_rope)
        p = jax.nn.softmax(s.astype(jnp.float32) * scale, axis=-1)
        return jnp.einsum('bhqk,bkhv->bqhv', p.astype(v.dtype), v)

    return _merge_blocks(lax.map(one, (_query_blocks(q_nope), _query_blocks(q_rope))))


def _hgrn_scan(q, k, v, log_f, s0):
    b, n = q.shape[:2]
    n_chunks = n // HG_CHUNK

    def chunks(a):
        return jnp.moveaxis(a.reshape((b, n_chunks, HG_CHUNK) + a.shape[2:]), 1, 0)

    tri = jnp.tril(jnp.ones((HG_CHUNK, HG_CHUNK), bool))[None, :, :, None, None]

    def step(s, inp):
        qc, kc, vc, gc = inp
        cum = jnp.cumsum(gc, axis=1)
        o_inter = jnp.einsum('bthk,bhkv->bthv', qc * jnp.exp(cum), s)
        rel = jnp.where(tri, cum[:, :, None] - cum[:, None, :], -jnp.inf)
        scores = jnp.einsum('bthk,bshk,btshk->bhts', qc, kc, jnp.exp(rel))
        o_intra = jnp.einsum('bhts,bshv->bthv', scores, vc)
        last = cum[:, -1]
        s_new = jnp.exp(last)[..., None] * s + jnp.einsum(
            'bshk,bshv->bhkv', kc * jnp.exp(last[:, None] - cum), vc)
        return s_new, o_inter + o_intra

    s_fin, o = lax.scan(step, s0, (chunks(q), chunks(k), chunks(v), chunks(log_f)))
    return _merge_blocks(o), s_fin


def _hgrn_direction(q_c, q_l, i_c, i_l, fr_c, fr_l, lb, reverse):
    def gates(fr):
        f = lb + (1.0 - lb) * jax.nn.sigmoid(fr)
        return 1.0 - f, jnp.log(f)

    def orient(a):
        return jnp.flip(a, axis=1) if reverse else a

    k_c, lf_c = gates(fr_c)
    k_l, lf_l = gates(fr_l)
    s0 = jnp.zeros((q_l.shape[0], HG_HEADS, HG_K_DIM, HG_V_DIM), jnp.float32)
    o_c, s_c = _hgrn_scan(orient(q_c), orient(k_c), orient(i_c), orient(lf_c), s0)
    o_l, _ = _hgrn_scan(orient(q_l), orient(k_l), orient(i_l), orient(lf_l), s_c)
    return orient(o_c), orient(o_l)


def _even_mixer(h_c, h_l, w_in, pool_w, pool_scale, lam_vec, subln, w_out, lam_init, cos, sin, need_ctx):
    lam_vec = lam_vec.astype(jnp.float32)
    lam = (jnp.exp(jnp.sum(lam_vec[0] * lam_vec[1])) - jnp.exp(jnp.sum(lam_vec[2] * lam_vec[3]))
           + lam_init)
    split_at = [POOL_WIDTH, POOL_WIDTH + DIFF_QK_WIDTH, POOL_WIDTH + 2 * DIFF_QK_WIDTH]

    def project(h):
        b, n = h.shape[:2]
        u, q, k, v = jnp.split(h @ w_in, split_at, axis=-1)
        return (u, q.reshape(b, n, 2 * DIFF_HEADS, DIFF_QK_DIM), k.reshape(b, n, 2 * DIFF_HEADS, DIFF_QK_DIM),
                v.reshape(b, n, DIFF_HEADS, DIFF_V_DIM))

    def pool_branch(u):
        b, n = u.shape[:2]
        d = _multiscale_pool(u).reshape(b, n, POOL_GROUPS, POOL_GROUP_DIM)
        y = jnp.einsum('blgc,gcd->blgd', d, pool_w).reshape(b, n, POOL_WIDTH)
        return y * pool_scale

    def diff_branch(q, k, v):
        b, n = q.shape[:2]
        o = _diff_attention(q, k, v, lam)
        return (_rmsnorm(o, subln) * (1.0 - lam_init)).reshape(b, n, DIFF_WIDTH)

    u_c, q_c, k_c, v_c = project(h_c)
    u_l, q_l, k_l, v_l = project(h_l)
    q_l = _apply_axial_rope(q_l, cos, sin)
    k_l = _apply_axial_rope(k_l, cos, sin)
    k_all = jnp.concatenate([k_c, k_l], axis=1)
    v_all = jnp.concatenate([v_c, v_l], axis=1)
    y_l = jnp.concatenate([pool_branch(u_l), diff_branch(q_l, k_all, v_all)], axis=-1) @ w_out
    y_c = None
    if need_ctx:
        y_c = jnp.concatenate([pool_branch(u_c), diff_branch(q_c, k_c, v_c)], axis=-1) @ w_out
    return y_c, y_l


def _odd_mixer(h_c, h_l, w_in, q_norm, w_uq, kv_norm, w_ukv, hg_norm, lbs, w_out, cos, sin, need_ctx):
    split_at = np.cumsum([MLA_Q_RANK, MLA_KV_RANK, MLA_ROPE_DIM, HG_QK_WIDTH, HG_QK_WIDTH,
                          HG_QK_WIDTH, HG_WIDTH]).tolist()

    def project(h):
        b, n = h.shape[:2]
        cq, ckv, kr, hq, hf_fwd, hf_bwd, hi, hg = jnp.split(h @ w_in, split_at, axis=-1)
        q = (_rmsnorm(cq, q_norm) @ w_uq).reshape(b, n, MLA_HEADS, MLA_NOPE_DIM + MLA_ROPE_DIM)
        kv = (_rmsnorm(ckv, kv_norm) @ w_ukv).reshape(b, n, MLA_HEADS, MLA_NOPE_DIM + MLA_V_DIM)
        ks = (b, n, HG_HEADS, HG_K_DIM)
        vs = (b, n, HG_HEADS, HG_V_DIM)
        hgrn = (jax.nn.silu(hq).reshape(ks).astype(jnp.float32), hf_fwd.reshape(ks).astype(jnp.float32),
                hf_bwd.reshape(ks).astype(jnp.float32), hi.reshape(vs).astype(jnp.float32), hg.reshape(vs))
        return (q[..., :MLA_NOPE_DIM], q[..., MLA_NOPE_DIM:], kv[..., :MLA_NOPE_DIM],
                kv[..., MLA_NOPE_DIM:], kr, hgrn)

    qn_c, qr_c, kn_c, v_c, kr_c, hg_c = project(h_c)
    qn_l, qr_l, kn_l, v_l, kr_l, hg_l = project(h_l)
    qr_l = _apply_axial_rope(qr_l, cos, sin)
    kr_l = _apply_axial_rope(kr_l, cos, sin)
    mla_l = _mla_attention(qn_l, qr_l, jnp.concatenate([kn_c, kn_l], axis=1),
                           jnp.concatenate([kr_c, kr_l], axis=1), jnp.concatenate([v_c, v_l], axis=1))

    oc_f, ol_f = _hgrn_direction(hg_c[0], hg_l[0], hg_c[3], hg_l[3], hg_c[1], hg_l[1],
                                 lbs[0].reshape(HG_HEADS, HG_K_DIM), False)
    oc_b, ol_b = _hgrn_direction(hg_c[0], hg_l[0], hg_c[3], hg_l[3], hg_c[2], hg_l[2],
                                 lbs[1].reshape(HG_HEADS, HG_K_DIM), True)

    def hgrn_readout(o, gate):
        b, n = gate.shape[:2]
        return (_rmsnorm(o.astype(gate.dtype), hg_norm) * jax.nn.silu(gate)).reshape(b, n, HG_WIDTH)

    def merge(mla, hg_out):
        b, n = mla.shape[:2]
        return jnp.concatenate([mla.reshape(b, n, MLA_WIDTH), hg_out], axis=-1) @ w_out

    y_l = merge(mla_l, hgrn_readout(ol_f + ol_b, hg_l[4]))
    y_c = None
    if need_ctx:
        mla_c = _mla_attention(qn_c, qr_c, kn_c, kr_c, v_c)
        y_c = merge(mla_c, hgrn_readout(oc_f + oc_b, hg_c[4]))
    return y_c, y_l


def _conv_ffn(h, w_gate, w_up, conv_w, conv_b, w_down):
    a = h @ w_gate
    ap = jnp.pad(a, ((0, 0), (1, 1), (0, 0)))
    a = ap[:, :-2] * conv_w[0] + ap[:, 1:-1] * conv_w[1] + ap[:, 2:] * conv_w[2] + conv_b
    return (jax.nn.silu(a) * (h @ w_up)) @ w_down


def setup_inputs(seed: int = 0) -> dict:
    key = jax.random.key(seed)
    ks = jax.random.split(key, 25)

    def nrm(k, shape, scale):
        return jax.random.normal(k, shape, jnp.float32) * scale

    def gain(k, shape):
        return 1.0 + 0.05 * jax.random.normal(k, shape, jnp.float32)

    return {
        "x": nrm(ks[0], (BATCH, SEQ, D_MODEL), 1.0),
        "c": nrm(ks[1], (BATCH, D_MODEL), 1.0),
        "ctx": nrm(ks[2], (BATCH, CTX_LEN, D_MODEL), 1.0),
        "c_ctx": nrm(ks[3], (D_MODEL,), 1.0),
        "ada_w": nrm(ks[4], (DEPTH, D_MODEL, 6 * D_MODEL), 0.5 * D_MODEL ** -0.5),
        "ada_b": nrm(ks[5], (DEPTH, 6 * D_MODEL), 0.02),
        "norm_g": gain(ks[6], (DEPTH, 4, D_MODEL)),
        "mix_w_out": nrm(ks[7], (DEPTH, MIX_WIDTH, D_MODEL), MIX_WIDTH ** -0.5),
        "ffn_w_gate": nrm(ks[8], (DEPTH, D_MODEL, D_FF), D_MODEL ** -0.5),
        "ffn_w_up": nrm(ks[9], (DEPTH, D_MODEL, D_FF), D_MODEL ** -0.5),
        "ffn_conv_w": nrm(ks[10], (DEPTH, CONV_WIDTH, D_FF), CONV_WIDTH ** -0.5),
        "ffn_conv_b": nrm(ks[11], (DEPTH, D_FF), 0.02),
        "ffn_w_down": nrm(ks[12], (DEPTH, D_FF, D_MODEL), D_FF ** -0.5),
        "ev_w_in": nrm(ks[13], (N_EVEN, D_MODEL, EVEN_IN), D_MODEL ** -0.5),
        "pool_w": nrm(ks[14], (N_EVEN, POOL_GROUPS, POOL_GROUP_DIM, POOL_GROUP_DIM), POOL_GROUP_DIM ** -0.5),
        "pool_scale": 1.0 + 0.1 * jax.random.normal(ks[15], (N_EVEN, POOL_WIDTH), jnp.float32),
        "diff_lambda": nrm(ks[16], (N_EVEN, 4, DIFF_QK_DIM), 0.1),
        "diff_subln": gain(ks[17], (N_EVEN, DIFF_V_DIM)),
        "od_w_in": nrm(ks[18], (N_ODD, D_MODEL, ODD_IN), D_MODEL ** -0.5),
        "mla_q_norm": gain(ks[19], (N_ODD, MLA_Q_RANK)),
        "mla_w_uq": nrm(ks[20], (N_ODD, MLA_Q_RANK, MLA_HEADS * (MLA_NOPE_DIM + MLA_ROPE_DIM)), MLA_Q_RANK ** -0.5),
        "mla_kv_norm": gain(ks[21], (N_ODD, MLA_KV_RANK)),
        "mla_w_ukv": nrm(ks[22], (N_ODD, MLA_KV_RANK, MLA_HEADS * (MLA_NOPE_DIM + MLA_V_DIM)), MLA_KV_RANK ** -0.5),
        "hgrn_norm": gain(ks[23], (N_ODD, HG_V_DIM)),
        "hgrn_lb": nrm(ks[24], (2, DEPTH, HG_QK_WIDTH), 0.5),
    }


def reference(x, c, ctx, c_ctx, ada_w, ada_b, norm_g, mix_w_out, ffn_w_gate, ffn_w_up, ffn_conv_w,
              ffn_conv_b, ffn_w_down, ev_w_in, pool_w, pool_scale, diff_lambda, diff_subln, od_w_in,
              mla_q_norm, mla_w_uq, mla_kv_norm, mla_w_ukv, hgrn_norm, hgrn_lb):
    n_lat = x.shape[1]
    cos, sin = _axial_rope_tables(n_lat, DIFF_QK_DIM, x.dtype)
    probs = jax.nn.softmax(hgrn_lb.astype(jnp.float32), axis=1)
    lower_bounds = jnp.cumsum(probs, axis=1) - probs[:, :1]
    xc = ctx
    for layer in range(DEPTH):
        last = layer == DEPTH - 1
        j = layer // 2
        sh1, sc1, g1, sh2, sc2, g2 = _modulation(c, ada_w[layer], ada_b[layer])
        csh1, csc1, cg1, csh2, csc2, cg2 = _modulation(c_ctx, ada_w[layer], ada_b[layer])
        h_l = _modulate(_rmsnorm(x, norm_g[layer, 0]), sh1, sc1)
        h_c = _modulate(_rmsnorm(xc, norm_g[layer, 0]), csh1, csc1)
        if layer % 2 == 0:
            lam_init = 0.8 - 0.6 * math.exp(-0.3 * layer)
            y_c, y_l = _even_mixer(h_c, h_l, ev_w_in[j], pool_w[j], pool_scale[j], diff_lambda[j],
                                   diff_subln[j], mix_w_out[layer], lam_init, cos, sin, not last)
        else:
            y_c, y_l = _odd_mixer(h_c, h_l, od_w_in[j], mla_q_norm[j], mla_w_uq[j], mla_kv_norm[j],
                                  mla_w_ukv[j], hgrn_norm[j], lower_bounds[:, layer], mix_w_out[layer],
                                  cos, sin, not last)
        x = x + g1 * _rmsnorm(y_l, norm_g[layer, 1])
        f_l = _conv_ffn(_modulate(_rmsnorm(x, norm_g[layer, 2]), sh2, sc2), ffn_w_gate[layer],
                        ffn_w_up[layer], ffn_conv_w[layer], ffn_conv_b[layer], ffn_w_down[layer])
        x = x + g2 * _rmsnorm(f_l, norm_g[layer, 3])
        if not last:
            xc = xc + cg1 * _rmsnorm(y_c, norm_g[layer, 1])
            f_c = _conv_ffn(_modulate(_rmsnorm(xc, norm_g[layer, 2]), csh2, csc2), ffn_w_gate[layer],
                            ffn_w_up[layer], ffn_conv_w[layer], ffn_conv_b[layer], ffn_w_down[layer])
            xc = xc + cg2 * _rmsnorm(f_c, norm_g[layer, 3])
    return x
```

```python
import functools
import math

import jax
import jax.numpy as jnp
import numpy as np
from jax import lax
from jax.experimental import pallas as pl
from jax.experimental.pallas import tpu as pltpu

F32 = jnp.float32
BF16 = jnp.bfloat16

D_MODEL = 1024
GRID_W = 64
RMS_EPS = 1e-6
ROPE_THETA = 10000.0

POOL_WINDOWS = (2, 4, 8, 16)
POOL_GROUPS = 4
POOL_WIDTH = 512
POOL_GROUP_DIM = POOL_WIDTH // POOL_GROUPS

DIFF_HEADS = 4
DIFF_QK_DIM = 64
DIFF_V_DIM = 128
DIFF_QK_WIDTH = 2 * DIFF_HEADS * DIFF_QK_DIM
DIFF_WIDTH = DIFF_HEADS * DIFF_V_DIM

MLA_HEADS = 4
MLA_Q_RANK = 512
MLA_KV_RANK = 256
MLA_NOPE_DIM = 128
MLA_ROPE_DIM = 64
MLA_V_DIM = 128
MLA_QK_PAD = 256
MLA_WIDTH = MLA_HEADS * MLA_V_DIM

HG_HEADS = 4
HG_K_DIM = 128
HG_V_DIM = 128
HG_QK_WIDTH = HG_HEADS * HG_K_DIM
HG_WIDTH = HG_HEADS * HG_V_DIM
HG_CHUNK = 64
HG_LEVELS = (64, 32, 16, 8, 4, 2)

D_FF = 2816
FF_CHUNK = 256

LANES = 128
BF16_SUBLANES = 16

TILE = 256
ATT_TQ = 128
HG_BLOCK = 256
HALO = BF16_SUBLANES
VMEM_LIMIT = 56 * 1024 * 1024


def _cparams(n_axes):
    return pltpu.CompilerParams(
        dimension_semantics=("arbitrary",) * n_axes, vmem_limit_bytes=VMEM_LIMIT)


def _dot(a, b):
    return jnp.dot(a, b, preferred_element_type=F32)


def _dot_nt(a, b):
    return lax.dot_general(a, b, (((1,), (1,)), ((), ())), preferred_element_type=F32)


def _rms(x, g):
    return x * lax.rsqrt(jnp.mean(x * x, axis=-1, keepdims=True) + RMS_EPS) * g


def _silu(x):
    return x * jax.nn.sigmoid(x)


def _const_spec(shape):
    zeros = (0,) * len(shape)
    return pl.BlockSpec(shape, lambda *_: zeros, pipeline_mode=pl.Buffered(1))


def _rope(x, cos, sin_signed):
    n = x.shape[-1]
    lane = lax.broadcasted_iota(jnp.int32, x.shape, 1)
    first_half = (lane & 31) < 16
    partner = jnp.where(first_half, pltpu.roll(x, n - 16, 1), pltpu.roll(x, 16, 1))
    return x * cos + partner * sin_signed


def _mod_kernel(c_ref, w_ref, b_ref, o_ref):
    s = _silu(c_ref[...])
    w = w_ref[0]
    s_hi = s.astype(BF16)
    s_lo = (s - s_hi.astype(F32)).astype(BF16)
    w_hi = w.astype(BF16)
    w_lo = (w - w_hi.astype(F32)).astype(BF16)
    o_ref[0] = _dot(s_hi, w_hi) + _dot(s_hi, w_lo) + _dot(s_lo, w_hi) + b_ref[0]


def _modulation(cond, ada_w, ada_b):
    depth, d, n = ada_w.shape
    rows = cond.shape[0]
    tn = 1536
    return pl.pallas_call(
        _mod_kernel,
        grid=(depth, n // tn),
        in_specs=[
            pl.BlockSpec((rows, d), lambda l, j: (0, 0)),
            pl.BlockSpec((1, d, tn), lambda l, j: (l, 0, j)),
            pl.BlockSpec((1, 1, tn), lambda l, j: (l, 0, j)),
        ],
        out_specs=pl.BlockSpec((1, rows, tn), lambda l, j: (l, 0, j)),
        out_shape=jax.ShapeDtypeStruct((depth, rows, n), F32),
        compiler_params=_cparams(2),
    )(cond, ada_w, ada_b.reshape(depth, 1, n))


def _even_in_kernel(x_ref, mod_ref, ng_ref, cos_ref, sin_ref, w_ref, u_ref, q_ref, kt_ref, v_ref):
    h = _rms(x_ref[0], ng_ref[0:1, :]) * (1.0 + mod_ref[0, 1:2, :]) + mod_ref[0, 0:1, :]
    p = _dot(h.astype(BF16), w_ref[...])
    cos = cos_ref[...]
    sin = sin_ref[...]
    q = p[:, POOL_WIDTH:POOL_WIDTH + DIFF_QK_WIDTH]
    k = p[:, POOL_WIDTH + DIFF_QK_WIDTH:POOL_WIDTH + 2 * DIFF_QK_WIDTH]
    u_ref[0] = p[:, :POOL_WIDTH].astype(BF16)
    q_ref[0] = (_rope(q, cos, sin) * (DIFF_QK_DIM ** -0.5)).astype(BF16)
    kt_ref[0] = _rope(k, cos, sin).T.astype(BF16)
    v_ref[0] = p[:, POOL_WIDTH + 2 * DIFF_QK_WIDTH:].astype(BF16)


def _even_in(x, mod_l, ng_l, cos, sin, w_in, ctx_tiles):
    b, t, d = x.shape
    nt = t // TILE
    n_in = w_in.shape[1]

    def mod_map(i, j):
        return (jnp.where(j < ctx_tiles, b, i), 0, 0)

    tok = lambda w: pl.BlockSpec((1, TILE, w), lambda i, j: (i, j, 0))
    return pl.pallas_call(
        _even_in_kernel,
        grid=(b, nt),
        in_specs=[
            tok(d),
            pl.BlockSpec((1, 6, d), mod_map),
            _const_spec((4, d)),
            pl.BlockSpec((TILE, DIFF_QK_WIDTH), lambda i, j: (j, 0)),
            pl.BlockSpec((TILE, DIFF_QK_WIDTH), lambda i, j: (j, 0)),
            _const_spec((d, n_in)),
        ],
        out_specs=[
            tok(POOL_WIDTH),
            tok(DIFF_QK_WIDTH),
            pl.BlockSpec((1, DIFF_QK_WIDTH, TILE), lambda i, j: (i, 0, j)),
            tok(DIFF_WIDTH),
        ],
        out_shape=[
            jax.ShapeDtypeStruct((b, t, POOL_WIDTH), BF16),
            jax.ShapeDtypeStruct((b, t, DIFF_QK_WIDTH), BF16),
            jax.ShapeDtypeStruct((b, DIFF_QK_WIDTH, t), BF16),
            jax.ShapeDtypeStruct((b, t, DIFF_WIDTH), BF16),
        ],
        compiler_params=_cparams(2),
    )(x, mod_l, ng_l, cos, sin, w_in)


def _softmax_parts(s):
    e = jnp.exp(s - jnp.max(s, axis=-1, keepdims=True))
    return e, 1.0 / jnp.sum(e, axis=-1, keepdims=True)


def _diff_attn_kernel(lam_init, n_ctx, q_ref, kt_ref, v_ref, lam_ref, g_ref, o_ref):
    lv = lam_ref[...]
    lam = (jnp.exp(jnp.sum(lv[0:1] * lv[1:2], axis=-1, keepdims=True))
           - jnp.exp(jnp.sum(lv[2:3] * lv[3:4], axis=-1, keepdims=True)) + lam_init)
    g = g_ref[...] * (1.0 - lam_init)
    lane = lax.broadcasted_iota(jnp.int32, (ATT_TQ, LANES), 1)

    def attend(nk):
        for h in range(DIFF_HEADS):
            cols = slice(h * LANES, (h + 1) * LANES)
            qp = q_ref[0, :, cols]
            kt = kt_ref[0, cols, :nk]
            e0, r0 = _softmax_parts(_dot(jnp.where(lane < DIFF_QK_DIM, qp, 0), kt))
            e1, r1 = _softmax_parts(_dot(jnp.where(lane >= DIFF_QK_DIM, qp, 0), kt))
            a = e0 * r0 - e1 * (lam * r1)
            o = _dot(a.astype(BF16), v_ref[0, :nk, cols])
            o_ref[0, :, cols] = _rms(o, g).astype(BF16)

    is_ctx = pl.program_id(1) < n_ctx // ATT_TQ

    @pl.when(is_ctx)
    def _():
        attend(n_ctx)

    @pl.when(jnp.logical_not(is_ctx))
    def _():
        attend(kt_ref.shape[2])


def _diff_attn(q, kt, v, lam_vec, subln, lam_init, n_ctx):
    b, t, _ = q.shape
    return pl.pallas_call(
        functools.partial(_diff_attn_kernel, lam_init, n_ctx),
        grid=(b, t // ATT_TQ),
        in_specs=[
            pl.BlockSpec((1, ATT_TQ, DIFF_QK_WIDTH), lambda i, j: (i, j, 0)),
            pl.BlockSpec((1, DIFF_QK_WIDTH, t), lambda i, j: (i, 0, 0)),
            pl.BlockSpec((1, t, DIFF_WIDTH), lambda i, j: (i, 0, 0)),
            _const_spec((4, DIFF_QK_DIM)),
            _const_spec((1, DIFF_V_DIM)),
        ],
        out_specs=pl.BlockSpec((1, ATT_TQ, DIFF_WIDTH), lambda i, j: (i, j, 0)),
        out_shape=jax.ShapeDtypeStruct((b, t, DIFF_WIDTH), BF16),
        compiler_params=_cparams(2),
    )(q, kt, v, lam_vec, subln.reshape(1, DIFF_V_DIM))


def _residual_and_prenorm(x, y, mod_ref, ng_ref, xo_ref, h2_ref):
    xn = x + mod_ref[0, 2:3, :] * _rms(y, ng_ref[1:2, :])
    xo_ref[0] = xn
    h2 = _rms(xn, ng_ref[2:3, :]) * (1.0 + mod_ref[0, 4:5, :]) + mod_ref[0, 3:4, :]
    h2_ref[0] = h2.astype(BF16)


def _even_out_kernel(t_off, seg_tiles, n_tiles, x_ref, u_ref, up_ref, un_ref, a_ref, mod_ref, ng_ref,
                     pw_ref, ps_ref, wo_ref, xo_ref, h2_ref, ext_ref):
    j = pl.program_id(1) + t_off
    prev_ok = jnp.logical_and(j != 0, j != seg_tiles)
    next_ok = jnp.logical_and(j != n_tiles - 1, j != seg_tiles - 1)
    ext_ref[0:HALO, :] = jnp.where(prev_ok, up_ref[0].astype(F32), 0.0)
    ext_ref[HALO:HALO + TILE, :] = u_ref[0].astype(F32)
    ext_ref[HALO + TILE:, :] = jnp.where(next_ok, un_ref[0].astype(F32), 0.0)
    lo = jnp.where(prev_ok, -HALO, 0)
    hi = jnp.where(next_ok, TILE + HALO, TILE)
    row = lax.broadcasted_iota(jnp.int32, (TILE, 1), 0)
    parts = []
    for gidx, win in enumerate(POOL_WINDOWS):
        half = win // 2
        cols = slice(gidx * POOL_GROUP_DIM, (gidx + 1) * POOL_GROUP_DIM)
        acc = ext_ref[HALO - half:HALO - half + TILE, cols]
        for off in range(-half + 1, half):
            acc = acc + ext_ref[HALO + off:HALO + off + TILE, cols]
        cnt = jnp.minimum(row + half, hi) - jnp.maximum(row - half, lo)
        dlt = acc / cnt.astype(F32) - ext_ref[HALO:HALO + TILE, cols]
        parts.append(_dot(dlt.astype(BF16), pw_ref[gidx]))
    yp = jnp.concatenate(parts, axis=1) * ps_ref[...]
    y = _dot(yp.astype(BF16), wo_ref[:POOL_WIDTH, :]) + _dot(a_ref[0], wo_ref[POOL_WIDTH:, :])
    _residual_and_prenorm(x_ref[0], y, mod_ref, ng_ref, xo_ref, h2_ref)


def _even_out(x, u, att, mod_l, ng_l, pool_w, pool_scale, w_out, ctx_tiles, t_off):
    b, t, d = x.shape
    nt = t // TILE
    nh = TILE // HALO
    n_out = nt - t_off

    def mod_map(i, j):
        return (jnp.where(j + t_off < ctx_tiles, b, i), 0, 0)

    tok = lambda w: pl.BlockSpec((1, TILE, w), lambda i, j: (i, j + t_off, 0))
    out_tok = lambda: pl.BlockSpec((1, TILE, d), lambda i, j: (i, j, 0))
    return pl.pallas_call(
        functools.partial(_even_out_kernel, t_off, ctx_tiles, nt),
        grid=(b, n_out),
        in_specs=[
            tok(d),
            tok(POOL_WIDTH),
            pl.BlockSpec((1, HALO, POOL_WIDTH), lambda i, j: (i, jnp.maximum((j + t_off) * nh - 1, 0), 0)),
            pl.BlockSpec((1, HALO, POOL_WIDTH),
                         lambda i, j: (i, jnp.minimum((j + t_off + 1) * nh, nt * nh - 1), 0)),
            tok(DIFF_WIDTH),
            pl.BlockSpec((1, 6, d), mod_map),
            _const_spec((4, d)),
            _const_spec((POOL_GROUPS, POOL_GROUP_DIM, POOL_GROUP_DIM)),
            _const_spec((1, POOL_WIDTH)),
            _const_spec((POOL_WIDTH + DIFF_WIDTH, d)),
        ],
        out_specs=[out_tok(), out_tok()],
        out_shape=[
            jax.ShapeDtypeStruct((b, n_out * TILE, d), F32),
            jax.ShapeDtypeStruct((b, n_out * TILE, d), BF16),
        ],
        scratch_shapes=[pltpu.VMEM((TILE + 2 * HALO, POOL_WIDTH), F32)],
        compiler_params=_cparams(2),
    )(x, u, u, u, att, mod_l, ng_l, pool_w, pool_scale.reshape(1, POOL_WIDTH), w_out)


def _ffn_kernel(seg_tiles, n_tiles, x_ref, h_ref, hp_ref, hn_ref, mod_ref, ng_ref, wg_ref, wu_ref,
                cw_ref, cb_ref, wd_ref, o_ref, a_ref, acc_ref):
    j = pl.program_id(1)
    prev_ok = jnp.logical_and(j != 0, j != seg_tiles)
    next_ok = jnp.logical_and(j != n_tiles - 1, j != seg_tiles - 1)
    h = h_ref[0]
    zero = jnp.zeros((HALO, h.shape[1]), BF16)
    h_ext = jnp.concatenate(
        [jnp.where(prev_ok, hp_ref[0], zero), h, jnp.where(next_ok, hn_ref[0], zero)], axis=0)
    for c in range(D_FF // FF_CHUNK):
        cols = slice(c * FF_CHUNK, (c + 1) * FF_CHUNK)
        a_ref[...] = _dot(h_ext, wg_ref[:, cols])
        a = (a_ref[HALO - 1:HALO - 1 + TILE, :] * cw_ref[0:1, cols]
             + a_ref[HALO:HALO + TILE, :] * cw_ref[1:2, cols]
             + a_ref[HALO + 1:HALO + 1 + TILE, :] * cw_ref[2:3, cols] + cb_ref[:, cols])
        act = _silu(a) * _dot(h, wu_ref[:, cols])
        part = _dot(act.astype(BF16), wd_ref[cols, :])
        if c == 0:
            acc_ref[...] = part
        else:
            acc_ref[...] += part
    o_ref[0] = x_ref[0] + mod_ref[0, 5:6, :] * _rms(acc_ref[...], ng_ref[3:4, :])


def _ffn(x, h2, mod_l, ng_l, w_gate, w_up, conv_w, conv_b, w_down, ctx_tiles):
    b, t, d = x.shape
    nt = t // TILE
    nh = TILE // HALO

    def mod_map(i, j):
        return (jnp.where(j < ctx_tiles, b, i), 0, 0)

    tok = lambda: pl.BlockSpec((1, TILE, d), lambda i, j: (i, j, 0))
    return pl.pallas_call(
        functools.partial(_ffn_kernel, ctx_tiles, nt),
        grid=(b, nt),
        in_specs=[
            tok(),
            tok(),
            pl.BlockSpec((1, HALO, d), lambda i, j: (i, jnp.maximum(j * nh - 1, 0), 0)),
            pl.BlockSpec((1, HALO, d), lambda i, j: (i, jnp.minimum((j + 1) * nh, nt * nh - 1), 0)),
            pl.BlockSpec((1, 6, d), mod_map),
            _const_spec((4, d)),
            _const_spec((d, D_FF)),
            _const_spec((d, D_FF)),
            _const_spec((3, D_FF)),
            _const_spec((1, D_FF)),
            _const_spec((D_FF, d)),
        ],
        out_specs=tok(),
        out_shape=jax.ShapeDtypeStruct((b, t, d), F32),
        scratch_shapes=[pltpu.VMEM((TILE + 2 * HALO, FF_CHUNK), F32), pltpu.VMEM((TILE, d), F32)],
        compiler_params=_cparams(2),
    )(x, h2, h2, h2, mod_l, ng_l, w_gate, w_up, conv_w, conv_b.reshape(1, D_FF), w_down)


def _odd_in_kernel(layer, x_ref, mod_ref, ng_ref, cos_ref, sin_ref, wq_ref, wkv_ref, wkr_ref, wh_ref,
                   qn_ref, wuq_ref, kvn_ref, wukv_ref, lb_ref,
                   q_ref, kt_ref, v_ref, hq_ref, hkf_ref, hkb_ref, lff_ref, lfb_ref, hv_ref, hg_ref):
    h = _rms(x_ref[0], ng_ref[0:1, :]) * (1.0 + mod_ref[0, 1:2, :]) + mod_ref[0, 0:1, :]
    hb = h.astype(BF16)
    cos = cos_ref[...]
    sin = sin_ref[...]

    cq = _rms(_dot(hb, wq_ref[...]), qn_ref[...])
    q = _dot(cq.astype(BF16), wuq_ref[...]) * ((MLA_NOPE_DIM + MLA_ROPE_DIM) ** -0.5)
    for hd in range(MLA_HEADS):
        base = hd * MLA_QK_PAD
        q_ref[0, :, base:base + MLA_NOPE_DIM] = q[:, base:base + MLA_NOPE_DIM].astype(BF16)
        q_ref[0, :, base + MLA_NOPE_DIM:base + MLA_QK_PAD] = _rope(
            q[:, base + MLA_NOPE_DIM:base + MLA_QK_PAD], cos, sin).astype(BF16)

    ckv = _rms(_dot(hb, wkv_ref[...]), kvn_ref[...])
    kv = _dot(ckv.astype(BF16), wukv_ref[...])
    kr_t = _rope(_dot(hb, wkr_ref[...]), cos, sin).T.astype(BF16)
    for hd in range(MLA_HEADS):
        base = hd * (MLA_NOPE_DIM + MLA_V_DIM)
        kt_ref[0, hd, :MLA_NOPE_DIM, :] = kv[:, base:base + MLA_NOPE_DIM].T.astype(BF16)
        kt_ref[0, hd, MLA_NOPE_DIM:, :] = kr_t
        v_ref[0, :, hd * MLA_V_DIM:(hd + 1) * MLA_V_DIM] = kv[:, base + MLA_NOPE_DIM:base + MLA_NOPE_DIM
                                                              + MLA_V_DIM].astype(BF16)

    g = _dot(hb, wh_ref[...])
    w = HG_QK_WIDTH
    hq_ref[0] = _silu(g[:, :w]).astype(BF16)
    hv_ref[0] = g[:, 3 * w:4 * w].astype(BF16)
    hg_ref[0] = _silu(g[:, 4 * w:]).astype(BF16)
    for direction, (k_ref, lf_ref) in enumerate(((hkf_ref, lff_ref), (hkb_ref, lfb_ref))):
        rows = [lb_ref[direction, i:i + 1, :] for i in range(lb_ref.shape[1])]
        top = functools.reduce(jnp.maximum, rows)
        ex = [jnp.exp(r - top) for r in rows]
        lb = sum(ex[1:layer + 1], jnp.zeros_like(top)) / sum(ex)
        f = lb + (1.0 - lb) * jax.nn.sigmoid(g[:, (1 + direction) * w:(2 + direction) * w])
        k_ref[0] = (1.0 - f).astype(BF16)
        lf_ref[0] = jnp.log(f)


def _odd_in(x, mod_l, ng_l, cos, sin, w_in, q_norm, w_uq, kv_norm, w_ukv, hgrn_lb, layer, ctx_tiles):
    b, t, d = x.shape
    nt = t // TILE
    o1 = MLA_Q_RANK
    o2 = o1 + MLA_KV_RANK
    o3 = o2 + MLA_ROPE_DIM
    w_q = w_in[:, :o1]
    w_kv = w_in[:, o1:o2]
    w_kr = jnp.pad(w_in[:, o2:o3], ((0, 0), (0, LANES - MLA_ROPE_DIM)))
    w_h = w_in[:, o3:]
    qk = MLA_NOPE_DIM + MLA_ROPE_DIM
    w_uq_pad = jnp.pad(w_uq.reshape(MLA_Q_RANK, MLA_HEADS, qk),
                       ((0, 0), (0, 0), (0, MLA_QK_PAD - qk))).reshape(MLA_Q_RANK, MLA_HEADS * MLA_QK_PAD)

    def mod_map(i, j):
        return (jnp.where(j < ctx_tiles, b, i), 0, 0)

    tok = lambda w: pl.BlockSpec((1, TILE, w), lambda i, j: (i, j, 0))
    sds = lambda w, dt: jax.ShapeDtypeStruct((b, t, w), dt)
    hw = HG_QK_WIDTH
    return pl.pallas_call(
        functools.partial(_odd_in_kernel, layer),
        grid=(b, nt),
        in_specs=[
            tok(d),
            pl.BlockSpec((1, 6, d), mod_map),
            _const_spec((4, d)),
            pl.BlockSpec((TILE, LANES), lambda i, j: (j, 0)),
            pl.BlockSpec((TILE, LANES), lambda i, j: (j, 0)),
            _const_spec(w_q.shape),
            _const_spec(w_kv.shape),
            _const_spec(w_kr.shape),
            _const_spec(w_h.shape),
            _const_spec((1, MLA_Q_RANK)),
            _const_spec(w_uq_pad.shape),
            _const_spec((1, MLA_KV_RANK)),
            _const_spec(w_ukv.shape),
            _const_spec(hgrn_lb.shape),
        ],
        out_specs=[
            tok(MLA_HEADS * MLA_QK_PAD),
            pl.BlockSpec((1, MLA_HEADS, MLA_QK_PAD, TILE), lambda i, j: (i, 0, 0, j)),
            tok(MLA_WIDTH),
            tok(hw), tok(hw), tok(hw), tok(hw), tok(hw), tok(hw), tok(hw),
        ],
        out_shape=[
            sds(MLA_HEADS * MLA_QK_PAD, BF16),
            jax.ShapeDtypeStruct((b, MLA_HEADS, MLA_QK_PAD, t), BF16),
            sds(MLA_WIDTH, BF16),
            sds(hw, BF16), sds(hw, BF16), sds(hw, BF16), sds(hw, F32), sds(hw, F32), sds(hw, BF16),
            sds(hw, BF16),
        ],
        compiler_params=_cparams(2),
    )(x, mod_l, ng_l, cos, sin, w_q, w_kv, w_kr, w_h, q_norm.reshape(1, -1), w_uq_pad,
      kv_norm.reshape(1, -1), w_ukv, hgrn_lb)


def _mla_attn_kernel(n_ctx, q_off, q_ref, kt_ref, v_ref, o_ref):
    def attend(nk):
        for h in range(MLA_HEADS):
            e, r = _softmax_parts(_dot(q_ref[0, :, h * MLA_QK_PAD:(h + 1) * MLA_QK_PAD], kt_ref[0, h, :, :nk]))
            cols = slice(h * MLA_V_DIM, (h + 1) * MLA_V_DIM)
            o_ref[0, :, cols] = (_dot(e.astype(BF16), v_ref[0, :nk, cols]) * r).astype(BF16)

    if q_off * ATT_TQ >= n_ctx:
        attend(kt_ref.shape[3])
    else:
        is_ctx = pl.program_id(1) + q_off < n_ctx // ATT_TQ

        @pl.when(is_ctx)
        def _():
            attend(n_ctx)

        @pl.when(jnp.logical_not(is_ctx))
        def _():
            attend(kt_ref.shape[3])


def _mla_attn(q, kt, v, n_ctx, q_off):
    b, t, _ = q.shape
    nq = t // ATT_TQ - q_off
    return pl.pallas_call(
        functools.partial(_mla_attn_kernel, n_ctx, q_off),
        grid=(b, nq),
        in_specs=[
            pl.BlockSpec((1, ATT_TQ, MLA_HEADS * MLA_QK_PAD), lambda i, j: (i, j + q_off, 0)),
            pl.BlockSpec((1, MLA_HEADS, MLA_QK_PAD, t), lambda i, j: (i, 0, 0, 0)),
            pl.BlockSpec((1, t, MLA_WIDTH), lambda i, j: (i, 0, 0)),
        ],
        out_specs=pl.BlockSpec((1, ATT_TQ, MLA_WIDTH), lambda i, j: (i, j, 0)),
        out_shape=jax.ShapeDtypeStruct((b, nq * ATT_TQ, MLA_WIDTH), BF16),
        compiler_params=_cparams(2),
    )(q, kt, v)


def _hgrn_prefix_matrix(reverse):
    c = HG_CHUNK
    t = np.arange(c)[:, None]
    j = np.arange(c)[None, :]
    if not reverse:
        blocks = [j <= t, j > t]
    else:
        blocks = [j >= t, j < t]
    for m in HG_LEVELS:
        half = m // 2
        mid = (t // m) * m + half
        if not reverse:
            is_q = (t % m) >= half
            blocks.append(np.where(is_q, (j >= mid) & (j <= t), (j > t) & (j < mid)))
        else:
            is_q = (t % m) < half
            blocks.append(np.where(is_q, (j >= t) & (j < mid), (j >= mid) & (j < t)))
    return np.concatenate(blocks, axis=0).astype(np.float32)


def _hgrn_masks(reverse):
    c = HG_CHUNK
    t = lax.broadcasted_iota(jnp.int32, (c, c), 0)
    s = lax.broadcasted_iota(jnp.int32, (c, c), 1)
    tq = lax.broadcasted_iota(jnp.int32, (c, 1), 0)
    masks = []
    for m in HG_LEVELS:
        half = m // 2
        shift = int(math.log2(m))
        t_hi = (t & (m - 1)) >= half
        s_hi = (s & (m - 1)) >= half
        same = (t >> shift) == (s >> shift)
        if not reverse:
            pair = jnp.where(same, jnp.where(t_hi, jnp.where(s_hi, 0.0, 1.0), 0.0), 0.0)
            is_q = (tq & (m - 1)) >= half
        else:
            pair = jnp.where(same, jnp.where(t_hi, 0.0, jnp.where(s_hi, 1.0, 0.0)), 0.0)
            is_q = (tq & (m - 1)) < half
        masks.append((pair > 0.5, is_q))
    return masks, t == s


def _hgrn_chunk(reverse, masks, eye, a_mat, lf, q, k, v, st):
    c = HG_CHUNK
    hi = lf.astype(BF16)
    r1 = lf - hi.astype(F32)
    mid = r1.astype(BF16)
    lo = (r1 - mid.astype(F32)).astype(BF16)
    e3 = _dot(a_mat, jnp.concatenate([hi, mid, lo], axis=1))
    e = e3[:, :LANES] + e3[:, LANES:2 * LANES] + e3[:, 2 * LANES:]
    cum = e[0:c]
    rem = e[c:2 * c]
    total = cum[0:1] if reverse else cum[c - 1:c]
    qf = q.astype(F32)
    kf = k.astype(F32)
    o = _dot_nt((qf * jnp.exp(cum)).astype(BF16), st.astype(BF16))
    scores = jnp.where(eye, _dot_nt(q, k), 0.0)
    for li, (pair, is_q) in enumerate(masks):
        x = (jnp.where(is_q, qf, kf) * jnp.exp(e[(2 + li) * c:(3 + li) * c])).astype(BF16)
        scores = scores + jnp.where(pair, _dot_nt(x, x), 0.0)
    o = o + _dot(scores.astype(BF16), v)
    ks = (kf * jnp.exp(rem)).astype(BF16)
    st_new = st * jnp.exp(total) + _dot(v.astype(F32).T.astype(BF16), ks)
    return o, st_new


def _hgrn_scan_body(reverse, a_ref, lf_ref, q_ref, k_ref, v_ref, st_ref, emit):
    masks, eye = _hgrn_masks(reverse)
    a_mat = a_ref[...]
    n_chunks = HG_BLOCK // HG_CHUNK

    @pl.when(pl.program_id(1) == 0)
    def _():
        st_ref[...] = jnp.zeros_like(st_ref)

    def step(i, carry):
        ci = (n_chunks - 1 - i) if reverse else i
        rows = pl.ds(pl.multiple_of(ci * HG_CHUNK, HG_CHUNK), HG_CHUNK)
        for h in range(HG_HEADS):
            cols = slice(h * HG_K_DIM, (h + 1) * HG_K_DIM)
            o, st_new = _hgrn_chunk(reverse, masks, eye, a_mat, lf_ref[0, rows, cols], q_ref[0, rows, cols],
                                    k_ref[0, rows, cols], v_ref[0, rows, cols], st_ref[h])
            st_ref[h] = st_new
            emit(rows, cols, o)
        return carry

    lax.fori_loop(0, n_chunks, step, 0)


def _hgrn_fwd_kernel(a_ref, lf_ref, q_ref, k_ref, v_ref, o_ref, st_ref):
    def emit(rows, cols, o):
        o_ref[0, rows, cols] = o

    _hgrn_scan_body(False, a_ref, lf_ref, q_ref, k_ref, v_ref, st_ref, emit)


def _hgrn_bwd_kernel(a_ref, lf_ref, q_ref, k_ref, v_ref, of_ref, gate_ref, gn_ref, o_ref, st_ref):
    def emit(rows, cols, o):
        o_ref[0, rows, cols] = (_rms(o + of_ref[0, rows, cols], gn_ref[...])
                                * gate_ref[0, rows, cols].astype(F32)).astype(BF16)

    _hgrn_scan_body(True, a_ref, lf_ref, q_ref, k_ref, v_ref, st_ref, emit)


def _hgrn_scan(reverse, lf, q, k, v, ctx_blocks, extra=()):
    b, t, w = lf.shape
    nb = t // HG_BLOCK
    a_mat = jnp.asarray(_hgrn_prefix_matrix(reverse), BF16)
    if reverse:
        blk = lambda j: jnp.where(j < ctx_blocks, ctx_blocks - 1 - j, nb - 1 - (j - ctx_blocks))
    else:
        blk = lambda j: j
    tok = pl.BlockSpec((1, HG_BLOCK, w), lambda i, j: (i, blk(j), 0))
    in_specs = [_const_spec(a_mat.shape), tok, tok, tok, tok]
    args = [a_mat, lf, q, k, v]
    if reverse:
        o_f, gate, gnorm = extra
        in_specs += [tok, tok, _const_spec((1, HG_V_DIM))]
        args += [o_f, gate, gnorm.reshape(1, HG_V_DIM)]
    return pl.pallas_call(
        _hgrn_bwd_kernel if reverse else _hgrn_fwd_kernel,
        grid=(b, nb),
        in_specs=in_specs,
        out_specs=tok,
        out_shape=jax.ShapeDtypeStruct((b, t, w), BF16 if reverse else F32),
        scratch_shapes=[pltpu.VMEM((HG_HEADS, HG_V_DIM, HG_K_DIM), F32)],
        compiler_params=_cparams(2),
    )(*args)


def _odd_out_kernel(x_ref, a_ref, g_ref, mod_ref, ng_ref, wo_ref, xo_ref, h2_ref):
    y = _dot(a_ref[0], wo_ref[:MLA_WIDTH, :]) + _dot(g_ref[0], wo_ref[MLA_WIDTH:, :])
    _residual_and_prenorm(x_ref[0], y, mod_ref, ng_ref, xo_ref, h2_ref)


def _odd_out(x, att, hg, mod_l, ng_l, w_out, ctx_tiles, t_off):
    b, t, d = x.shape
    n_out = t // TILE - t_off

    def mod_map(i, j):
        return (jnp.where(j + t_off < ctx_tiles, b, i), 0, 0)

    out_tok = lambda w: pl.BlockSpec((1, TILE, w), lambda i, j: (i, j, 0))
    return pl.pallas_call(
        _odd_out_kernel,
        grid=(b, n_out),
        in_specs=[
            pl.BlockSpec((1, TILE, d), lambda i, j: (i, j + t_off, 0)),
            out_tok(MLA_WIDTH),
            pl.BlockSpec((1, TILE, HG_WIDTH), lambda i, j: (i, j + t_off, 0)),
            pl.BlockSpec((1, 6, d), mod_map),
            _const_spec((4, d)),
            _const_spec((MLA_WIDTH + HG_WIDTH, d)),
        ],
        out_specs=[out_tok(d), out_tok(d)],
        out_shape=[
            jax.ShapeDtypeStruct((b, n_out * TILE, d), F32),
            jax.ShapeDtypeStruct((b, n_out * TILE, d), BF16),
        ],
        compiler_params=_cparams(2),
    )(x, att, hg, mod_l, ng_l, w_out)


def _rope_tables(n_ctx, n_lat, copies):
    rows = n_lat // GRID_W
    pos = jnp.stack([jnp.repeat(jnp.arange(rows), GRID_W), jnp.tile(jnp.arange(GRID_W), rows)], axis=-1)
    axis_dim = DIFF_QK_DIM // 2
    inv_freq = ROPE_THETA ** (-jnp.arange(0, axis_dim, 2, dtype=F32) / axis_dim)
    ang = pos.astype(F32)[..., None] * inv_freq
    cos = jnp.cos(ang)
    sin = jnp.sin(ang)
    cos64 = jnp.concatenate([cos[:, 0], cos[:, 0], cos[:, 1], cos[:, 1]], axis=-1)
    sin64 = jnp.concatenate([-sin[:, 0], sin[:, 0], -sin[:, 1], sin[:, 1]], axis=-1)
    cos64 = jnp.concatenate([jnp.ones((n_ctx, 64), F32), cos64], axis=0)
    sin64 = jnp.concatenate([jnp.zeros((n_ctx, 64), F32), sin64], axis=0)
    return cos64, sin64


def kernel(x, c, ctx, c_ctx, ada_w, ada_b, norm_g, mix_w_out, ffn_w_gate, ffn_w_up, ffn_conv_w, ffn_conv_b,
           ffn_w_down, ev_w_in, pool_w, pool_scale, diff_lambda, diff_subln, od_w_in, mla_q_norm, mla_w_uq,
           mla_kv_norm, mla_w_ukv, hgrn_norm, hgrn_lb):
    b, n_lat, d = x.shape
    n_ctx = ctx.shape[1]
    depth = ada_w.shape[0]
    assert d == D_MODEL and n_ctx % TILE == 0 and n_lat % TILE == 0 and n_lat % GRID_W == 0
    ctx_tiles = n_ctx // TILE

    rows = -(-(b + 1) // 8) * 8
    cond = jnp.concatenate([c, c_ctx[None, :], jnp.zeros((rows - b - 1, d), F32)], axis=0)
    mod = _modulation(cond, ada_w, ada_b).reshape(depth, rows, 6, d)

    cos64, sin64 = _rope_tables(n_ctx, n_lat, 1)
    cos_diff = jnp.tile(cos64, (1, 2 * DIFF_HEADS))
    sin_diff = jnp.tile(sin64, (1, 2 * DIFF_HEADS))
    pad = ((0, 0), (0, LANES - MLA_ROPE_DIM))
    cos_mla = jnp.pad(cos64, pad, constant_values=1.0)
    sin_mla = jnp.pad(sin64, pad)

    xs = jnp.concatenate([ctx, x], axis=1)
    for layer in range(depth):
        last = layer == depth - 1
        j = layer // 2
        t_off = ctx_tiles if last else 0
        mod_l = mod[layer]
        ng_l = norm_g[layer]
        w_out = mix_w_out[layer].astype(BF16)
        if layer % 2 == 0:
            lam_init = 0.8 - 0.6 * math.exp(-0.3 * layer)
            u, q, kt, v = _even_in(xs, mod_l, ng_l, cos_diff, sin_diff, ev_w_in[j].astype(BF16), ctx_tiles)
            att = _diff_attn(q, kt, v, diff_lambda[j], diff_subln[j], lam_init, n_ctx)
            x_mid, h2 = _even_out(xs, u, att, mod_l, ng_l, pool_w[j].astype(BF16), pool_scale[j], w_out,
                                  ctx_tiles, t_off)
        else:
            (q, kt, v, hq, hk_f, hk_b, lf_f, lf_b, hv, hgate) = _odd_in(
                xs, mod_l, ng_l, cos_mla, sin_mla, od_w_in[j].astype(BF16), mla_q_norm[j],
                mla_w_uq[j].astype(BF16), mla_kv_norm[j], mla_w_ukv[j].astype(BF16), hgrn_lb, layer, ctx_tiles)
            att = _mla_attn(q, kt, v, n_ctx, t_off * (TILE // ATT_TQ))
            o_f = _hgrn_scan(False, lf_f, hq, hk_f, hv, n_ctx // HG_BLOCK)
            hg = _hgrn_scan(True, lf_b, hq, hk_b, hv, n_ctx // HG_BLOCK, (o_f, hgate, hgrn_norm[j]))
            x_mid, h2 = _odd_out(xs, att, hg, mod_l, ng_l, w_out, ctx_tiles, t_off)
        xs = _ffn(x_mid, h2, mod_l, ng_l, ffn_w_gate[layer].astype(BF16), ffn_w_up[layer].astype(BF16),
                  ffn_conv_w[layer], ffn_conv_b[layer], ffn_w_down[layer].astype(BF16),
                  0 if last else ctx_tiles)
    return xs
```

```python
import functools
import math

import jax
import jax.numpy as jnp
import numpy as np
from jax import lax
from jax.experimental import pallas as pl
from jax.experimental.pallas import tpu as pltpu

F32 = jnp.float32
BF16 = jnp.bfloat16

D_MODEL = 1024
GRID_W = 64
RMS_EPS = 1e-6
ROPE_THETA = 10000.0
LOG2E = math.log2(math.e)

POOL_WINDOWS = (2, 4, 8, 16)
POOL_GROUPS = 4
POOL_WIDTH = 512
POOL_GROUP_DIM = POOL_WIDTH // POOL_GROUPS

DIFF_HEADS = 4
DIFF_QK_DIM = 64
DIFF_V_DIM = 128
DIFF_QK_WIDTH = 2 * DIFF_HEADS * DIFF_QK_DIM
DIFF_WIDTH = DIFF_HEADS * DIFF_V_DIM

MLA_HEADS = 4
MLA_Q_RANK = 512
MLA_KV_RANK = 256
MLA_NOPE_DIM = 128
MLA_ROPE_DIM = 64
MLA_V_DIM = 128
MLA_QK_PAD = 256
MLA_WIDTH = MLA_HEADS * MLA_V_DIM

HG_HEADS = 4
HG_K_DIM = 128
HG_V_DIM = 128
HG_QK_WIDTH = HG_HEADS * HG_K_DIM
HG_WIDTH = HG_HEADS * HG_V_DIM
HG_CHUNK = 64
HG_LEVELS = (64, 32, 16, 8, 4, 2)

D_FF = 2816
FF_CHUNK = 256

LANES = 128
SUBLANES = 8
BF16_SUBLANES = 16
MXU_TILE = 256

TILE = 256
MAX_GROUP = 4
ODD_IN_GROUP = 2
DIFF_TQ = 128
MLA_TQ = 256
HG_BLOCK = 256
HALO = BF16_SUBLANES
VMEM_LIMIT = 56 * 1024 * 1024


def _cparams(n_axes):
    return pltpu.CompilerParams(
        dimension_semantics=("arbitrary",) * n_axes, vmem_limit_bytes=VMEM_LIMIT)


def _dot(a, b):
    return jnp.dot(a, b, preferred_element_type=F32)


def _dot_nt(a, b):
    return lax.dot_general(a, b, (((1,), (1,)), ((), ())), preferred_element_type=F32)


def _rms(x, g):
    return x * lax.rsqrt(jnp.mean(x * x, axis=-1, keepdims=True) + RMS_EPS) * g


def _silu(x):
    return x * jax.nn.sigmoid(x)


def _const_spec(shape):
    zeros = (0,) * len(shape)
    return pl.BlockSpec(shape, lambda *_: zeros, pipeline_mode=pl.Buffered(1))


def _group_size(b, max_group):
    return max(g for g in range(1, max_group + 1) if b % g == 0)


def _group_mod(mod_l, b, group):
    return jnp.concatenate([mod_l[:b], jnp.broadcast_to(mod_l[b:b + 1], (group,) + mod_l.shape[1:])], axis=0)


def _rope(x, cos, sin_signed):
    n = x.shape[-1]
    lane = lax.broadcasted_iota(jnp.int32, x.shape, 1)
    first_half = (lane & 31) < 16
    partner = jnp.where(first_half, pltpu.roll(x, n - 16, 1), pltpu.roll(x, 16, 1))
    return x * cos + partner * sin_signed


def _prenorm_rows(x_ref, mod_ref, ng_ref, shift_row, scale_row, norm_row):
    parts = []
    for g in range(x_ref.shape[0]):
        h = (_rms(x_ref[g], ng_ref[norm_row:norm_row + 1, :]) * (1.0 + mod_ref[g, scale_row:scale_row + 1, :])
             + mod_ref[g, shift_row:shift_row + 1, :])
        parts.append(h.astype(BF16))
    return parts[0] if len(parts) == 1 else jnp.concatenate(parts, axis=0)


def _mod_kernel(c_ref, w_ref, b_ref, o_ref):
    s = _silu(c_ref[...])
    w = w_ref[0]
    s_hi = s.astype(BF16)
    s_lo = (s - s_hi.astype(F32)).astype(BF16)
    w_hi = w.astype(BF16)
    w_lo = (w - w_hi.astype(F32)).astype(BF16)
    o_ref[0] = _dot(s_hi, w_hi) + _dot(s_hi, w_lo) + _dot(s_lo, w_hi) + b_ref[0]


def _modulation(cond, ada_w, ada_b):
    depth, d, n = ada_w.shape
    rows = cond.shape[0]
    tn = 1536
    return pl.pallas_call(
        _mod_kernel,
        grid=(depth, n // tn),
        in_specs=[
            pl.BlockSpec((rows, d), lambda l, j: (0, 0)),
            pl.BlockSpec((1, d, tn), lambda l, j: (l, 0, j)),
            pl.BlockSpec((1, 1, tn), lambda l, j: (l, 0, j)),
        ],
        out_specs=pl.BlockSpec((1, rows, tn), lambda l, j: (l, 0, j)),
        out_shape=jax.ShapeDtypeStruct((depth, rows, n), F32),
        compiler_params=_cparams(2),
    )(cond, ada_w, ada_b.reshape(depth, 1, n))


def _even_in_kernel(x_ref, mod_ref, ng_ref, cos_ref, sin_ref, w_ref, u_ref, qt_ref, k_ref, vt_ref):
    p = _dot(_prenorm_rows(x_ref, mod_ref, ng_ref, 0, 1, 0), w_ref[...])
    cos = cos_ref[...]
    sin = sin_ref[...]
    o_q = POOL_WIDTH
    o_k = o_q + DIFF_QK_WIDTH
    o_v = o_k + DIFF_QK_WIDTH
    for g in range(x_ref.shape[0]):
        rows = slice(g * TILE, (g + 1) * TILE)
        u_ref[g] = p[rows, :o_q].astype(BF16)
        qt_ref[g] = (_rope(p[rows, o_q:o_k], cos, sin) * (DIFF_QK_DIM ** -0.5 * LOG2E)).T.astype(BF16)
        k_ref[g] = _rope(p[rows, o_k:o_v], cos, sin).astype(BF16)
        vt_ref[g] = p[rows, o_v:].T.astype(BF16)


def _even_in(x, mod_g, ng_l, cos, sin, w_in, ctx_tiles, group):
    b, t, d = x.shape
    nt = t // TILE
    n_in = w_in.shape[1]

    def mod_map(i, j):
        return (jnp.where(j < ctx_tiles, b // group, i), 0, 0)

    tok = lambda w: pl.BlockSpec((group, TILE, w), lambda i, j: (i, j, 0))
    tok_t = lambda w: pl.BlockSpec((group, w, TILE), lambda i, j: (i, 0, j))
    return pl.pallas_call(
        _even_in_kernel,
        grid=(b // group, nt),
        in_specs=[
            tok(d),
            pl.BlockSpec((group, 6, d), mod_map),
            _const_spec((4, d)),
            pl.BlockSpec((TILE, DIFF_QK_WIDTH), lambda i, j: (j, 0)),
            pl.BlockSpec((TILE, DIFF_QK_WIDTH), lambda i, j: (j, 0)),
            _const_spec((d, n_in)),
        ],
        out_specs=[tok(POOL_WIDTH), tok_t(DIFF_QK_WIDTH), tok(DIFF_QK_WIDTH), tok_t(DIFF_WIDTH)],
        out_shape=[
            jax.ShapeDtypeStruct((b, t, POOL_WIDTH), BF16),
            jax.ShapeDtypeStruct((b, DIFF_QK_WIDTH, t), BF16),
            jax.ShapeDtypeStruct((b, t, DIFF_QK_WIDTH), BF16),
            jax.ShapeDtypeStruct((b, DIFF_WIDTH, t), BF16),
        ],
        compiler_params=_cparams(2),
    )(x, mod_g, ng_l, cos, sin, w_in)


def _key_halves(nk):
    if nk <= MXU_TILE:
        return ((0, nk),)
    first = -(-(nk // 2) // MXU_TILE) * MXU_TILE
    return ((0, first), (first, nk))


def _attend_t(n_heads, nk, scores_operands, values_t, s_ref, e_ref, finish):
    def scores(h):
        keys, q_t = scores_operands(h)
        tops = []
        for a, b in _key_halves(nk):
            s = _dot(keys[a:b, :], q_t)
            s_ref[h % 2, a:b, :] = s
            tops.append(jnp.max(s, axis=0, keepdims=True))
        return functools.reduce(jnp.maximum, tops)

    top = scores(0)
    for h in range(n_heads):
        nxt = scores(h + 1) if h + 1 < n_heads else None
        e = jnp.exp2(s_ref[h % 2, :nk, :] - top)
        e_ref[h % 2, :nk, :] = e.astype(BF16)
        l = jnp.sum(e, axis=0, keepdims=True)
        vt = values_t(h)
        o_t = sum(_dot(vt[:, a:b], e_ref[h % 2, a:b, :]) for a, b in _key_halves(nk))
        finish(h, o_t, l)
        top = nxt


def _diff_attn_kernel(lam_init, n_ctx, qt_ref, k_ref, vt_ref, lam_ref, g_ref, o_ref, s_ref, e_ref):
    lv = lam_ref[...]
    lam = (jnp.exp(jnp.sum(lv[0:1] * lv[1:2], axis=-1, keepdims=True))
           - jnp.exp(jnp.sum(lv[2:3] * lv[3:4], axis=-1, keepdims=True)) + lam_init)
    gain = g_ref[...] * (1.0 - lam_init)
    row = lax.broadcasted_iota(jnp.int32, (LANES, DIFF_TQ), 0)

    def attend(nk):
        def scores_operands(h):
            cols = slice(h * LANES, (h + 1) * LANES)
            qp = qt_ref[0, cols, :]
            q_bd = jnp.concatenate(
                [jnp.where(row < DIFF_QK_DIM, qp, 0), jnp.where(row >= DIFF_QK_DIM, qp, 0)], axis=1)
            return k_ref.at[0, :nk, cols], q_bd

        def finish(h, o_t, l):
            r = 1.0 / l
            o = o_t[:, :DIFF_TQ] * r[:, :DIFF_TQ] - o_t[:, DIFF_TQ:] * (lam * r[:, DIFF_TQ:])
            o = o * lax.rsqrt(jnp.mean(o * o, axis=0, keepdims=True) + RMS_EPS) * gain
            o_ref[0, :, h * LANES:(h + 1) * LANES] = o.T.astype(BF16)

        _attend_t(DIFF_HEADS, nk, scores_operands,
                  lambda h: vt_ref.at[0, h * LANES:(h + 1) * LANES, :nk], s_ref, e_ref, finish)

    is_ctx = pl.program_id(1) < n_ctx // DIFF_TQ

    @pl.when(is_ctx)
    def _():
        attend(n_ctx)

    @pl.when(jnp.logical_not(is_ctx))
    def _():
        attend(k_ref.shape[1])


def _diff_attn(qt, k, vt, lam_vec, subln, lam_init, n_ctx):
    b, t, _ = k.shape
    per_sample = lambda shape: pl.BlockSpec(shape, lambda i, j: (i, 0, 0), pipeline_mode=pl.Buffered(1))
    return pl.pallas_call(
        functools.partial(_diff_attn_kernel, lam_init, n_ctx),
        grid=(b, t // DIFF_TQ),
        in_specs=[
            pl.BlockSpec((1, DIFF_QK_WIDTH, DIFF_TQ), lambda i, j: (i, 0, j)),
            per_sample((1, t, DIFF_QK_WIDTH)),
            per_sample((1, DIFF_WIDTH, t)),
            _const_spec((4, DIFF_QK_DIM)),
            _const_spec((DIFF_V_DIM, 1)),
        ],
        out_specs=pl.BlockSpec((1, DIFF_TQ, DIFF_WIDTH), lambda i, j: (i, j, 0)),
        out_shape=jax.ShapeDtypeStruct((b, t, DIFF_WIDTH), BF16),
        scratch_shapes=[pltpu.VMEM((2, t, 2 * DIFF_TQ), F32), pltpu.VMEM((2, t, 2 * DIFF_TQ), BF16)],
        compiler_params=_cparams(2),
    )(qt, k, vt, lam_vec, subln.reshape(DIFF_V_DIM, 1))


def _residual_and_prenorm(x_ref, y, mod_ref, ng_ref, xo_ref, h2_ref):
    for g in range(x_ref.shape[0]):
        xn = x_ref[g] + mod_ref[g, 2:3, :] * _rms(y[g * TILE:(g + 1) * TILE], ng_ref[1:2, :])
        xo_ref[g] = xn
        h2 = _rms(xn, ng_ref[2:3, :]) * (1.0 + mod_ref[g, 4:5, :]) + mod_ref[g, 3:4, :]
        h2_ref[g] = h2.astype(BF16)


def _even_out_kernel(t_off, seg_tiles, n_tiles, x_ref, u_ref, up_ref, un_ref, a_ref, mod_ref, ng_ref,
                     pw_ref, ps_ref, wo_ref, xo_ref, h2_ref, ext_ref):
    j = pl.program_id(1) + t_off
    prev_ok = jnp.logical_and(j != 0, j != seg_tiles)
    next_ok = jnp.logical_and(j != n_tiles - 1, j != seg_tiles - 1)
    lo = jnp.where(prev_ok, -HALO, 0)
    hi = jnp.where(next_ok, TILE + HALO, TILE)
    row = lax.broadcasted_iota(jnp.int32, (TILE, 1), 0)
    group = x_ref.shape[0]
    pooled = []
    for g in range(group):
        ext_ref[g, 0:HALO, :] = jnp.where(prev_ok, up_ref[g].astype(F32), 0.0)
        ext_ref[g, HALO:HALO + TILE, :] = u_ref[g].astype(F32)
        ext_ref[g, HALO + TILE:, :] = jnp.where(next_ok, un_ref[g].astype(F32), 0.0)
        parts = []
        for gidx, win in enumerate(POOL_WINDOWS):
            half = win // 2
            cols = slice(gidx * POOL_GROUP_DIM, (gidx + 1) * POOL_GROUP_DIM)
            acc = ext_ref[g, HALO - half:HALO - half + TILE, cols]
            for off in range(-half + 1, half):
                acc = acc + ext_ref[g, HALO + off:HALO + off + TILE, cols]
            cnt = jnp.minimum(row + half, hi) - jnp.maximum(row - half, lo)
            parts.append((acc / cnt.astype(F32) - ext_ref[g, HALO:HALO + TILE, cols]).astype(BF16))
        pooled.append(parts)
    yp = jnp.concatenate(
        [_dot(jnp.concatenate([pooled[g][gidx] for g in range(group)], axis=0), pw_ref[gidx])
         for gidx in range(POOL_GROUPS)], axis=1) * ps_ref[...]
    att = jnp.concatenate([a_ref[g] for g in range(group)], axis=0)
    y = _dot(yp.astype(BF16), wo_ref[:POOL_WIDTH, :]) + _dot(att, wo_ref[POOL_WIDTH:, :])
    _residual_and_prenorm(x_ref, y, mod_ref, ng_ref, xo_ref, h2_ref)


def _even_out(x, u, att, mod_g, ng_l, pool_w, pool_scale, w_out, ctx_tiles, t_off, group):
    b, t, d = x.shape
    nt = t // TILE
    nh = TILE // HALO
    n_out = nt - t_off

    def mod_map(i, j):
        return (jnp.where(j + t_off < ctx_tiles, b // group, i), 0, 0)

    tok = lambda w: pl.BlockSpec((group, TILE, w), lambda i, j: (i, j + t_off, 0))
    out_tok = lambda: pl.BlockSpec((group, TILE, d), lambda i, j: (i, j, 0))
    return pl.pallas_call(
        functools.partial(_even_out_kernel, t_off, ctx_tiles, nt),
        grid=(b // group, n_out),
        in_specs=[
            tok(d),
            tok(POOL_WIDTH),
            pl.BlockSpec((group, HALO, POOL_WIDTH), lambda i, j: (i, jnp.maximum((j + t_off) * nh - 1, 0), 0)),
            pl.BlockSpec((group, HALO, POOL_WIDTH),
                         lambda i, j: (i, jnp.minimum((j + t_off + 1) * nh, nt * nh - 1), 0)),
            tok(DIFF_WIDTH),
            pl.BlockSpec((group, 6, d), mod_map),
            _const_spec((4, d)),
            _const_spec((POOL_GROUPS, POOL_GROUP_DIM, POOL_GROUP_DIM)),
            _const_spec((1, POOL_WIDTH)),
            _const_spec((POOL_WIDTH + DIFF_WIDTH, d)),
        ],
        out_specs=[out_tok(), out_tok()],
        out_shape=[
            jax.ShapeDtypeStruct((b, n_out * TILE, d), F32),
            jax.ShapeDtypeStruct((b, n_out * TILE, d), BF16),
        ],
        scratch_shapes=[pltpu.VMEM((group, TILE + 2 * HALO, POOL_WIDTH), F32)],
        compiler_params=_cparams(2),
    )(x, u, u, u, att, mod_g, ng_l, pool_w, pool_scale.reshape(1, POOL_WIDTH), w_out)


def _ffn_kernel(seg_tiles, n_tiles, x_ref, h_ref, hp_ref, hn_ref, mod_ref, ng_ref, wg_ref, wu_ref,
                cw_ref, cb_ref, wd_ref, o_ref, a_ref, act_ref):
    j = pl.program_id(1)
    prev_ok = jnp.logical_and(j != 0, j != seg_tiles)
    next_ok = jnp.logical_and(j != n_tiles - 1, j != seg_tiles - 1)
    group = x_ref.shape[0]
    ext = TILE + 2 * HALO
    zero = jnp.zeros((HALO, h_ref.shape[2]), BF16)
    pieces = []
    for g in range(group):
        pieces += [jnp.where(prev_ok, hp_ref[g], zero), h_ref[g], jnp.where(next_ok, hn_ref[g], zero)]
    h_ext = jnp.concatenate(pieces, axis=0)
    h = jnp.concatenate([h_ref[g] for g in range(group)], axis=0) if group > 1 else h_ref[0]
    for c in range(D_FF // FF_CHUNK):
        cols = slice(c * FF_CHUNK, (c + 1) * FF_CHUNK)
        a_ref[...] = _dot(h_ext, wg_ref[:, cols])
        up = _dot(h, wu_ref[:, cols])
        for g in range(group):
            base = g * ext + HALO
            a = (a_ref[base - 1:base - 1 + TILE, :] * cw_ref[0:1, cols]
                 + a_ref[base:base + TILE, :] * cw_ref[1:2, cols]
                 + a_ref[base + 1:base + 1 + TILE, :] * cw_ref[2:3, cols] + cb_ref[:, cols])
            act_ref[g * TILE:(g + 1) * TILE, cols] = (_silu(a) * up[g * TILE:(g + 1) * TILE]).astype(BF16)
    f = _dot(act_ref[...], wd_ref[...])
    for g in range(group):
        o_ref[g] = x_ref[g] + mod_ref[g, 5:6, :] * _rms(f[g * TILE:(g + 1) * TILE], ng_ref[3:4, :])


def _ffn(x, h2, mod_g, ng_l, w_gate, w_up, conv_w, conv_b, w_down, ctx_tiles, group):
    b, t, d = x.shape
    nt = t // TILE
    nh = TILE // HALO

    def mod_map(i, j):
        return (jnp.where(j < ctx_tiles, b // group, i), 0, 0)

    tok = lambda: pl.BlockSpec((group, TILE, d), lambda i, j: (i, j, 0))
    return pl.pallas_call(
        functools.partial(_ffn_kernel, ctx_tiles, nt),
        grid=(b // group, nt),
        in_specs=[
            tok(),
            tok(),
            pl.BlockSpec((group, HALO, d), lambda i, j: (i, jnp.maximum(j * nh - 1, 0), 0)),
            pl.BlockSpec((group, HALO, d), lambda i, j: (i, jnp.minimum((j + 1) * nh, nt * nh - 1), 0)),
            pl.BlockSpec((group, 6, d), mod_map),
            _const_spec((4, d)),
            _const_spec((d, D_FF)),
            _const_spec((d, D_FF)),
            _const_spec((3, D_FF)),
            _const_spec((1, D_FF)),
            _const_spec((D_FF, d)),
        ],
        out_specs=tok(),
        out_shape=jax.ShapeDtypeStruct((b, t, d), F32),
        scratch_shapes=[pltpu.VMEM((group * (TILE + 2 * HALO), FF_CHUNK), F32),
                        pltpu.VMEM((group * TILE, D_FF), BF16)],
        compiler_params=_cparams(2),
    )(x, h2, h2, h2, mod_g, ng_l, w_gate, w_up, conv_w, conv_b.reshape(1, D_FF), w_down)


def _odd_in_kernel(layer, x_ref, mod_ref, ng_ref, cos_ref, sin_ref, wq_ref, wkv_ref, wkr_ref, wh_ref,
                   qn_ref, wuq_ref, kvn_ref, wukv_ref, lb_ref,
                   qt_ref, kc_ref, vt_ref, hq_ref, hkf_ref, hkb_ref, lff_ref, lfb_ref, hv_ref, hg_ref):
    group = x_ref.shape[0]
    hb = _prenorm_rows(x_ref, mod_ref, ng_ref, 0, 1, 0)
    cos = cos_ref[...]
    sin = sin_ref[...]

    cq = _rms(_dot(hb, wq_ref[...]), qn_ref[...])
    q = _dot(cq.astype(BF16), wuq_ref[...]) * ((MLA_NOPE_DIM + MLA_ROPE_DIM) ** -0.5 * LOG2E)
    ckv = _rms(_dot(hb, wkv_ref[...]), kvn_ref[...])
    kv = _dot(ckv.astype(BF16), wukv_ref[...])
    kr_all = _dot(hb, wkr_ref[...])
    for g in range(group):
        rows = slice(g * TILE, (g + 1) * TILE)
        kr = _rope(kr_all[rows], cos, sin).astype(BF16)
        for hd in range(MLA_HEADS):
            qb = hd * MLA_QK_PAD
            q_h = jnp.concatenate(
                [q[rows, qb:qb + MLA_NOPE_DIM], _rope(q[rows, qb + MLA_NOPE_DIM:qb + MLA_QK_PAD], cos, sin)], axis=1)
            qt_ref[g, hd] = q_h.T.astype(BF16)
            kb = hd * (MLA_NOPE_DIM + MLA_V_DIM)
            kc_ref[g, hd, :, :MLA_NOPE_DIM] = kv[rows, kb:kb + MLA_NOPE_DIM].astype(BF16)
            kc_ref[g, hd, :, MLA_NOPE_DIM:] = kr
            vt_ref[g, hd * MLA_V_DIM:(hd + 1) * MLA_V_DIM, :] = kv[
                rows, kb + MLA_NOPE_DIM:kb + MLA_NOPE_DIM + MLA_V_DIM].T.astype(BF16)

    gates = _dot(hb, wh_ref[...])
    w = HG_QK_WIDTH
    lbs = []
    for direction in range(2):
        lrows = [lb_ref[direction, i:i + 1, :] for i in range(lb_ref.shape[1])]
        top = functools.reduce(jnp.maximum, lrows)
        ex = [jnp.exp(r - top) for r in lrows]
        lbs.append(sum(ex[1:layer + 1], jnp.zeros_like(top)) / sum(ex))
    for g in range(group):
        rows = slice(g * TILE, (g + 1) * TILE)
        hq_ref[g] = _silu(gates[rows, :w]).astype(BF16)
        hv_ref[g] = gates[rows, 3 * w:4 * w].astype(BF16)
        hg_ref[g] = _silu(gates[rows, 4 * w:]).astype(BF16)
        for direction, (k_ref, lf_ref) in enumerate(((hkf_ref, lff_ref), (hkb_ref, lfb_ref))):
            lb = lbs[direction]
            f = lb + (1.0 - lb) * jax.nn.sigmoid(gates[rows, (1 + direction) * w:(2 + direction) * w])
            k_ref[g] = (1.0 - f).astype(BF16)
            lf_ref[g] = jnp.log(f)


def _odd_in(x, mod_g, ng_l, cos, sin, w_in, q_norm, w_uq, kv_norm, w_ukv, hgrn_lb, layer, ctx_tiles, group):
    b, t, d = x.shape
    nt = t // TILE
    o1 = MLA_Q_RANK
    o2 = o1 + MLA_KV_RANK
    o3 = o2 + MLA_ROPE_DIM
    w_q = w_in[:, :o1]
    w_kv = w_in[:, o1:o2]
    w_kr = jnp.pad(w_in[:, o2:o3], ((0, 0), (0, LANES - MLA_ROPE_DIM)))
    w_h = w_in[:, o3:]
    qk = MLA_NOPE_DIM + MLA_ROPE_DIM
    w_uq_pad = jnp.pad(w_uq.reshape(MLA_Q_RANK, MLA_HEADS, qk),
                       ((0, 0), (0, 0), (0, MLA_QK_PAD - qk))).reshape(MLA_Q_RANK, MLA_HEADS * MLA_QK_PAD)

    def mod_map(i, j):
        return (jnp.where(j < ctx_tiles, b // group, i), 0, 0)

    tok = lambda w: pl.BlockSpec((group, TILE, w), lambda i, j: (i, j, 0))
    sds = lambda w, dt: jax.ShapeDtypeStruct((b, t, w), dt)
    hw = HG_QK_WIDTH
    return pl.pallas_call(
        functools.partial(_odd_in_kernel, layer),
        grid=(b // group, nt),
        in_specs=[
            tok(d),
            pl.BlockSpec((group, 6, d), mod_map),
            _const_spec((4, d)),
            pl.BlockSpec((TILE, LANES), lambda i, j: (j, 0)),
            pl.BlockSpec((TILE, LANES), lambda i, j: (j, 0)),
            _const_spec(w_q.shape),
            _const_spec(w_kv.shape),
            _const_spec(w_kr.shape),
            _const_spec(w_h.shape),
            _const_spec((1, MLA_Q_RANK)),
            _const_spec(w_uq_pad.shape),
            _const_spec((1, MLA_KV_RANK)),
            _const_spec(w_ukv.shape),
            _const_spec(hgrn_lb.shape),
        ],
        out_specs=[
            pl.BlockSpec((group, MLA_HEADS, MLA_QK_PAD, TILE), lambda i, j: (i, 0, 0, j)),
            pl.BlockSpec((group, MLA_HEADS, TILE, MLA_QK_PAD), lambda i, j: (i, 0, j, 0)),
            pl.BlockSpec((group, MLA_WIDTH, TILE), lambda i, j: (i, 0, j)),
            tok(hw), tok(hw), tok(hw), tok(hw), tok(hw), tok(hw), tok(hw),
        ],
        out_shape=[
            jax.ShapeDtypeStruct((b, MLA_HEADS, MLA_QK_PAD, t), BF16),
            jax.ShapeDtypeStruct((b, MLA_HEADS, t, MLA_QK_PAD), BF16),
            jax.ShapeDtypeStruct((b, MLA_WIDTH, t), BF16),
            sds(hw, BF16), sds(hw, BF16), sds(hw, BF16), sds(hw, F32), sds(hw, F32), sds(hw, BF16),
            sds(hw, BF16),
        ],
        compiler_params=_cparams(2),
    )(x, mod_g, ng_l, cos, sin, w_q, w_kv, w_kr, w_h, q_norm.reshape(1, -1), w_uq_pad,
      kv_norm.reshape(1, -1), w_ukv, hgrn_lb)


def _mla_attn_kernel(n_ctx, q_off, qt_ref, kc_ref, vt_ref, o_ref, s_ref, e_ref):
    def attend(nk):
        def finish(h, o_t, l):
            o_ref[0, :, h * MLA_V_DIM:(h + 1) * MLA_V_DIM] = (o_t * (1.0 / l)).T.astype(BF16)

        _attend_t(MLA_HEADS, nk, lambda h: (kc_ref.at[0, h, :nk, :], qt_ref[0, h]),
                  lambda h: vt_ref.at[0, h * MLA_V_DIM:(h + 1) * MLA_V_DIM, :nk], s_ref, e_ref, finish)

    if q_off * MLA_TQ >= n_ctx:
        attend(kc_ref.shape[2])
    else:
        is_ctx = pl.program_id(1) + q_off < n_ctx // MLA_TQ

        @pl.when(is_ctx)
        def _():
            attend(n_ctx)

        @pl.when(jnp.logical_not(is_ctx))
        def _():
            attend(kc_ref.shape[2])


def _mla_attn(qt, kc, vt, n_ctx, q_off):
    b, _, t, _ = kc.shape
    nq = t // MLA_TQ - q_off
    return pl.pallas_call(
        functools.partial(_mla_attn_kernel, n_ctx, q_off),
        grid=(b, nq),
        in_specs=[
            pl.BlockSpec((1, MLA_HEADS, MLA_QK_PAD, MLA_TQ), lambda i, j: (i, 0, 0, j + q_off)),
            pl.BlockSpec((1, MLA_HEADS, t, MLA_QK_PAD), lambda i, j: (i, 0, 0, 0), pipeline_mode=pl.Buffered(1)),
            pl.BlockSpec((1, MLA_WIDTH, t), lambda i, j: (i, 0, 0), pipeline_mode=pl.Buffered(1)),
        ],
        out_specs=pl.BlockSpec((1, MLA_TQ, MLA_WIDTH), lambda i, j: (i, j, 0)),
        out_shape=jax.ShapeDtypeStruct((b, nq * MLA_TQ, MLA_WIDTH), BF16),
        scratch_shapes=[pltpu.VMEM((2, t, MLA_TQ), F32), pltpu.VMEM((2, t, MLA_TQ), BF16)],
        compiler_params=_cparams(2),
    )(qt, kc, vt)


def _hgrn_triangle(reverse):
    t = np.arange(HG_CHUNK)[:, None]
    j = np.arange(HG_CHUNK)[None, :]
    return ((j >= t) if reverse else (j <= t)).astype(np.float32)


def _hgrn_masks(reverse):
    c = HG_CHUNK
    t = lax.broadcasted_iota(jnp.int32, (c, c), 0)
    s = lax.broadcasted_iota(jnp.int32, (c, c), 1)
    tq = lax.broadcasted_iota(jnp.int32, (c, 1), 0)
    masks = []
    for m in HG_LEVELS:
        half = m // 2
        shift = int(math.log2(m))
        t_hi = (t & (m - 1)) >= half
        s_hi = (s & (m - 1)) >= half
        same = (t >> shift) == (s >> shift)
        if not reverse:
            pair = jnp.where(same, jnp.where(t_hi, jnp.where(s_hi, 0.0, 1.0), 0.0), 0.0)
            is_q = (tq & (m - 1)) >= half
        else:
            pair = jnp.where(same, jnp.where(t_hi, 0.0, jnp.where(s_hi, 1.0, 0.0)), 0.0)
            is_q = (tq & (m - 1)) < half
        masks.append((pair > 0.5, is_q))
    return masks, t == s


def _hgrn_reference_rows(cum, m, reverse):
    c = HG_CHUNK
    half = m // 2
    ref_in_block = half if reverse else half - 1
    if m >= 2 * SUBLANES:
        return jnp.concatenate(
            [jnp.broadcast_to(cum[b0 + ref_in_block:b0 + ref_in_block + 1, :], (m, LANES))
             for b0 in range(0, c, m)], axis=0)
    cum3 = cum.reshape(c // SUBLANES, SUBLANES, LANES)
    r = lax.broadcasted_iota(jnp.int32, cum3.shape, 1)
    out = None
    for b0 in reversed(range(0, SUBLANES, m)):
        pick = jnp.broadcast_to(cum3[:, b0 + ref_in_block:b0 + ref_in_block + 1, :], cum3.shape)
        out = pick if out is None else jnp.where(r < b0 + m, pick, out)
    return out.reshape(c, LANES)


def _hgrn_scan_body(reverse, tri_ref, lf_ref, q_ref, k_ref, v_ref, st_ref, emit):
    masks, eye = _hgrn_masks(reverse)
    tri = tri_ref[...]
    c = HG_CHUNK
    n_chunks = HG_BLOCK // c
    chunk_order = list(range(n_chunks - 1, -1, -1) if reverse else range(n_chunks))
    probs = [(ci, h) for ci in chunk_order for h in range(HG_HEADS)]
    where = lambda ci, h: (slice(ci * c, (ci + 1) * c), slice(h * HG_K_DIM, (h + 1) * HG_K_DIM))

    @pl.when(pl.program_id(1) == 0)
    def _():
        st_ref[...] = jnp.zeros_like(st_ref)

    cums = []
    for p in probs:
        rows, cols = where(*p)
        lf = lf_ref[0, rows, cols]
        hi = lf.astype(BF16)
        r1 = lf - hi.astype(F32)
        mid = r1.astype(BF16)
        lo = (r1 - mid.astype(F32)).astype(BF16)
        c3 = _dot(tri, jnp.concatenate([hi, mid, lo], axis=1))
        cums.append(c3[:, :LANES] + c3[:, LANES:2 * LANES] + c3[:, 2 * LANES:])

    qe, ks, decay, scores = [], [], [], []
    for p, cum in zip(probs, cums):
        rows, cols = where(*p)
        total = cum[0:1] if reverse else cum[c - 1:c]
        q = q_ref[0, rows, cols]
        k = k_ref[0, rows, cols]
        qe.append((q.astype(F32) * jnp.exp(cum)).astype(BF16))
        ks.append((k.astype(F32) * jnp.exp(total - cum)).astype(BF16))
        decay.append(jnp.exp(total))
        scores.append(jnp.where(eye, _dot_nt(q, k), 0.0))

    for m, (pair, is_q) in zip(HG_LEVELS, masks):
        for i, (p, cum) in enumerate(zip(probs, cums)):
            rows, cols = where(*p)
            d = cum - _hgrn_reference_rows(cum, m, reverse)
            qk = jnp.where(is_q, q_ref[0, rows, cols].astype(F32), k_ref[0, rows, cols].astype(F32))
            x = (qk * jnp.exp(jnp.minimum(jnp.where(is_q, d, -d), 0.0))).astype(BF16)
            scores[i] = scores[i] + jnp.where(pair, _dot_nt(x, x), 0.0)

    intra, update = [], []
    for i, p in enumerate(probs):
        rows, cols = where(*p)
        v = v_ref[0, rows, cols]
        intra.append(_dot(scores[i].astype(BF16), v))
        update.append(_dot(v.astype(F32).T.astype(BF16), ks[i]))

    for h in range(HG_HEADS):
        st = st_ref[h]
        for n in range(n_chunks):
            i = n * HG_HEADS + h
            rows, cols = where(*probs[i])
            emit(rows, cols, intra[i] + _dot_nt(qe[i], st.astype(BF16)))
            st = st * decay[i] + update[i]
        st_ref[h] = st


def _hgrn_fwd_kernel(tri_ref, lf_ref, q_ref, k_ref, v_ref, o_ref, st_ref):
    def emit(rows, cols, o):
        o_ref[0, rows, cols] = o

    _hgrn_scan_body(False, tri_ref, lf_ref, q_ref, k_ref, v_ref, st_ref, emit)


def _hgrn_bwd_kernel(tri_ref, lf_ref, q_ref, k_ref, v_ref, of_ref, gate_ref, gn_ref, o_ref, st_ref):
    def emit(rows, cols, o):
        o_ref[0, rows, cols] = (_rms(o + of_ref[0, rows, cols], gn_ref[...])
                                * gate_ref[0, rows, cols].astype(F32)).astype(BF16)

    _hgrn_scan_body(True, tri_ref, lf_ref, q_ref, k_ref, v_ref, st_ref, emit)


def _hgrn_scan(reverse, lf, q, k, v, ctx_blocks, extra=()):
    b, t, w = lf.shape
    nb = t // HG_BLOCK
    tri = jnp.asarray(_hgrn_triangle(reverse), BF16)
    if reverse:
        blk = lambda j: jnp.where(j < ctx_blocks, ctx_blocks - 1 - j, nb - 1 - (j - ctx_blocks))
    else:
        blk = lambda j: j
    tok = pl.BlockSpec((1, HG_BLOCK, w), lambda i, j: (i, blk(j), 0))
    in_specs = [_const_spec(tri.shape), tok, tok, tok, tok]
    args = [tri, lf, q, k, v]
    if reverse:
        o_f, gate, gnorm = extra
        in_specs += [tok, tok, _const_spec((1, HG_V_DIM))]
        args += [o_f, gate, gnorm.reshape(1, HG_V_DIM)]
    return pl.pallas_call(
        _hgrn_bwd_kernel if reverse else _hgrn_fwd_kernel,
        grid=(b, nb),
        in_specs=in_specs,
        out_specs=tok,
        out_shape=jax.ShapeDtypeStruct((b, t, w), BF16 if reverse else F32),
        scratch_shapes=[pltpu.VMEM((HG_HEADS, HG_V_DIM, HG_K_DIM), F32)],
        compiler_params=_cparams(2),
    )(*args)


def _odd_out_kernel(x_ref, a_ref, g_ref, mod_ref, ng_ref, wo_ref, xo_ref, h2_ref):
    group = x_ref.shape[0]
    att = jnp.concatenate([a_ref[g] for g in range(group)], axis=0)
    hg = jnp.concatenate([g_ref[g] for g in range(group)], axis=0)
    y = _dot(att, wo_ref[:MLA_WIDTH, :]) + _dot(hg, wo_ref[MLA_WIDTH:, :])
    _residual_and_prenorm(x_ref, y, mod_ref, ng_ref, xo_ref, h2_ref)


def _odd_out(x, att, hg, mod_g, ng_l, w_out, ctx_tiles, t_off, group):
    b, t, d = x.shape
    n_out = t // TILE - t_off

    def mod_map(i, j):
        return (jnp.where(j + t_off < ctx_tiles, b // group, i), 0, 0)

    out_tok = lambda w: pl.BlockSpec((group, TILE, w), lambda i, j: (i, j, 0))
    return pl.pallas_call(
        _odd_out_kernel,
        grid=(b // group, n_out),
        in_specs=[
            pl.BlockSpec((group, TILE, d), lambda i, j: (i, j + t_off, 0)),
            out_tok(MLA_WIDTH),
            pl.BlockSpec((group, TILE, HG_WIDTH), lambda i, j: (i, j + t_off, 0)),
            pl.BlockSpec((group, 6, d), mod_map),
            _const_spec((4, d)),
            _const_spec((MLA_WIDTH + HG_WIDTH, d)),
        ],
        out_specs=[out_tok(d), out_tok(d)],
        out_shape=[
            jax.ShapeDtypeStruct((b, n_out * TILE, d), F32),
            jax.ShapeDtypeStruct((b, n_out * TILE, d), BF16),
        ],
        compiler_params=_cparams(2),
    )(x, att, hg, mod_g, ng_l, w_out)


def _rope_tables(n_ctx, n_lat):
    rows = n_lat // GRID_W
    pos = jnp.stack([jnp.repeat(jnp.arange(rows), GRID_W), jnp.tile(jnp.arange(GRID_W), rows)], axis=-1)
    axis_dim = DIFF_QK_DIM // 2
    inv_freq = ROPE_THETA ** (-jnp.arange(0, axis_dim, 2, dtype=F32) / axis_dim)
    ang = pos.astype(F32)[..., None] * inv_freq
    cos = jnp.cos(ang)
    sin = jnp.sin(ang)
    cos64 = jnp.concatenate([cos[:, 0], cos[:, 0], cos[:, 1], cos[:, 1]], axis=-1)
    sin64 = jnp.concatenate([-sin[:, 0], sin[:, 0], -sin[:, 1], sin[:, 1]], axis=-1)
    cos64 = jnp.concatenate([jnp.ones((n_ctx, 64), F32), cos64], axis=0)
    sin64 = jnp.concatenate([jnp.zeros((n_ctx, 64), F32), sin64], axis=0)
    return cos64, sin64


def kernel(x, c, ctx, c_ctx, ada_w, ada_b, norm_g, mix_w_out, ffn_w_gate, ffn_w_up, ffn_conv_w, ffn_conv_b,
           ffn_w_down, ev_w_in, pool_w, pool_scale, diff_lambda, diff_subln, od_w_in, mla_q_norm, mla_w_uq,
           mla_kv_norm, mla_w_ukv, hgrn_norm, hgrn_lb):
    b, n_lat, d = x.shape
    n_ctx = ctx.shape[1]
    depth = ada_w.shape[0]
    assert d == D_MODEL and n_ctx % TILE == 0 and n_lat % TILE == 0 and n_lat % GRID_W == 0
    ctx_tiles = n_ctx // TILE
    group = _group_size(b, MAX_GROUP)
    group_odd_in = _group_size(b, ODD_IN_GROUP)

    rows = -(-(b + 1) // SUBLANES) * SUBLANES
    cond = jnp.concatenate([c, c_ctx[None, :], jnp.zeros((rows - b - 1, d), F32)], axis=0)
    mod = _modulation(cond, ada_w, ada_b).reshape(depth, rows, 6, d)

    cos64, sin64 = _rope_tables(n_ctx, n_lat)
    cos_diff = jnp.tile(cos64, (1, 2 * DIFF_HEADS))
    sin_diff = jnp.tile(sin64, (1, 2 * DIFF_HEADS))
    pad = ((0, 0), (0, LANES - MLA_ROPE_DIM))
    cos_mla = jnp.pad(cos64, pad, constant_values=1.0)
    sin_mla = jnp.pad(sin64, pad)

    xs = jnp.concatenate([ctx, x], axis=1)
    for layer in range(depth):
        last = layer == depth - 1
        j = layer // 2
        t_off = ctx_tiles if last else 0
        mod_g = _group_mod(mod[layer], b, group)
        ng_l = norm_g[layer]
        w_out = mix_w_out[layer].astype(BF16)
        if layer % 2 == 0:
            lam_init = 0.8 - 0.6 * math.exp(-0.3 * layer)
            u, qt, k, vt = _even_in(xs, mod_g, ng_l, cos_diff, sin_diff, ev_w_in[j].astype(BF16), ctx_tiles,
                                    group)
            att = _diff_attn(qt, k, vt, diff_lambda[j], diff_subln[j], lam_init, n_ctx)
            x_mid, h2 = _even_out(xs, u, att, mod_g, ng_l, pool_w[j].astype(BF16), pool_scale[j], w_out,
                                  ctx_tiles, t_off, group)
        else:
            (qt, kc, vt, hq, hk_f, hk_b, lf_f, lf_b, hv, hgate) = _odd_in(
                xs, _group_mod(mod[layer], b, group_odd_in), ng_l, cos_mla, sin_mla, od_w_in[j].astype(BF16),
                mla_q_norm[j], mla_w_uq[j].astype(BF16), mla_kv_norm[j], mla_w_ukv[j].astype(BF16), hgrn_lb,
                layer, ctx_tiles, group_odd_in)
            att = _mla_attn(qt, kc, vt, n_ctx, t_off * (TILE // MLA_TQ))
            o_f = _hgrn_scan(False, lf_f, hq, hk_f, hv, n_ctx // HG_BLOCK)
            hg = _hgrn_scan(True, lf_b, hq, hk_b, hv, n_ctx // HG_BLOCK, (o_f, hgate, hgrn_norm[j]))
            x_mid, h2 = _odd_out(xs, att, hg, mod_g, ng_l, w_out, ctx_tiles, t_off, group)
        xs = _ffn(x_mid, h2, mod_g, ng_l, ffn_w_gate[layer].astype(BF16), ffn_w_up[layer].astype(BF16),
                  ffn_conv_w[layer], ffn_conv_b[layer], ffn_w_down[layer].astype(BF16),
                  0 if last else ctx_tiles, group)
    return xs
```

```python
import functools
import math

import jax
import jax.numpy as jnp
import numpy as np
from jax import lax
from jax.experimental import pallas as pl
from jax.experimental.pallas import tpu as pltpu

F32 = jnp.float32
BF16 = jnp.bfloat16

D_MODEL = 1024
GRID_W = 64
RMS_EPS = 1e-6
ROPE_THETA = 10000.0
LOG2E = math.log2(math.e)

POOL_WINDOWS = (2, 4, 8, 16)
POOL_GROUPS = 4
POOL_WIDTH = 512
POOL_GROUP_DIM = POOL_WIDTH // POOL_GROUPS

DIFF_HEADS = 4
DIFF_QK_DIM = 64
DIFF_V_DIM = 128
DIFF_QK_WIDTH = 2 * DIFF_HEADS * DIFF_QK_DIM
DIFF_WIDTH = DIFF_HEADS * DIFF_V_DIM

MLA_HEADS = 4
MLA_Q_RANK = 512
MLA_KV_RANK = 256
MLA_NOPE_DIM = 128
MLA_ROPE_DIM = 64
MLA_V_DIM = 128
MLA_QK_PAD = 256
MLA_WIDTH = MLA_HEADS * MLA_V_DIM

HG_HEADS = 4
HG_K_DIM = 128
HG_V_DIM = 128
HG_QK_WIDTH = HG_HEADS * HG_K_DIM
HG_WIDTH = HG_HEADS * HG_V_DIM
HG_CHUNK = 64
HG_LEVELS = (64, 32, 16, 8, 4, 2)

D_FF = 2816
FF_CHUNK = 256

LANES = 128
SUBLANES = 8
BF16_SUBLANES = 16
MXU_TILE = 256

TILE = 256
MAX_GROUP = 4
ODD_IN_GROUP = 2
DIFF_TQ = 128
MLA_TQ = 256
HG_BLOCK = 256
HALO = BF16_SUBLANES
VMEM_LIMIT = 56 * 1024 * 1024
SOFTMAX_DENOM_FLOOR = 2.0 ** -100
SCORE_BOUND_SLACK = 1.0 + 2.0 ** -8


def _cparams(n_axes):
    return pltpu.CompilerParams(
        dimension_semantics=("arbitrary",) * n_axes, vmem_limit_bytes=VMEM_LIMIT)


def _dot(a, b):
    return jnp.dot(a, b, preferred_element_type=F32)


def _dot_nt(a, b):
    return lax.dot_general(a, b, (((1,), (1,)), ((), ())), preferred_element_type=F32)


def _rms(x, g):
    return x * lax.rsqrt(jnp.mean(x * x, axis=-1, keepdims=True) + RMS_EPS) * g


def _silu(x):
    return x * jax.nn.sigmoid(x)


def _const_spec(shape):
    zeros = (0,) * len(shape)
    return pl.BlockSpec(shape, lambda *_: zeros, pipeline_mode=pl.Buffered(1))


def _group_size(b, max_group):
    return max(g for g in range(1, max_group + 1) if b % g == 0)


def _group_mod(mod_l, b, group):
    return jnp.concatenate([mod_l[:b], jnp.broadcast_to(mod_l[b:b + 1], (group,) + mod_l.shape[1:])], axis=0)


def _rope(x, cos, sin_signed):
    n = x.shape[-1]
    lane = lax.broadcasted_iota(jnp.int32, x.shape, 1)
    first_half = (lane & 31) < 16
    partner = jnp.where(first_half, pltpu.roll(x, n - 16, 1), pltpu.roll(x, 16, 1))
    return x * cos + partner * sin_signed


def _prenorm_rows(x_ref, mod_ref, ng_ref, shift_row, scale_row, norm_row):
    parts = []
    for g in range(x_ref.shape[0]):
        h = (_rms(x_ref[g], ng_ref[norm_row:norm_row + 1, :]) * (1.0 + mod_ref[g, scale_row:scale_row + 1, :])
             + mod_ref[g, shift_row:shift_row + 1, :])
        parts.append(h.astype(BF16))
    return parts[0] if len(parts) == 1 else jnp.concatenate(parts, axis=0)


def _mod_kernel(c_ref, w_ref, b_ref, o_ref):
    s = _silu(c_ref[...])
    w = w_ref[0]
    s_hi = s.astype(BF16)
    s_lo = (s - s_hi.astype(F32)).astype(BF16)
    w_hi = w.astype(BF16)
    w_lo = (w - w_hi.astype(F32)).astype(BF16)
    o_ref[0] = _dot(s_hi, w_hi) + _dot(s_hi, w_lo) + _dot(s_lo, w_hi) + b_ref[0]


def _modulation(cond, ada_w, ada_b):
    depth, d, n = ada_w.shape
    rows = cond.shape[0]
    tn = 1536
    return pl.pallas_call(
        _mod_kernel,
        grid=(depth, n // tn),
        in_specs=[
            pl.BlockSpec((rows, d), lambda l, j: (0, 0)),
            pl.BlockSpec((1, d, tn), lambda l, j: (l, 0, j)),
            pl.BlockSpec((1, 1, tn), lambda l, j: (l, 0, j)),
        ],
        out_specs=pl.BlockSpec((1, rows, tn), lambda l, j: (l, 0, j)),
        out_shape=jax.ShapeDtypeStruct((depth, rows, n), F32),
        compiler_params=_cparams(2),
    )(cond, ada_w, ada_b.reshape(depth, 1, n))


class _SplitRows:
    def __init__(self, ctx_ref, lat_ref, is_ctx):
        self.ctx_ref, self.lat_ref, self.is_ctx = ctx_ref, lat_ref, is_ctx
        self.shape = lat_ref.shape

    def __getitem__(self, g):
        return jnp.where(self.is_ctx, self.ctx_ref[g], self.lat_ref[g])


def _token_rows(split_ctx_tiles, t_off, refs):
    if split_ctx_tiles is None:
        return refs[0], refs[1:]
    return _SplitRows(refs[0], refs[1], pl.program_id(1) + t_off < split_ctx_tiles), refs[2:]


def _token_specs(x, x_ctx, group, ctx_tiles, t_off):
    d = x.shape[2]
    if x_ctx is None:
        return [x], [pl.BlockSpec((group, TILE, d), lambda i, j: (i, j + t_off, 0))]
    return [x_ctx, x], [
        pl.BlockSpec((group, TILE, d), lambda i, j: (i, jnp.minimum(j + t_off, ctx_tiles - 1), 0)),
        pl.BlockSpec((group, TILE, d), lambda i, j: (i, jnp.maximum(j + t_off - ctx_tiles, 0), 0)),
    ]


def _even_in_kernel(split_ctx_tiles, *refs):
    x_ref, (mod_ref, ng_ref, cos_ref, sin_ref, w_ref, u_ref, qt_ref, k_ref, vt_ref) = _token_rows(
        split_ctx_tiles, 0, refs)
    p = _dot(_prenorm_rows(x_ref, mod_ref, ng_ref, 0, 1, 0), w_ref[...])
    cos = cos_ref[...]
    sin = sin_ref[...]
    o_q = POOL_WIDTH
    o_k = o_q + DIFF_QK_WIDTH
    o_v = o_k + DIFF_QK_WIDTH
    for g in range(x_ref.shape[0]):
        rows = slice(g * TILE, (g + 1) * TILE)
        u_ref[g] = p[rows, :o_q].astype(BF16)
        qt_ref[g] = (_rope(p[rows, o_q:o_k], cos, sin) * (DIFF_QK_DIM ** -0.5 * LOG2E)).T.astype(BF16)
        k_ref[g] = _rope(p[rows, o_k:o_v], cos, sin).astype(BF16)
        vt_ref[g] = p[rows, o_v:].T.astype(BF16)


def _even_in(x, x_ctx, mod_g, ng_l, cos, sin, w_in, ctx_tiles, group):
    b, t, d = x.shape
    if x_ctx is not None:
        t += x_ctx.shape[1]
    nt = t // TILE
    n_in = w_in.shape[1]

    def mod_map(i, j):
        return (jnp.where(j < ctx_tiles, b // group, i), 0, 0)

    tok = lambda w: pl.BlockSpec((group, TILE, w), lambda i, j: (i, j, 0))
    tok_t = lambda w: pl.BlockSpec((group, w, TILE), lambda i, j: (i, 0, j))
    tokens, token_specs = _token_specs(x, x_ctx, group, ctx_tiles, 0)
    return pl.pallas_call(
        functools.partial(_even_in_kernel, None if x_ctx is None else ctx_tiles),
        grid=(b // group, nt),
        in_specs=token_specs + [
            pl.BlockSpec((group, 6, d), mod_map),
            _const_spec((4, d)),
            pl.BlockSpec((TILE, DIFF_QK_WIDTH), lambda i, j: (j, 0)),
            pl.BlockSpec((TILE, DIFF_QK_WIDTH), lambda i, j: (j, 0)),
            _const_spec((d, n_in)),
        ],
        out_specs=[tok(POOL_WIDTH), tok_t(DIFF_QK_WIDTH), tok(DIFF_QK_WIDTH), tok_t(DIFF_WIDTH)],
        out_shape=[
            jax.ShapeDtypeStruct((b, t, POOL_WIDTH), BF16),
            jax.ShapeDtypeStruct((b, DIFF_QK_WIDTH, t), BF16),
            jax.ShapeDtypeStruct((b, t, DIFF_QK_WIDTH), BF16),
            jax.ShapeDtypeStruct((b, DIFF_WIDTH, t), BF16),
        ],
        compiler_params=_cparams(2),
    )(*tokens, mod_g, ng_l, cos, sin, w_in)


def _key_halves(nk):
    if nk <= MXU_TILE:
        return ((0, nk),)
    first = -(-(nk // 2) // MXU_TILE) * MXU_TILE
    return ((0, first), (first, nk))


def _attend_t(n_heads, nk, scores_operands, values_t, key_sq, s_ref, e_ref, finish):
    halves = _key_halves(nk)

    def weighted_values(h):
        vt = values_t(h)
        return sum(_dot(vt[:, a:b], e_ref[h % 2, a:b, :]) for a, b in halves)

    l_low = None
    for h in range(n_heads):
        keys, q_t = scores_operands(h)
        qf = q_t.astype(F32)
        bound = jnp.sqrt(jnp.sum(qf * qf, axis=0, keepdims=True) * key_sq(h)) * SCORE_BOUND_SLACK
        l = None
        for a, b in halves:
            e = jnp.exp2(_dot(keys[a:b, :], q_t) - bound)
            e_ref[h % 2, a:b, :] = e.astype(BF16)
            part = jnp.sum(e, axis=0, keepdims=True)
            l = part if l is None else l + part
        finish(h, weighted_values(h), l)
        l_low = l if l_low is None else jnp.minimum(l_low, l)
    accurate = jnp.min(l_low) > SOFTMAX_DENOM_FLOOR

    @pl.when(jnp.logical_not(accurate))
    def _():
        def scores(h):
            keys, q_t = scores_operands(h)
            tops = []
            for a, b in halves:
                s = _dot(keys[a:b, :], q_t)
                s_ref[h % 2, a:b, :] = s
                tops.append(jnp.max(s, axis=0, keepdims=True))
            return functools.reduce(jnp.maximum, tops)

        top = scores(0)
        for h in range(n_heads):
            nxt = scores(h + 1) if h + 1 < n_heads else None
            e = jnp.exp2(s_ref[h % 2, :nk, :] - top)
            e_ref[h % 2, :nk, :] = e.astype(BF16)
            finish(h, weighted_values(h), jnp.sum(e, axis=0, keepdims=True))
            top = nxt


def _store_key_sq(ksq_ref, row, cols, sq_norms, n_ctx):
    n = cols.stop - cols.start
    ksq_ref[row:row + 1, cols] = jnp.broadcast_to(jnp.max(sq_norms[:n_ctx], axis=0, keepdims=True), (1, n))
    r2 = row + ksq_ref.shape[0] // 2
    ksq_ref[r2:r2 + 1, cols] = jnp.broadcast_to(jnp.max(sq_norms, axis=0, keepdims=True), (1, n))


def _diff_attn_kernel(lam_init, n_ctx, qt_ref, k_ref, vt_ref, lam_ref, g_ref, o_ref, s_ref, e_ref, ksq_ref):
    lv = lam_ref[...]
    lam = (jnp.exp(jnp.sum(lv[0:1] * lv[1:2], axis=-1, keepdims=True))
           - jnp.exp(jnp.sum(lv[2:3] * lv[3:4], axis=-1, keepdims=True)) + lam_init)
    gain = g_ref[...] * (1.0 - lam_init)
    row = lax.broadcasted_iota(jnp.int32, (LANES, DIFF_TQ), 0)

    @pl.when(pl.program_id(1) == 0)
    def _():
        first_head = (lax.broadcasted_iota(jnp.int32, (1, LANES), 1) < DIFF_QK_DIM).astype(F32)
        for h in range(DIFF_HEADS):
            kf = k_ref[0, :, h * LANES:(h + 1) * LANES].astype(F32)
            sq = kf * kf
            sq0 = jnp.sum(sq * first_head, axis=1, keepdims=True)
            sq1 = jnp.sum(sq * (1.0 - first_head), axis=1, keepdims=True)
            _store_key_sq(ksq_ref, h, slice(0, DIFF_TQ), sq0, n_ctx)
            _store_key_sq(ksq_ref, h, slice(DIFF_TQ, 2 * DIFF_TQ), sq1, n_ctx)

    def attend(nk, ctx_only):
        ksq_row = 0 if ctx_only else DIFF_HEADS

        def scores_operands(h):
            cols = slice(h * LANES, (h + 1) * LANES)
            qp = qt_ref[0, cols, :]
            q_bd = jnp.concatenate(
                [jnp.where(row < DIFF_QK_DIM, qp, 0), jnp.where(row >= DIFF_QK_DIM, qp, 0)], axis=1)
            return k_ref.at[0, :nk, cols], q_bd

        def finish(h, o_t, l):
            r = 1.0 / l
            o = o_t[:, :DIFF_TQ] * r[:, :DIFF_TQ] - o_t[:, DIFF_TQ:] * (lam * r[:, DIFF_TQ:])
            o = o * lax.rsqrt(jnp.mean(o * o, axis=0, keepdims=True) + RMS_EPS) * gain
            o_ref[0, :, h * LANES:(h + 1) * LANES] = o.T.astype(BF16)

        _attend_t(DIFF_HEADS, nk, scores_operands,
                  lambda h: vt_ref.at[0, h * LANES:(h + 1) * LANES, :nk],
                  lambda h: ksq_ref[ksq_row + h:ksq_row + h + 1, :], s_ref, e_ref, finish)

    is_ctx = pl.program_id(1) < n_ctx // DIFF_TQ

    @pl.when(is_ctx)
    def _():
        attend(n_ctx, True)

    @pl.when(jnp.logical_not(is_ctx))
    def _():
        attend(k_ref.shape[1], False)


def _diff_attn(qt, k, vt, lam_vec, subln, lam_init, n_ctx):
    b, t, _ = k.shape
    per_sample = lambda shape: pl.BlockSpec(shape, lambda i, j: (i, 0, 0), pipeline_mode=pl.Buffered(1))
    return pl.pallas_call(
        functools.partial(_diff_attn_kernel, lam_init, n_ctx),
        grid=(b, t // DIFF_TQ),
        in_specs=[
            pl.BlockSpec((1, DIFF_QK_WIDTH, DIFF_TQ), lambda i, j: (i, 0, j)),
            per_sample((1, t, DIFF_QK_WIDTH)),
            per_sample((1, DIFF_WIDTH, t)),
            _const_spec((4, DIFF_QK_DIM)),
            _const_spec((DIFF_V_DIM, 1)),
        ],
        out_specs=pl.BlockSpec((1, DIFF_TQ, DIFF_WIDTH), lambda i, j: (i, j, 0)),
        out_shape=jax.ShapeDtypeStruct((b, t, DIFF_WIDTH), BF16),
        scratch_shapes=[pltpu.VMEM((2, t, 2 * DIFF_TQ), F32), pltpu.VMEM((2, t, 2 * DIFF_TQ), BF16),
                        pltpu.VMEM((2 * DIFF_HEADS, 2 * DIFF_TQ), F32)],
        compiler_params=_cparams(2),
    )(qt, k, vt, lam_vec, subln.reshape(DIFF_V_DIM, 1))


def _residual_and_prenorm(x_ref, y, mod_ref, ng_ref, xo_ref, h2_ref):
    for g in range(x_ref.shape[0]):
        xn = x_ref[g] + mod_ref[g, 2:3, :] * _rms(y[g * TILE:(g + 1) * TILE], ng_ref[1:2, :])
        xo_ref[g] = xn
        h2 = _rms(xn, ng_ref[2:3, :]) * (1.0 + mod_ref[g, 4:5, :]) + mod_ref[g, 3:4, :]
        h2_ref[g] = h2.astype(BF16)


def _even_out_kernel(split_ctx_tiles, t_off, seg_tiles, n_tiles, *refs):
    x_ref, (u_ref, up_ref, un_ref, a_ref, mod_ref, ng_ref, pw_ref, ps_ref, wo_ref, xo_ref, h2_ref,
            ext_ref) = _token_rows(split_ctx_tiles, t_off, refs)
    j = pl.program_id(1) + t_off
    prev_ok = jnp.logical_and(j != 0, j != seg_tiles)
    next_ok = jnp.logical_and(j != n_tiles - 1, j != seg_tiles - 1)
    lo = jnp.where(prev_ok, -HALO, 0)
    hi = jnp.where(next_ok, TILE + HALO, TILE)
    row = lax.broadcasted_iota(jnp.int32, (TILE, 1), 0)
    group = x_ref.shape[0]
    pooled = []
    for g in range(group):
        ext_ref[g, 0:HALO, :] = jnp.where(prev_ok, up_ref[g].astype(F32), 0.0)
        ext_ref[g, HALO:HALO + TILE, :] = u_ref[g].astype(F32)
        ext_ref[g, HALO + TILE:, :] = jnp.where(next_ok, un_ref[g].astype(F32), 0.0)
        parts = []
        for gidx, win in enumerate(POOL_WINDOWS):
            half = win // 2
            cols = slice(gidx * POOL_GROUP_DIM, (gidx + 1) * POOL_GROUP_DIM)
            acc = ext_ref[g, HALO - half:HALO - half + TILE, cols]
            for off in range(-half + 1, half):
                acc = acc + ext_ref[g, HALO + off:HALO + off + TILE, cols]
            cnt = jnp.minimum(row + half, hi) - jnp.maximum(row - half, lo)
            parts.append((acc / cnt.astype(F32) - ext_ref[g, HALO:HALO + TILE, cols]).astype(BF16))
        pooled.append(parts)
    yp = jnp.concatenate(
        [_dot(jnp.concatenate([pooled[g][gidx] for g in range(group)], axis=0), pw_ref[gidx])
         for gidx in range(POOL_GROUPS)], axis=1) * ps_ref[...]
    att = jnp.concatenate([a_ref[g] for g in range(group)], axis=0)
    y = _dot(yp.astype(BF16), wo_ref[:POOL_WIDTH, :]) + _dot(att, wo_ref[POOL_WIDTH:, :])
    _residual_and_prenorm(x_ref, y, mod_ref, ng_ref, xo_ref, h2_ref)


def _even_out(x, x_ctx, u, att, mod_g, ng_l, pool_w, pool_scale, w_out, ctx_tiles, t_off, group):
    b, t, d = x.shape
    if x_ctx is not None:
        t += x_ctx.shape[1]
    nt = t // TILE
    nh = TILE // HALO
    n_out = nt - t_off

    def mod_map(i, j):
        return (jnp.where(j + t_off < ctx_tiles, b // group, i), 0, 0)

    tok = lambda w: pl.BlockSpec((group, TILE, w), lambda i, j: (i, j + t_off, 0))
    out_tok = lambda: pl.BlockSpec((group, TILE, d), lambda i, j: (i, j, 0))
    tokens, token_specs = _token_specs(x, x_ctx, group, ctx_tiles, t_off)
    return pl.pallas_call(
        functools.partial(_even_out_kernel, None if x_ctx is None else ctx_tiles, t_off, ctx_tiles, nt),
        grid=(b // group, n_out),
        in_specs=token_specs + [
            tok(POOL_WIDTH),
            pl.BlockSpec((group, HALO, POOL_WIDTH), lambda i, j: (i, jnp.maximum((j + t_off) * nh - 1, 0), 0)),
            pl.BlockSpec((group, HALO, POOL_WIDTH),
                         lambda i, j: (i, jnp.minimum((j + t_off + 1) * nh, nt * nh - 1), 0)),
            tok(DIFF_WIDTH),
            pl.BlockSpec((group, 6, d), mod_map),
            _const_spec((4, d)),
            _const_spec((POOL_GROUPS, POOL_GROUP_DIM, POOL_GROUP_DIM)),
            _const_spec((1, POOL_WIDTH)),
            _const_spec((POOL_WIDTH + DIFF_WIDTH, d)),
        ],
        out_specs=[out_tok(), out_tok()],
        out_shape=[
            jax.ShapeDtypeStruct((b, n_out * TILE, d), F32),
            jax.ShapeDtypeStruct((b, n_out * TILE, d), BF16),
        ],
        scratch_shapes=[pltpu.VMEM((group, TILE + 2 * HALO, POOL_WIDTH), F32)],
        compiler_params=_cparams(2),
    )(*tokens, u, u, u, att, mod_g, ng_l, pool_w, pool_scale.reshape(1, POOL_WIDTH), w_out)


def _ffn_kernel(seg_tiles, n_tiles, x_ref, h_ref, hp_ref, hn_ref, mod_ref, ng_ref, wg_ref, wu_ref,
                cw_ref, cb_ref, wd_ref, o_ref, a_ref, act_ref):
    j = pl.program_id(1)
    prev_ok = jnp.logical_and(j != 0, j != seg_tiles)
    next_ok = jnp.logical_and(j != n_tiles - 1, j != seg_tiles - 1)
    group = x_ref.shape[0]
    ext = TILE + 2 * HALO
    zero = jnp.zeros((HALO, h_ref.shape[2]), BF16)
    pieces = []
    for g in range(group):
        pieces += [jnp.where(prev_ok, hp_ref[g], zero), h_ref[g], jnp.where(next_ok, hn_ref[g], zero)]
    h_ext = jnp.concatenate(pieces, axis=0)
    h = jnp.concatenate([h_ref[g] for g in range(group)], axis=0) if group > 1 else h_ref[0]
    for c in range(D_FF // FF_CHUNK):
        cols = slice(c * FF_CHUNK, (c + 1) * FF_CHUNK)
        a_ref[...] = _dot(h_ext, wg_ref[:, cols])
        up = _dot(h, wu_ref[:, cols])
        for g in range(group):
            base = g * ext + HALO
            a = (a_ref[base - 1:base - 1 + TILE, :] * cw_ref[0:1, cols]
                 + a_ref[base:base + TILE, :] * cw_ref[1:2, cols]
                 + a_ref[base + 1:base + 1 + TILE, :] * cw_ref[2:3, cols] + cb_ref[:, cols])
            act_ref[g * TILE:(g + 1) * TILE, cols] = (_silu(a) * up[g * TILE:(g + 1) * TILE]).astype(BF16)
    f = _dot(act_ref[...], wd_ref[...])
    for g in range(group):
        o_ref[g] = x_ref[g] + mod_ref[g, 5:6, :] * _rms(f[g * TILE:(g + 1) * TILE], ng_ref[3:4, :])


def _ffn(x, h2, mod_g, ng_l, w_gate, w_up, conv_w, conv_b, w_down, ctx_tiles, group):
    b, t, d = x.shape
    nt = t // TILE
    nh = TILE // HALO

    def mod_map(i, j):
        return (jnp.where(j < ctx_tiles, b // group, i), 0, 0)

    tok = lambda: pl.BlockSpec((group, TILE, d), lambda i, j: (i, j, 0))
    return pl.pallas_call(
        functools.partial(_ffn_kernel, ctx_tiles, nt),
        grid=(b // group, nt),
        in_specs=[
            tok(),
            tok(),
            pl.BlockSpec((group, HALO, d), lambda i, j: (i, jnp.maximum(j * nh - 1, 0), 0)),
            pl.BlockSpec((group, HALO, d), lambda i, j: (i, jnp.minimum((j + 1) * nh, nt * nh - 1), 0)),
            pl.BlockSpec((group, 6, d), mod_map),
            _const_spec((4, d)),
            _const_spec((d, D_FF)),
            _const_spec((d, D_FF)),
            _const_spec((3, D_FF)),
            _const_spec((1, D_FF)),
            _const_spec((D_FF, d)),
        ],
        out_specs=tok(),
        out_shape=jax.ShapeDtypeStruct((b, t, d), F32),
        scratch_shapes=[pltpu.VMEM((group * (TILE + 2 * HALO), FF_CHUNK), F32),
                        pltpu.VMEM((group * TILE, D_FF), BF16)],
        compiler_params=_cparams(2),
    )(x, h2, h2, h2, mod_g, ng_l, w_gate, w_up, conv_w, conv_b.reshape(1, D_FF), w_down)


def _odd_in_kernel(layer, x_ref, mod_ref, ng_ref, cos_ref, sin_ref, wq_ref, wkv_ref, wkr_ref, wh_ref,
                   qn_ref, wuq_ref, kvn_ref, wukv_ref, lb_ref,
                   qt_ref, kc_ref, vt_ref, hq_ref, hkf_ref, hkb_ref, lff_ref, lfb_ref, hv_ref, hg_ref):
    group = x_ref.shape[0]
    hb = _prenorm_rows(x_ref, mod_ref, ng_ref, 0, 1, 0)
    cos = cos_ref[...]
    sin = sin_ref[...]

    cq = _rms(_dot(hb, wq_ref[...]), qn_ref[...])
    q = _dot(cq.astype(BF16), wuq_ref[...]) * ((MLA_NOPE_DIM + MLA_ROPE_DIM) ** -0.5 * LOG2E)
    ckv = _rms(_dot(hb, wkv_ref[...]), kvn_ref[...])
    kv = _dot(ckv.astype(BF16), wukv_ref[...])
    kr_all = _dot(hb, wkr_ref[...])
    for g in range(group):
        rows = slice(g * TILE, (g + 1) * TILE)
        kr = _rope(kr_all[rows], cos, sin).astype(BF16)
        for hd in range(MLA_HEADS):
            qb = hd * MLA_QK_PAD
            q_h = jnp.concatenate(
                [q[rows, qb:qb + MLA_NOPE_DIM], _rope(q[rows, qb + MLA_NOPE_DIM:qb + MLA_QK_PAD], cos, sin)], axis=1)
            qt_ref[g, hd] = q_h.T.astype(BF16)
            kb = hd * (MLA_NOPE_DIM + MLA_V_DIM)
            kc_ref[g, hd, :, :MLA_NOPE_DIM] = kv[rows, kb:kb + MLA_NOPE_DIM].astype(BF16)
            kc_ref[g, hd, :, MLA_NOPE_DIM:] = kr
            vt_ref[g, hd * MLA_V_DIM:(hd + 1) * MLA_V_DIM, :] = kv[
                rows, kb + MLA_NOPE_DIM:kb + MLA_NOPE_DIM + MLA_V_DIM].T.astype(BF16)

    gates = _dot(hb, wh_ref[...])
    w = HG_QK_WIDTH
    lbs = []
    for direction in range(2):
        lrows = [lb_ref[direction, i:i + 1, :] for i in range(lb_ref.shape[1])]
        top = functools.reduce(jnp.maximum, lrows)
        ex = [jnp.exp(r - top) for r in lrows]
        lbs.append(sum(ex[1:layer + 1], jnp.zeros_like(top)) / sum(ex))
    for g in range(group):
        rows = slice(g * TILE, (g + 1) * TILE)
        hq_ref[g] = _silu(gates[rows, :w]).astype(BF16)
        hv_ref[g] = gates[rows, 3 * w:4 * w].astype(BF16)
        hg_ref[g] = _silu(gates[rows, 4 * w:]).astype(BF16)
        for direction, (k_ref, lf_ref) in enumerate(((hkf_ref, lff_ref), (hkb_ref, lfb_ref))):
            lb = lbs[direction]
            f = lb + (1.0 - lb) * jax.nn.sigmoid(gates[rows, (1 + direction) * w:(2 + direction) * w])
            k_ref[g] = (1.0 - f).astype(BF16)
            lf_ref[g] = jnp.log(f)


def _odd_in(x, mod_g, ng_l, cos, sin, w_in, q_norm, w_uq, kv_norm, w_ukv, hgrn_lb, layer, ctx_tiles, group):
    b, t, d = x.shape
    nt = t // TILE
    o1 = MLA_Q_RANK
    o2 = o1 + MLA_KV_RANK
    o3 = o2 + MLA_ROPE_DIM
    w_q = w_in[:, :o1]
    w_kv = w_in[:, o1:o2]
    w_kr = jnp.pad(w_in[:, o2:o3], ((0, 0), (0, LANES - MLA_ROPE_DIM)))
    w_h = w_in[:, o3:]
    qk = MLA_NOPE_DIM + MLA_ROPE_DIM
    w_uq_pad = jnp.pad(w_uq.reshape(MLA_Q_RANK, MLA_HEADS, qk),
                       ((0, 0), (0, 0), (0, MLA_QK_PAD - qk))).reshape(MLA_Q_RANK, MLA_HEADS * MLA_QK_PAD)

    def mod_map(i, j):
        return (jnp.where(j < ctx_tiles, b // group, i), 0, 0)

    tok = lambda w: pl.BlockSpec((group, TILE, w), lambda i, j: (i, j, 0))
    sds = lambda w, dt: jax.ShapeDtypeStruct((b, t, w), dt)
    hw = HG_QK_WIDTH
    return pl.pallas_call(
        functools.partial(_odd_in_kernel, layer),
        grid=(b // group, nt),
        in_specs=[
            tok(d),
            pl.BlockSpec((group, 6, d), mod_map),
            _const_spec((4, d)),
            pl.BlockSpec((TILE, LANES), lambda i, j: (j, 0)),
            pl.BlockSpec((TILE, LANES), lambda i, j: (j, 0)),
            _const_spec(w_q.shape),
            _const_spec(w_kv.shape),
            _const_spec(w_kr.shape),
            _const_spec(w_h.shape),
            _const_spec((1, MLA_Q_RANK)),
            _const_spec(w_uq_pad.shape),
            _const_spec((1, MLA_KV_RANK)),
            _const_spec(w_ukv.shape),
            _const_spec(hgrn_lb.shape),
        ],
        out_specs=[
            pl.BlockSpec((group, MLA_HEADS, MLA_QK_PAD, TILE), lambda i, j: (i, 0, 0, j)),
            pl.BlockSpec((group, MLA_HEADS, TILE, MLA_QK_PAD), lambda i, j: (i, 0, j, 0)),
            pl.BlockSpec((group, MLA_WIDTH, TILE), lambda i, j: (i, 0, j)),
            tok(hw), tok(hw), tok(hw), tok(hw), tok(hw), tok(hw), tok(hw),
        ],
        out_shape=[
            jax.ShapeDtypeStruct((b, MLA_HEADS, MLA_QK_PAD, t), BF16),
            jax.ShapeDtypeStruct((b, MLA_HEADS, t, MLA_QK_PAD), BF16),
            jax.ShapeDtypeStruct((b, MLA_WIDTH, t), BF16),
            sds(hw, BF16), sds(hw, BF16), sds(hw, BF16), sds(hw, F32), sds(hw, F32), sds(hw, BF16),
            sds(hw, BF16),
        ],
        compiler_params=_cparams(2),
    )(x, mod_g, ng_l, cos, sin, w_q, w_kv, w_kr, w_h, q_norm.reshape(1, -1), w_uq_pad,
      kv_norm.reshape(1, -1), w_ukv, hgrn_lb)


def _mla_attn_kernel(n_ctx, q_off, qt_ref, kc_ref, vt_ref, o_ref, s_ref, e_ref, ksq_ref):
    @pl.when(pl.program_id(1) == 0)
    def _():
        for h in range(MLA_HEADS):
            kf = kc_ref[0, h].astype(F32)
            _store_key_sq(ksq_ref, h, slice(0, MLA_TQ), jnp.sum(kf * kf, axis=1, keepdims=True), n_ctx)

    def attend(nk, ctx_only):
        ksq_row = 0 if ctx_only else MLA_HEADS

        def finish(h, o_t, l):
            o_ref[0, :, h * MLA_V_DIM:(h + 1) * MLA_V_DIM] = (o_t * (1.0 / l)).T.astype(BF16)

        _attend_t(MLA_HEADS, nk, lambda h: (kc_ref.at[0, h, :nk, :], qt_ref[0, h]),
                  lambda h: vt_ref.at[0, h * MLA_V_DIM:(h + 1) * MLA_V_DIM, :nk],
                  lambda h: ksq_ref[ksq_row + h:ksq_row + h + 1, :], s_ref, e_ref, finish)

    if q_off * MLA_TQ >= n_ctx:
        attend(kc_ref.shape[2], False)
    else:
        is_ctx = pl.program_id(1) + q_off < n_ctx // MLA_TQ

        @pl.when(is_ctx)
        def _():
            attend(n_ctx, True)

        @pl.when(jnp.logical_not(is_ctx))
        def _():
            attend(kc_ref.shape[2], False)


def _mla_attn(qt, kc, vt, n_ctx, q_off):
    b, _, t, _ = kc.shape
    nq = t // MLA_TQ - q_off
    return pl.pallas_call(
        functools.partial(_mla_attn_kernel, n_ctx, q_off),
        grid=(b, nq),
        in_specs=[
            pl.BlockSpec((1, MLA_HEADS, MLA_QK_PAD, MLA_TQ), lambda i, j: (i, 0, 0, j + q_off)),
            pl.BlockSpec((1, MLA_HEADS, t, MLA_QK_PAD), lambda i, j: (i, 0, 0, 0), pipeline_mode=pl.Buffered(1)),
            pl.BlockSpec((1, MLA_WIDTH, t), lambda i, j: (i, 0, 0), pipeline_mode=pl.Buffered(1)),
        ],
        out_specs=pl.BlockSpec((1, MLA_TQ, MLA_WIDTH), lambda i, j: (i, j, 0)),
        out_shape=jax.ShapeDtypeStruct((b, nq * MLA_TQ, MLA_WIDTH), BF16),
        scratch_shapes=[pltpu.VMEM((2, t, MLA_TQ), F32), pltpu.VMEM((2, t, MLA_TQ), BF16),
                        pltpu.VMEM((2 * MLA_HEADS, MLA_TQ), F32)],
        compiler_params=_cparams(2),
    )(qt, kc, vt)


def _hgrn_triangle(reverse):
    t = np.arange(HG_CHUNK)[:, None]
    j = np.arange(HG_CHUNK)[None, :]
    return ((j >= t) if reverse else (j <= t)).astype(np.float32)


def _hgrn_masks(reverse):
    c = HG_CHUNK
    t = lax.broadcasted_iota(jnp.int32, (c, c), 0)
    s = lax.broadcasted_iota(jnp.int32, (c, c), 1)
    tq = lax.broadcasted_iota(jnp.int32, (c, 1), 0)
    masks = []
    for m in HG_LEVELS:
        half = m // 2
        shift = int(math.log2(m))
        t_hi = (t & (m - 1)) >= half
        s_hi = (s & (m - 1)) >= half
        same = (t >> shift) == (s >> shift)
        if not reverse:
            pair = jnp.where(same, jnp.where(t_hi, jnp.where(s_hi, 0.0, 1.0), 0.0), 0.0)
            is_q = (tq & (m - 1)) >= half
        else:
            pair = jnp.where(same, jnp.where(t_hi, 0.0, jnp.where(s_hi, 1.0, 0.0)), 0.0)
            is_q = (tq & (m - 1)) < half
        masks.append((pair > 0.5, is_q))
    return masks, t == s


def _hgrn_reference_rows(cum, m, reverse):
    c = HG_CHUNK
    half = m // 2
    ref_in_block = half if reverse else half - 1
    if m >= 2 * SUBLANES:
        return jnp.concatenate(
            [jnp.broadcast_to(cum[b0 + ref_in_block:b0 + ref_in_block + 1, :], (m, LANES))
             for b0 in range(0, c, m)], axis=0)
    cum3 = cum.reshape(c // SUBLANES, SUBLANES, LANES)
    r = lax.broadcasted_iota(jnp.int32, cum3.shape, 1)
    out = None
    for b0 in reversed(range(0, SUBLANES, m)):
        pick = jnp.broadcast_to(cum3[:, b0 + ref_in_block:b0 + ref_in_block + 1, :], cum3.shape)
        out = pick if out is None else jnp.where(r < b0 + m, pick, out)
    return out.reshape(c, LANES)


def _hgrn_scan_body(reverse, tri_ref, lf_ref, q_ref, k_ref, v_ref, st_ref, emit):
    masks, eye = _hgrn_masks(reverse)
    tri = tri_ref[...]
    c = HG_CHUNK
    n_chunks = HG_BLOCK // c
    chunk_order = list(range(n_chunks - 1, -1, -1) if reverse else range(n_chunks))
    probs = [(ci, h) for ci in chunk_order for h in range(HG_HEADS)]
    where = lambda ci, h: (slice(ci * c, (ci + 1) * c), slice(h * HG_K_DIM, (h + 1) * HG_K_DIM))

    @pl.when(pl.program_id(1) == 0)
    def _():
        st_ref[...] = jnp.zeros_like(st_ref)

    cums = []
    for p in probs:
        rows, cols = where(*p)
        lf = lf_ref[0, rows, cols]
        hi = lf.astype(BF16)
        r1 = lf - hi.astype(F32)
        mid = r1.astype(BF16)
        lo = (r1 - mid.astype(F32)).astype(BF16)
        c3 = _dot(tri, jnp.concatenate([hi, mid, lo], axis=1))
        cums.append(c3[:, :LANES] + c3[:, LANES:2 * LANES] + c3[:, 2 * LANES:])

    qe, ks, decay, scores = [], [], [], []
    for p, cum in zip(probs, cums):
        rows, cols = where(*p)
        total = cum[0:1] if reverse else cum[c - 1:c]
        q = q_ref[0, rows, cols]
        k = k_ref[0, rows, cols]
        qe.append((q.astype(F32) * jnp.exp(cum)).astype(BF16))
        ks.append((k.astype(F32) * jnp.exp(total - cum)).astype(BF16))
        decay.append(jnp.exp(total))
        scores.append(jnp.where(eye, _dot_nt(q, k), 0.0))

    for m, (pair, is_q) in zip(HG_LEVELS, masks):
        for i, (p, cum) in enumerate(zip(probs, cums)):
            rows, cols = where(*p)
            d = cum - _hgrn_reference_rows(cum, m, reverse)
            qk = jnp.where(is_q, q_ref[0, rows, cols].astype(F32), k_ref[0, rows, cols].astype(F32))
            x = (qk * jnp.exp(jnp.minimum(jnp.where(is_q, d, -d), 0.0))).astype(BF16)
            scores[i] = scores[i] + jnp.where(pair, _dot_nt(x, x), 0.0)

    intra, update = [], []
    for i, p in enumerate(probs):
        rows, cols = where(*p)
        v = v_ref[0, rows, cols]
        intra.append(_dot(scores[i].astype(BF16), v))
        update.append(_dot(v.astype(F32).T.astype(BF16), ks[i]))

    for h in range(HG_HEADS):
        st = st_ref[h]
        for n in range(n_chunks):
            i = n * HG_HEADS + h
            rows, cols = where(*probs[i])
            emit(rows, cols, intra[i] + _dot_nt(qe[i], st.astype(BF16)))
            st = st * decay[i] + update[i]
        st_ref[h] = st


def _hgrn_fwd_kernel(tri_ref, lf_ref, q_ref, k_ref, v_ref, o_ref, st_ref):
    def emit(rows, cols, o):
        o_ref[0, rows, cols] = o

    _hgrn_scan_body(False, tri_ref, lf_ref, q_ref, k_ref, v_ref, st_ref, emit)


def _hgrn_bwd_kernel(tri_ref, lf_ref, q_ref, k_ref, v_ref, of_ref, gate_ref, gn_ref, o_ref, st_ref):
    def emit(rows, cols, o):
        o_ref[0, rows, cols] = (_rms(o + of_ref[0, rows, cols], gn_ref[...])
                                * gate_ref[0, rows, cols].astype(F32)).astype(BF16)

    _hgrn_scan_body(True, tri_ref, lf_ref, q_ref, k_ref, v_ref, st_ref, emit)


def _hgrn_scan(reverse, lf, q, k, v, ctx_blocks, extra=()):
    b, t, w = lf.shape
    nb = t // HG_BLOCK
    tri = jnp.asarray(_hgrn_triangle(reverse), BF16)
    if reverse:
        blk = lambda j: jnp.where(j < ctx_blocks, ctx_blocks - 1 - j, nb - 1 - (j - ctx_blocks))
    else:
        blk = lambda j: j
    tok = pl.BlockSpec((1, HG_BLOCK, w), lambda i, j: (i, blk(j), 0))
    in_specs = [_const_spec(tri.shape), tok, tok, tok, tok]
    args = [tri, lf, q, k, v]
    if reverse:
        o_f, gate, gnorm = extra
        in_specs += [tok, tok, _const_spec((1, HG_V_DIM))]
        args += [o_f, gate, gnorm.reshape(1, HG_V_DIM)]
    return pl.pallas_call(
        _hgrn_bwd_kernel if reverse else _hgrn_fwd_kernel,
        grid=(b, nb),
        in_specs=in_specs,
        out_specs=tok,
        out_shape=jax.ShapeDtypeStruct((b, t, w), BF16 if reverse else F32),
        scratch_shapes=[pltpu.VMEM((HG_HEADS, HG_V_DIM, HG_K_DIM), F32)],
        compiler_params=_cparams(2),
    )(*args)


def _odd_out_kernel(x_ref, a_ref, g_ref, mod_ref, ng_ref, wo_ref, xo_ref, h2_ref):
    group = x_ref.shape[0]
    att = jnp.concatenate([a_ref[g] for g in range(group)], axis=0)
    hg = jnp.concatenate([g_ref[g] for g in range(group)], axis=0)
    y = _dot(att, wo_ref[:MLA_WIDTH, :]) + _dot(hg, wo_ref[MLA_WIDTH:, :])
    _residual_and_prenorm(x_ref, y, mod_ref, ng_ref, xo_ref, h2_ref)


def _odd_out(x, att, hg, mod_g, ng_l, w_out, ctx_tiles, t_off, group):
    b, t, d = x.shape
    n_out = t // TILE - t_off

    def mod_map(i, j):
        return (jnp.where(j + t_off < ctx_tiles, b // group, i), 0, 0)

    out_tok = lambda w: pl.BlockSpec((group, TILE, w), lambda i, j: (i, j, 0))
    return pl.pallas_call(
        _odd_out_kernel,
        grid=(b // group, n_out),
        in_specs=[
            pl.BlockSpec((group, TILE, d), lambda i, j: (i, j + t_off, 0)),
            out_tok(MLA_WIDTH),
            pl.BlockSpec((group, TILE, HG_WIDTH), lambda i, j: (i, j + t_off, 0)),
            pl.BlockSpec((group, 6, d), mod_map),
            _const_spec((4, d)),
            _const_spec((MLA_WIDTH + HG_WIDTH, d)),
        ],
        out_specs=[out_tok(d), out_tok(d)],
        out_shape=[
            jax.ShapeDtypeStruct((b, n_out * TILE, d), F32),
            jax.ShapeDtypeStruct((b, n_out * TILE, d), BF16),
        ],
        compiler_params=_cparams(2),
    )(x, att, hg, mod_g, ng_l, w_out)


def _rope_tables(n_ctx, n_lat):
    rows = n_lat // GRID_W
    pos = jnp.stack([jnp.repeat(jnp.arange(rows), GRID_W), jnp.tile(jnp.arange(GRID_W), rows)], axis=-1)
    axis_dim = DIFF_QK_DIM // 2
    inv_freq = ROPE_THETA ** (-jnp.arange(0, axis_dim, 2, dtype=F32) / axis_dim)
    ang = pos.astype(F32)[..., None] * inv_freq
    cos = jnp.cos(ang)
    sin = jnp.sin(ang)
    cos64 = jnp.concatenate([cos[:, 0], cos[:, 0], cos[:, 1], cos[:, 1]], axis=-1)
    sin64 = jnp.concatenate([-sin[:, 0], sin[:, 0], -sin[:, 1], sin[:, 1]], axis=-1)
    cos64 = jnp.concatenate([jnp.ones((n_ctx, 64), F32), cos64], axis=0)
    sin64 = jnp.concatenate([jnp.zeros((n_ctx, 64), F32), sin64], axis=0)
    return cos64, sin64


def kernel(x, c, ctx, c_ctx, ada_w, ada_b, norm_g, mix_w_out, ffn_w_gate, ffn_w_up, ffn_conv_w, ffn_conv_b,
           ffn_w_down, ev_w_in, pool_w, pool_scale, diff_lambda, diff_subln, od_w_in, mla_q_norm, mla_w_uq,
           mla_kv_norm, mla_w_ukv, hgrn_norm, hgrn_lb):
    b, n_lat, d = x.shape
    n_ctx = ctx.shape[1]
    depth = ada_w.shape[0]
    assert d == D_MODEL and n_ctx % TILE == 0 and n_lat % TILE == 0 and n_lat % GRID_W == 0
    ctx_tiles = n_ctx // TILE
    group = _group_size(b, MAX_GROUP)
    group_odd_in = _group_size(b, ODD_IN_GROUP)

    rows = -(-(b + 1) // SUBLANES) * SUBLANES
    cond = jnp.concatenate([c, c_ctx[None, :], jnp.zeros((rows - b - 1, d), F32)], axis=0)
    mod = _modulation(cond, ada_w, ada_b).reshape(depth, rows, 6, d)

    cos64, sin64 = _rope_tables(n_ctx, n_lat)
    cos_diff = jnp.tile(cos64, (1, 2 * DIFF_HEADS))
    sin_diff = jnp.tile(sin64, (1, 2 * DIFF_HEADS))
    pad = ((0, 0), (0, LANES - MLA_ROPE_DIM))
    cos_mla = jnp.pad(cos64, pad, constant_values=1.0)
    sin_mla = jnp.pad(sin64, pad)

    xs, xs_ctx = x, ctx
    for layer in range(depth):
        last = layer == depth - 1
        j = layer // 2
        t_off = ctx_tiles if last else 0
        mod_g = _group_mod(mod[layer], b, group)
        ng_l = norm_g[layer]
        w_out = mix_w_out[layer].astype(BF16)
        if layer % 2 == 0:
            lam_init = 0.8 - 0.6 * math.exp(-0.3 * layer)
            u, qt, k, vt = _even_in(xs, xs_ctx, mod_g, ng_l, cos_diff, sin_diff, ev_w_in[j].astype(BF16),
                                    ctx_tiles, group)
            att = _diff_attn(qt, k, vt, diff_lambda[j], diff_subln[j], lam_init, n_ctx)
            x_mid, h2 = _even_out(xs, xs_ctx, u, att, mod_g, ng_l, pool_w[j].astype(BF16), pool_scale[j], w_out,
                                  ctx_tiles, t_off, group)
        else:
            assert xs_ctx is None
            (qt, kc, vt, hq, hk_f, hk_b, lf_f, lf_b, hv, hgate) = _odd_in(
                xs, _group_mod(mod[layer], b, group_odd_in), ng_l, cos_mla, sin_mla, od_w_in[j].astype(BF16),
                mla_q_norm[j], mla_w_uq[j].astype(BF16), mla_kv_norm[j], mla_w_ukv[j].astype(BF16), hgrn_lb,
                layer, ctx_tiles, group_odd_in)
            att = _mla_attn(qt, kc, vt, n_ctx, t_off * (TILE // MLA_TQ))
            o_f = _hgrn_scan(False, lf_f, hq, hk_f, hv, n_ctx // HG_BLOCK)
            hg = _hgrn_scan(True, lf_b, hq, hk_b, hv, n_ctx // HG_BLOCK, (o_f, hgate, hgrn_norm[j]))
            x_mid, h2 = _odd_out(xs, att, hg, mod_g, ng_l, w_out, ctx_tiles, t_off, group)
        xs = _ffn(x_mid, h2, mod_g, ng_l, ffn_w_gate[layer].astype(BF16), ffn_w_up[layer].astype(BF16),
                  ffn_conv_w[layer], ffn_conv_b[layer], ffn_w_down[layer].astype(BF16),
                  0 if last else ctx_tiles, group)
        xs_ctx = None
    return xs
```

```python
import functools
import math

import jax
import jax.numpy as jnp
import numpy as np
from jax import lax
from jax.experimental import pallas as pl
from jax.experimental.pallas import tpu as pltpu

F32 = jnp.float32
BF16 = jnp.bfloat16

D_MODEL = 1024
GRID_W = 64
RMS_EPS = 1e-6
ROPE_THETA = 10000.0
LOG2E = math.log2(math.e)

POOL_WINDOWS = (2, 4, 8, 16)
POOL_GROUPS = 4
POOL_WIDTH = 512
POOL_GROUP_DIM = POOL_WIDTH // POOL_GROUPS

DIFF_HEADS = 4
DIFF_QK_DIM = 64
DIFF_V_DIM = 128
DIFF_QK_WIDTH = 2 * DIFF_HEADS * DIFF_QK_DIM
DIFF_WIDTH = DIFF_HEADS * DIFF_V_DIM

MLA_HEADS = 4
MLA_Q_RANK = 512
MLA_KV_RANK = 256
MLA_NOPE_DIM = 128
MLA_ROPE_DIM = 64
MLA_V_DIM = 128
MLA_QK_PAD = 256
MLA_WIDTH = MLA_HEADS * MLA_V_DIM

HG_HEADS = 4
HG_K_DIM = 128
HG_V_DIM = 128
HG_QK_WIDTH = HG_HEADS * HG_K_DIM
HG_WIDTH = HG_HEADS * HG_V_DIM
HG_CHUNK = 64
HG_LEVELS = (64, 32, 16, 8, 4, 2)

D_FF = 2816
FF_CHUNK = 256

LANES = 128
SUBLANES = 8
BF16_SUBLANES = 16
MXU_TILE = 256

TILE = 256
MAX_GROUP = 4
ODD_IN_GROUP = 4
DIFF_TQ = 128
DIFF_STEP_Q = 256
MLA_TQ = 256
MLA_STEP_TILES = 2
HG_BLOCK = 256
HALO = BF16_SUBLANES
VMEM_LIMIT = 56 * 1024 * 1024
SOFTMAX_DENOM_FLOOR = 2.0 ** -100
SCORE_BOUND_SLACK = 1.0 + 2.0 ** -8


def _cparams(n_axes):
    return pltpu.CompilerParams(
        dimension_semantics=("arbitrary",) * n_axes, vmem_limit_bytes=VMEM_LIMIT)


def _dot(a, b):
    return jnp.dot(a, b, preferred_element_type=F32)


def _dot_nt(a, b):
    return lax.dot_general(a, b, (((1,), (1,)), ((), ())), preferred_element_type=F32)


def _rms(x, g):
    return x * lax.rsqrt(jnp.mean(x * x, axis=-1, keepdims=True) + RMS_EPS) * g


def _silu(x):
    return x * jax.nn.sigmoid(x)


def _const_spec(shape):
    zeros = (0,) * len(shape)
    return pl.BlockSpec(shape, lambda *_: zeros, pipeline_mode=pl.Buffered(1))


def _group_size(b, max_group):
    return max(g for g in range(1, max_group + 1) if b % g == 0)


def _group_mod(mod_l, b, group):
    return jnp.concatenate([mod_l[:b], jnp.broadcast_to(mod_l[b:b + 1], (group,) + mod_l.shape[1:])], axis=0)


def _rope(x, cos, sin_signed):
    n = x.shape[-1]
    lane = lax.broadcasted_iota(jnp.int32, x.shape, 1)
    first_half = (lane & 31) < 16
    partner = jnp.where(first_half, pltpu.roll(x, n - 16, 1), pltpu.roll(x, 16, 1))
    return x * cos + partner * sin_signed


def _prenorm(g, x_ref, mod_ref, ng_ref):
    h = _rms(x_ref[g], ng_ref[0:1, :]) * (1.0 + mod_ref[g, 1:2, :]) + mod_ref[g, 0:1, :]
    return h.astype(BF16)


def _mod_kernel(c_ref, w_ref, b_ref, o_ref):
    s = _silu(c_ref[...])
    w = w_ref[0]
    s_hi = s.astype(BF16)
    s_lo = (s - s_hi.astype(F32)).astype(BF16)
    w_hi = w.astype(BF16)
    w_lo = (w - w_hi.astype(F32)).astype(BF16)
    o_ref[0] = _dot(s_hi, w_hi) + _dot(s_hi, w_lo) + _dot(s_lo, w_hi) + b_ref[0]


def _modulation(cond, ada_w, ada_b):
    depth, d, n = ada_w.shape
    rows = cond.shape[0]
    tn = 1536
    return pl.pallas_call(
        _mod_kernel,
        grid=(depth, n // tn),
        in_specs=[
            pl.BlockSpec((rows, d), lambda l, j: (0, 0)),
            pl.BlockSpec((1, d, tn), lambda l, j: (l, 0, j)),
            pl.BlockSpec((1, 1, tn), lambda l, j: (l, 0, j)),
        ],
        out_specs=pl.BlockSpec((1, rows, tn), lambda l, j: (l, 0, j)),
        out_shape=jax.ShapeDtypeStruct((depth, rows, n), F32),
        compiler_params=_cparams(2),
    )(cond, ada_w, ada_b.reshape(depth, 1, n))


class _SplitRows:
    def __init__(self, ctx_ref, lat_ref, is_ctx):
        self.ctx_ref, self.lat_ref, self.is_ctx = ctx_ref, lat_ref, is_ctx
        self.shape = lat_ref.shape

    def __getitem__(self, g):
        return jnp.where(self.is_ctx, self.ctx_ref[g], self.lat_ref[g])


def _token_rows(split_ctx_tiles, t_off, refs):
    if split_ctx_tiles is None:
        return refs[0], refs[1:]
    return _SplitRows(refs[0], refs[1], pl.program_id(1) + t_off < split_ctx_tiles), refs[2:]


def _token_specs(x, x_ctx, group, ctx_tiles, t_off):
    d = x.shape[2]
    if x_ctx is None:
        return [x], [pl.BlockSpec((group, TILE, d), lambda i, j: (i, j + t_off, 0))]
    return [x_ctx, x], [
        pl.BlockSpec((group, TILE, d), lambda i, j: (i, jnp.minimum(j + t_off, ctx_tiles - 1), 0)),
        pl.BlockSpec((group, TILE, d), lambda i, j: (i, jnp.maximum(j + t_off - ctx_tiles, 0), 0)),
    ]


def _even_in_kernel(split_ctx_tiles, *refs):
    x_ref, (mod_ref, ng_ref, cos_ref, sin_ref, w_ref, u_ref, qt_ref, k_ref, vt_ref) = _token_rows(
        split_ctx_tiles, 0, refs)
    cos = cos_ref[...]
    sin = sin_ref[...]
    o_q = POOL_WIDTH
    o_k = o_q + DIFF_QK_WIDTH
    o_v = o_k + DIFF_QK_WIDTH
    for g in range(x_ref.shape[0]):
        p = _dot(_prenorm(g, x_ref, mod_ref, ng_ref), w_ref[...])
        u_ref[g] = p[:, :o_q].astype(BF16)
        qt_ref[g] = (_rope(p[:, o_q:o_k], cos, sin) * (DIFF_QK_DIM ** -0.5 * LOG2E)).T.astype(BF16)
        k_ref[g] = _rope(p[:, o_k:o_v], cos, sin).astype(BF16)
        vt_ref[g] = p[:, o_v:].T.astype(BF16)


def _even_in(x, x_ctx, mod_g, ng_l, cos, sin, w_in, ctx_tiles, group):
    b, t, d = x.shape
    if x_ctx is not None:
        t += x_ctx.shape[1]
    nt = t // TILE
    n_in = w_in.shape[1]

    def mod_map(i, j):
        return (jnp.where(j < ctx_tiles, b // group, i), 0, 0)

    tok = lambda w: pl.BlockSpec((group, TILE, w), lambda i, j: (i, j, 0))
    tok_t = lambda w: pl.BlockSpec((group, w, TILE), lambda i, j: (i, 0, j))
    tokens, token_specs = _token_specs(x, x_ctx, group, ctx_tiles, 0)
    return pl.pallas_call(
        functools.partial(_even_in_kernel, None if x_ctx is None else ctx_tiles),
        grid=(b // group, nt),
        in_specs=token_specs + [
            pl.BlockSpec((group, 6, d), mod_map),
            _const_spec((4, d)),
            pl.BlockSpec((TILE, DIFF_QK_WIDTH), lambda i, j: (j, 0)),
            pl.BlockSpec((TILE, DIFF_QK_WIDTH), lambda i, j: (j, 0)),
            _const_spec((d, n_in)),
        ],
        out_specs=[tok(POOL_WIDTH), tok_t(DIFF_QK_WIDTH), tok(DIFF_QK_WIDTH), tok_t(DIFF_WIDTH)],
        out_shape=[
            jax.ShapeDtypeStruct((b, t, POOL_WIDTH), BF16),
            jax.ShapeDtypeStruct((b, DIFF_QK_WIDTH, t), BF16),
            jax.ShapeDtypeStruct((b, t, DIFF_QK_WIDTH), BF16),
            jax.ShapeDtypeStruct((b, DIFF_WIDTH, t), BF16),
        ],
        compiler_params=_cparams(2),
    )(*tokens, mod_g, ng_l, cos, sin, w_in)


def _key_halves(nk):
    if nk <= MXU_TILE:
        return ((0, nk),)
    first = -(-(nk // 2) // LANES) * LANES
    return ((0, first), (first, nk))


def _attend_t(n_heads, nk, scores_operands, values_t, key_sq, s_ref, e_ref, finish):
    halves = _key_halves(nk)

    def weighted_values(h):
        vt = values_t(h)
        return sum(_dot(vt[:, a:b], e_ref[h % 2, a:b, :]) for a, b in halves)

    l_low = None
    for h in range(n_heads):
        keys, q_t = scores_operands(h)
        qf = q_t.astype(F32)
        bound = jnp.sqrt(jnp.sum(qf * qf, axis=0, keepdims=True) * key_sq(h)) * SCORE_BOUND_SLACK
        l = None
        for a, b in halves:
            e = jnp.exp2(_dot(keys[a:b, :], q_t) - bound)
            e_ref[h % 2, a:b, :] = e.astype(BF16)
            part = jnp.sum(e, axis=0, keepdims=True)
            l = part if l is None else l + part
        finish(h, weighted_values(h), l)
        l_low = l if l_low is None else jnp.minimum(l_low, l)
    accurate = jnp.min(l_low) > SOFTMAX_DENOM_FLOOR

    @pl.when(jnp.logical_not(accurate))
    def _():
        def scores(h):
            keys, q_t = scores_operands(h)
            tops = []
            for a, b in halves:
                s = _dot(keys[a:b, :], q_t)
                s_ref[h % 2, a:b, :] = s
                tops.append(jnp.max(s, axis=0, keepdims=True))
            return functools.reduce(jnp.maximum, tops)

        top = scores(0)
        for h in range(n_heads):
            nxt = scores(h + 1) if h + 1 < n_heads else None
            e = jnp.exp2(s_ref[h % 2, :nk, :] - top)
            e_ref[h % 2, :nk, :] = e.astype(BF16)
            finish(h, weighted_values(h), jnp.sum(e, axis=0, keepdims=True))
            top = nxt


def _store_key_sq(ksq_ref, row, cols, keys_sq, dim_weights, n_ctx):
    n = cols.stop - cols.start
    for r, rows in ((row, slice(0, n_ctx)), (row + ksq_ref.shape[0] // 2, slice(None))):
        dim_max = jnp.max(keys_sq[rows], axis=0, keepdims=True)
        ksq_ref[r:r + 1, cols] = jnp.broadcast_to(jnp.sum(dim_max * dim_weights, axis=1, keepdims=True), (1, n))


def _diff_attn_kernel(lam_init, n_ctx, qt_ref, k_ref, vt_ref, lam_ref, g_ref, o_ref, s_ref, e_ref, ksq_ref):
    lv = lam_ref[...]
    lam = (jnp.exp(jnp.sum(lv[0:1] * lv[1:2], axis=-1, keepdims=True))
           - jnp.exp(jnp.sum(lv[2:3] * lv[3:4], axis=-1, keepdims=True)) + lam_init)
    gain = g_ref[...] * (1.0 - lam_init)
    row = lax.broadcasted_iota(jnp.int32, (LANES, DIFF_TQ), 0)

    @pl.when(pl.program_id(1) == 0)
    def _():
        first_head = (lax.broadcasted_iota(jnp.int32, (1, LANES), 1) < DIFF_QK_DIM).astype(F32)
        for h in range(DIFF_HEADS):
            kf = k_ref[0, :, h * LANES:(h + 1) * LANES].astype(F32)
            _store_key_sq(ksq_ref, h, slice(0, DIFF_TQ), kf * kf, first_head, n_ctx)
            _store_key_sq(ksq_ref, h, slice(DIFF_TQ, 2 * DIFF_TQ), kf * kf, 1.0 - first_head, n_ctx)

    q_tiles = DIFF_STEP_Q // DIFF_TQ

    def attend(nk, ctx_only):
        ksq_row = 0 if ctx_only else DIFF_HEADS

        def scores_operands(p):
            h, qi = divmod(p, q_tiles)
            cols = slice(h * LANES, (h + 1) * LANES)
            qp = qt_ref[0, cols, qi * DIFF_TQ:(qi + 1) * DIFF_TQ]
            q_bd = jnp.concatenate(
                [jnp.where(row < DIFF_QK_DIM, qp, 0), jnp.where(row >= DIFF_QK_DIM, qp, 0)], axis=1)
            return k_ref.at[0, :nk, cols], q_bd

        def finish(p, o_t, l):
            h, qi = divmod(p, q_tiles)
            r = 1.0 / l
            o = o_t[:, :DIFF_TQ] * r[:, :DIFF_TQ] - o_t[:, DIFF_TQ:] * (lam * r[:, DIFF_TQ:])
            o = o * lax.rsqrt(jnp.mean(o * o, axis=0, keepdims=True) + RMS_EPS) * gain
            o_ref[0, qi * DIFF_TQ:(qi + 1) * DIFF_TQ, h * LANES:(h + 1) * LANES] = o.T.astype(BF16)

        def values_t(p):
            h = p // q_tiles
            return vt_ref.at[0, h * LANES:(h + 1) * LANES, :nk]

        def key_sq(p):
            r = ksq_row + p // q_tiles
            return ksq_ref[r:r + 1, :]

        _attend_t(DIFF_HEADS * q_tiles, nk, scores_operands, values_t, key_sq, s_ref, e_ref, finish)

    is_ctx = pl.program_id(1) < n_ctx // DIFF_STEP_Q

    @pl.when(is_ctx)
    def _():
        attend(n_ctx, True)

    @pl.when(jnp.logical_not(is_ctx))
    def _():
        attend(k_ref.shape[1], False)


def _diff_attn(qt, k, vt, lam_vec, subln, lam_init, n_ctx):
    b, t, _ = k.shape
    per_sample = lambda shape: pl.BlockSpec(shape, lambda i, j: (i, 0, 0), pipeline_mode=pl.Buffered(1))
    return pl.pallas_call(
        functools.partial(_diff_attn_kernel, lam_init, n_ctx),
        grid=(b, t // DIFF_STEP_Q),
        in_specs=[
            pl.BlockSpec((1, DIFF_QK_WIDTH, DIFF_STEP_Q), lambda i, j: (i, 0, j)),
            per_sample((1, t, DIFF_QK_WIDTH)),
            per_sample((1, DIFF_WIDTH, t)),
            _const_spec((4, DIFF_QK_DIM)),
            _const_spec((DIFF_V_DIM, 1)),
        ],
        out_specs=pl.BlockSpec((1, DIFF_STEP_Q, DIFF_WIDTH), lambda i, j: (i, j, 0)),
        out_shape=jax.ShapeDtypeStruct((b, t, DIFF_WIDTH), BF16),
        scratch_shapes=[pltpu.VMEM((2, t, 2 * DIFF_TQ), F32), pltpu.VMEM((2, t, 2 * DIFF_TQ), BF16),
                        pltpu.VMEM((2 * DIFF_HEADS, 2 * DIFF_TQ), F32)],
        compiler_params=_cparams(2),
    )(qt, k, vt, lam_vec, subln.reshape(DIFF_V_DIM, 1))


def _residual_and_prenorm(g, x_ref, y, mod_ref, ng_ref, xo_ref, h2_ref):
    xn = x_ref[g] + mod_ref[g, 2:3, :] * _rms(y, ng_ref[1:2, :])
    xo_ref[g] = xn
    h2 = _rms(xn, ng_ref[2:3, :]) * (1.0 + mod_ref[g, 4:5, :]) + mod_ref[g, 3:4, :]
    h2_ref[g] = h2.astype(BF16)


def _even_out_kernel(split_ctx_tiles, t_off, seg_tiles, n_tiles, *refs):
    x_ref, (u_ref, up_ref, un_ref, a_ref, mod_ref, ng_ref, pw_ref, ps_ref, wo_ref, xo_ref, h2_ref,
            ext_ref) = _token_rows(split_ctx_tiles, t_off, refs)
    j = pl.program_id(1) + t_off
    prev_ok = jnp.logical_and(j != 0, j != seg_tiles)
    next_ok = jnp.logical_and(j != n_tiles - 1, j != seg_tiles - 1)
    lo = jnp.where(prev_ok, -HALO, 0)
    hi = jnp.where(next_ok, TILE + HALO, TILE)
    row = lax.broadcasted_iota(jnp.int32, (TILE, 1), 0)
    for g in range(x_ref.shape[0]):
        ext_ref[g, 0:HALO, :] = jnp.where(prev_ok, up_ref[g].astype(F32), 0.0)
        ext_ref[g, HALO:HALO + TILE, :] = u_ref[g].astype(F32)
        ext_ref[g, HALO + TILE:, :] = jnp.where(next_ok, un_ref[g].astype(F32), 0.0)
        parts = []
        for gidx, win in enumerate(POOL_WINDOWS):
            half = win // 2
            cols = slice(gidx * POOL_GROUP_DIM, (gidx + 1) * POOL_GROUP_DIM)
            acc = ext_ref[g, HALO - half:HALO - half + TILE, cols]
            for off in range(-half + 1, half):
                acc = acc + ext_ref[g, HALO + off:HALO + off + TILE, cols]
            cnt = jnp.minimum(row + half, hi) - jnp.maximum(row - half, lo)
            dlt = acc / cnt.astype(F32) - ext_ref[g, HALO:HALO + TILE, cols]
            parts.append(_dot(dlt.astype(BF16), pw_ref[gidx]))
        yp = jnp.concatenate(parts, axis=1) * ps_ref[...]
        y = _dot(yp.astype(BF16), wo_ref[:POOL_WIDTH, :]) + _dot(a_ref[g], wo_ref[POOL_WIDTH:, :])
        _residual_and_prenorm(g, x_ref, y, mod_ref, ng_ref, xo_ref, h2_ref)


def _even_out(x, x_ctx, u, att, mod_g, ng_l, pool_w, pool_scale, w_out, ctx_tiles, t_off, group):
    b, t, d = x.shape
    if x_ctx is not None:
        t += x_ctx.shape[1]
    nt = t // TILE
    nh = TILE // HALO
    n_out = nt - t_off

    def mod_map(i, j):
        return (jnp.where(j + t_off < ctx_tiles, b // group, i), 0, 0)

    tok = lambda w: pl.BlockSpec((group, TILE, w), lambda i, j: (i, j + t_off, 0))
    out_tok = lambda: pl.BlockSpec((group, TILE, d), lambda i, j: (i, j, 0))
    tokens, token_specs = _token_specs(x, x_ctx, group, ctx_tiles, t_off)
    return pl.pallas_call(
        functools.partial(_even_out_kernel, None if x_ctx is None else ctx_tiles, t_off, ctx_tiles, nt),
        grid=(b // group, n_out),
        in_specs=token_specs + [
            tok(POOL_WIDTH),
            pl.BlockSpec((group, HALO, POOL_WIDTH), lambda i, j: (i, jnp.maximum((j + t_off) * nh - 1, 0), 0)),
            pl.BlockSpec((group, HALO, POOL_WIDTH),
                         lambda i, j: (i, jnp.minimum((j + t_off + 1) * nh, nt * nh - 1), 0)),
            tok(DIFF_WIDTH),
            pl.BlockSpec((group, 6, d), mod_map),
            _const_spec((4, d)),
            _const_spec((POOL_GROUPS, POOL_GROUP_DIM, POOL_GROUP_DIM)),
            _const_spec((1, POOL_WIDTH)),
            _const_spec((POOL_WIDTH + DIFF_WIDTH, d)),
        ],
        out_specs=[out_tok(), out_tok()],
        out_shape=[
            jax.ShapeDtypeStruct((b, n_out * TILE, d), F32),
            jax.ShapeDtypeStruct((b, n_out * TILE, d), BF16),
        ],
        scratch_shapes=[pltpu.VMEM((group, TILE + 2 * HALO, POOL_WIDTH), F32)],
        compiler_params=_cparams(2),
    )(*tokens, u, u, u, att, mod_g, ng_l, pool_w, pool_scale.reshape(1, POOL_WIDTH), w_out)


def _ffn_kernel(seg_tiles, n_tiles, x_ref, h_ref, hp_ref, hn_ref, mod_ref, ng_ref, wg_ref, wu_ref,
                cw_ref, cb_ref, wd_ref, o_ref, a_ref, act_ref):
    j = pl.program_id(1)
    prev_ok = jnp.logical_and(j != 0, j != seg_tiles)
    next_ok = jnp.logical_and(j != n_tiles - 1, j != seg_tiles - 1)
    group = x_ref.shape[0]
    ext = TILE + 2 * HALO
    zero = jnp.zeros((HALO, h_ref.shape[2]), BF16)
    pieces = []
    for g in range(group):
        pieces += [jnp.where(prev_ok, hp_ref[g], zero), h_ref[g], jnp.where(next_ok, hn_ref[g], zero)]
    h_ext = jnp.concatenate(pieces, axis=0)
    h = jnp.concatenate([h_ref[g] for g in range(group)], axis=0) if group > 1 else h_ref[0]
    for c in range(D_FF // FF_CHUNK):
        cols = slice(c * FF_CHUNK, (c + 1) * FF_CHUNK)
        a_ref[...] = _dot(h_ext, wg_ref[:, cols])
        up = _dot(h, wu_ref[:, cols])
        for g in range(group):
            base = g * ext + HALO
            a = (a_ref[base - 1:base - 1 + TILE, :] * cw_ref[0:1, cols]
                 + a_ref[base:base + TILE, :] * cw_ref[1:2, cols]
                 + a_ref[base + 1:base + 1 + TILE, :] * cw_ref[2:3, cols] + cb_ref[:, cols])
            act_ref[g * TILE:(g + 1) * TILE, cols] = (_silu(a) * up[g * TILE:(g + 1) * TILE]).astype(BF16)
    for g in range(group):
        f = _dot(act_ref[g * TILE:(g + 1) * TILE, :], wd_ref[...])
        o_ref[g] = x_ref[g] + mod_ref[g, 5:6, :] * _rms(f, ng_ref[3:4, :])


def _ffn(x, h2, mod_g, ng_l, w_gate, w_up, conv_w, conv_b, w_down, ctx_tiles, group):
    b, t, d = x.shape
    nt = t // TILE
    nh = TILE // HALO

    def mod_map(i, j):
        return (jnp.where(j < ctx_tiles, b // group, i), 0, 0)

    tok = lambda: pl.BlockSpec((group, TILE, d), lambda i, j: (i, j, 0))
    return pl.pallas_call(
        functools.partial(_ffn_kernel, ctx_tiles, nt),
        grid=(b // group, nt),
        in_specs=[
            tok(),
            tok(),
            pl.BlockSpec((group, HALO, d), lambda i, j: (i, jnp.maximum(j * nh - 1, 0), 0)),
            pl.BlockSpec((group, HALO, d), lambda i, j: (i, jnp.minimum((j + 1) * nh, nt * nh - 1), 0)),
            pl.BlockSpec((group, 6, d), mod_map),
            _const_spec((4, d)),
            _const_spec((d, D_FF)),
            _const_spec((d, D_FF)),
            _const_spec((3, D_FF)),
            _const_spec((1, D_FF)),
            _const_spec((D_FF, d)),
        ],
        out_specs=tok(),
        out_shape=jax.ShapeDtypeStruct((b, t, d), F32),
        scratch_shapes=[pltpu.VMEM((group * (TILE + 2 * HALO), FF_CHUNK), F32),
                        pltpu.VMEM((group * TILE, D_FF), BF16)],
        compiler_params=_cparams(2),
    )(x, h2, h2, h2, mod_g, ng_l, w_gate, w_up, conv_w, conv_b.reshape(1, D_FF), w_down)


def _odd_in_kernel(layer, x_ref, mod_ref, ng_ref, cos_ref, sin_ref, wq_ref, wkv_ref, wkr_ref, wh_ref,
                   qn_ref, wuq_ref, kvn_ref, wukv_ref, lb_ref,
                   qt_ref, kc_ref, vt_ref, hq_ref, hkf_ref, hkb_ref, lff_ref, lfb_ref, hv_ref, hg_ref):
    cos = cos_ref[...]
    sin = sin_ref[...]
    w = HG_QK_WIDTH
    lbs = []
    for direction in range(2):
        lrows = [lb_ref[direction, i:i + 1, :] for i in range(lb_ref.shape[1])]
        top = functools.reduce(jnp.maximum, lrows)
        ex = [jnp.exp(r - top) for r in lrows]
        lbs.append(sum(ex[1:layer + 1], jnp.zeros_like(top)) / sum(ex))

    for g in range(x_ref.shape[0]):
        hb = _prenorm(g, x_ref, mod_ref, ng_ref)
        cq = _rms(_dot(hb, wq_ref[...]), qn_ref[...])
        q = _dot(cq.astype(BF16), wuq_ref[...]) * ((MLA_NOPE_DIM + MLA_ROPE_DIM) ** -0.5 * LOG2E)
        ckv = _rms(_dot(hb, wkv_ref[...]), kvn_ref[...])
        kv = _dot(ckv.astype(BF16), wukv_ref[...])
        kr = _rope(_dot(hb, wkr_ref[...]), cos, sin).astype(BF16)
        for hd in range(MLA_HEADS):
            qb = hd * MLA_QK_PAD
            q_h = jnp.concatenate(
                [q[:, qb:qb + MLA_NOPE_DIM], _rope(q[:, qb + MLA_NOPE_DIM:qb + MLA_QK_PAD], cos, sin)], axis=1)
            qt_ref[g, hd] = q_h.T.astype(BF16)
            kb = hd * (MLA_NOPE_DIM + MLA_V_DIM)
            kc_ref[g, hd, :, :MLA_NOPE_DIM] = kv[:, kb:kb + MLA_NOPE_DIM].astype(BF16)
            kc_ref[g, hd, :, MLA_NOPE_DIM:] = kr
            vt_ref[g, hd * MLA_V_DIM:(hd + 1) * MLA_V_DIM, :] = kv[
                :, kb + MLA_NOPE_DIM:kb + MLA_NOPE_DIM + MLA_V_DIM].T.astype(BF16)

        hq_ref[g] = _silu(_dot(hb, wh_ref[:, :w])).astype(BF16)
        hv_ref[g] = _dot(hb, wh_ref[:, 3 * w:4 * w]).astype(BF16)
        hg_ref[g] = _silu(_dot(hb, wh_ref[:, 4 * w:])).astype(BF16)
        for direction, (k_ref, lf_ref) in enumerate(((hkf_ref, lff_ref), (hkb_ref, lfb_ref))):
            lb = lbs[direction]
            f = lb + (1.0 - lb) * jax.nn.sigmoid(_dot(hb, wh_ref[:, (1 + direction) * w:(2 + direction) * w]))
            k_ref[g] = (1.0 - f).astype(BF16)
            lf_ref[g] = jnp.log2(f)


def _odd_in(x, mod_g, ng_l, cos, sin, w_in, q_norm, w_uq, kv_norm, w_ukv, hgrn_lb, layer, ctx_tiles, group):
    b, t, d = x.shape
    nt = t // TILE
    o1 = MLA_Q_RANK
    o2 = o1 + MLA_KV_RANK
    o3 = o2 + MLA_ROPE_DIM
    w_q = w_in[:, :o1]
    w_kv = w_in[:, o1:o2]
    w_kr = jnp.pad(w_in[:, o2:o3], ((0, 0), (0, LANES - MLA_ROPE_DIM)))
    w_h = w_in[:, o3:]
    qk = MLA_NOPE_DIM + MLA_ROPE_DIM
    w_uq_pad = jnp.pad(w_uq.reshape(MLA_Q_RANK, MLA_HEADS, qk),
                       ((0, 0), (0, 0), (0, MLA_QK_PAD - qk))).reshape(MLA_Q_RANK, MLA_HEADS * MLA_QK_PAD)

    def mod_map(i, j):
        return (jnp.where(j < ctx_tiles, b // group, i), 0, 0)

    tok = lambda w: pl.BlockSpec((group, TILE, w), lambda i, j: (i, j, 0))
    sds = lambda w, dt: jax.ShapeDtypeStruct((b, t, w), dt)
    hw = HG_QK_WIDTH
    return pl.pallas_call(
        functools.partial(_odd_in_kernel, layer),
        grid=(b // group, nt),
        in_specs=[
            tok(d),
            pl.BlockSpec((group, 6, d), mod_map),
            _const_spec((4, d)),
            pl.BlockSpec((TILE, LANES), lambda i, j: (j, 0)),
            pl.BlockSpec((TILE, LANES), lambda i, j: (j, 0)),
            _const_spec(w_q.shape),
            _const_spec(w_kv.shape),
            _const_spec(w_kr.shape),
            _const_spec(w_h.shape),
            _const_spec((1, MLA_Q_RANK)),
            _const_spec(w_uq_pad.shape),
            _const_spec((1, MLA_KV_RANK)),
            _const_spec(w_ukv.shape),
            _const_spec(hgrn_lb.shape),
        ],
        out_specs=[
            pl.BlockSpec((group, MLA_HEADS, MLA_QK_PAD, TILE), lambda i, j: (i, 0, 0, j)),
            pl.BlockSpec((group, MLA_HEADS, TILE, MLA_QK_PAD), lambda i, j: (i, 0, j, 0)),
            pl.BlockSpec((group, MLA_WIDTH, TILE), lambda i, j: (i, 0, j)),
            tok(hw), tok(hw), tok(hw), tok(hw), tok(hw), tok(hw), tok(hw),
        ],
        out_shape=[
            jax.ShapeDtypeStruct((b, MLA_HEADS, MLA_QK_PAD, t), BF16),
            jax.ShapeDtypeStruct((b, MLA_HEADS, t, MLA_QK_PAD), BF16),
            jax.ShapeDtypeStruct((b, MLA_WIDTH, t), BF16),
            sds(hw, BF16), sds(hw, BF16), sds(hw, BF16), sds(hw, F32), sds(hw, F32), sds(hw, BF16),
            sds(hw, BF16),
        ],
        compiler_params=_cparams(2),
    )(x, mod_g, ng_l, cos, sin, w_q, w_kv, w_kr, w_h, q_norm.reshape(1, -1), w_uq_pad,
      kv_norm.reshape(1, -1), w_ukv, hgrn_lb)


def _mla_attn_kernel(n_ctx, q_off, q_tiles, *refs):
    qt_refs = refs[:q_tiles]
    kc_ref, vt_ref, o_ref, s_ref, e_ref, ksq_ref = refs[q_tiles:]

    @pl.when(pl.program_id(1) == 0)
    def _():
        all_dims = jnp.ones((1, MLA_QK_PAD), F32)
        for h in range(MLA_HEADS):
            kf = kc_ref[0, h].astype(F32)
            _store_key_sq(ksq_ref, h, slice(0, MLA_TQ), kf * kf, all_dims, n_ctx)

    def attend(nk, ctx_only):
        ksq_row = 0 if ctx_only else MLA_HEADS

        def finish(p, o_t, l):
            h, qi = divmod(p, q_tiles)
            o_ref[0, qi * MLA_TQ:(qi + 1) * MLA_TQ, h * MLA_V_DIM:(h + 1) * MLA_V_DIM] = (
                o_t * (1.0 / l)).T.astype(BF16)

        def scores_operands(p):
            h, qi = divmod(p, q_tiles)
            return kc_ref.at[0, h, :nk, :], qt_refs[qi][0, h]

        def values_t(p):
            h = p // q_tiles
            return vt_ref.at[0, h * MLA_V_DIM:(h + 1) * MLA_V_DIM, :nk]

        def key_sq(p):
            r = ksq_row + p // q_tiles
            return ksq_ref[r:r + 1, :]

        _attend_t(MLA_HEADS * q_tiles, nk, scores_operands, values_t, key_sq, s_ref, e_ref, finish)

    if q_off * MLA_TQ >= n_ctx:
        attend(kc_ref.shape[2], False)
    else:
        is_ctx = pl.program_id(1) + q_off < n_ctx // MLA_TQ

        @pl.when(is_ctx)
        def _():
            attend(n_ctx, True)

        @pl.when(jnp.logical_not(is_ctx))
        def _():
            attend(kc_ref.shape[2], False)


def _mla_attn(qt, kc, vt, n_ctx, q_off):
    b, _, t, _ = kc.shape
    nq = t // MLA_TQ - q_off
    q_tiles = MLA_STEP_TILES if (q_off * MLA_TQ >= n_ctx and nq % MLA_STEP_TILES == 0) else 1

    def q_spec(qi):
        return pl.BlockSpec((1, MLA_HEADS, MLA_QK_PAD, MLA_TQ), lambda i, j: (i, 0, 0, j * q_tiles + qi + q_off))

    return pl.pallas_call(
        functools.partial(_mla_attn_kernel, n_ctx, q_off, q_tiles),
        grid=(b, nq // q_tiles),
        in_specs=[q_spec(qi) for qi in range(q_tiles)] + [
            pl.BlockSpec((1, MLA_HEADS, t, MLA_QK_PAD), lambda i, j: (i, 0, 0, 0), pipeline_mode=pl.Buffered(1)),
            pl.BlockSpec((1, MLA_WIDTH, t), lambda i, j: (i, 0, 0), pipeline_mode=pl.Buffered(1)),
        ],
        out_specs=pl.BlockSpec((1, q_tiles * MLA_TQ, MLA_WIDTH), lambda i, j: (i, j, 0)),
        out_shape=jax.ShapeDtypeStruct((b, nq * MLA_TQ, MLA_WIDTH), BF16),
        scratch_shapes=[pltpu.VMEM((2, t, MLA_TQ), F32), pltpu.VMEM((2, t, MLA_TQ), BF16),
                        pltpu.VMEM((2 * MLA_HEADS, MLA_TQ), F32)],
        compiler_params=_cparams(2),
    )(*([qt] * q_tiles), kc, vt)


def _hgrn_triangle(reverse):
    t = np.arange(HG_CHUNK)[:, None]
    j = np.arange(HG_CHUNK)[None, :]
    return ((j >= t) if reverse else (j <= t)).astype(np.float32)


def _hgrn_masks(reverse):
    c = HG_CHUNK
    t = lax.broadcasted_iota(jnp.int32, (c, c), 0)
    s = lax.broadcasted_iota(jnp.int32, (c, c), 1)
    tq = lax.broadcasted_iota(jnp.int32, (c, 1), 0)
    masks = []
    for m in HG_LEVELS:
        half = m // 2
        shift = int(math.log2(m))
        t_hi = (t & (m - 1)) >= half
        s_hi = (s & (m - 1)) >= half
        same = (t >> shift) == (s >> shift)
        if not reverse:
            pair = jnp.where(same, jnp.where(t_hi, jnp.where(s_hi, 0.0, 1.0), 0.0), 0.0)
            is_q = (tq & (m - 1)) >= half
        else:
            pair = jnp.where(same, jnp.where(t_hi, 0.0, jnp.where(s_hi, 1.0, 0.0)), 0.0)
            is_q = (tq & (m - 1)) < half
        masks.append((pair > 0.5, is_q))
    return masks, t == s


def _hgrn_reference_rows(cum, m, reverse):
    c = HG_CHUNK
    half = m // 2
    ref_in_block = half if reverse else half - 1
    if m >= 2 * SUBLANES:
        return jnp.concatenate(
            [jnp.broadcast_to(cum[b0 + ref_in_block:b0 + ref_in_block + 1, :], (m, LANES))
             for b0 in range(0, c, m)], axis=0)
    cum3 = cum.reshape(c // SUBLANES, SUBLANES, LANES)
    r = lax.broadcasted_iota(jnp.int32, cum3.shape, 1)
    out = None
    for b0 in reversed(range(0, SUBLANES, m)):
        pick = jnp.broadcast_to(cum3[:, b0 + ref_in_block:b0 + ref_in_block + 1, :], cum3.shape)
        out = pick if out is None else jnp.where(r < b0 + m, pick, out)
    return out.reshape(c, LANES)


def _hgrn_scan_body(reverse, tri_ref, lf_ref, q_ref, k_ref, v_ref, st_ref, emit):
    masks, eye = _hgrn_masks(reverse)
    tri = tri_ref[...]
    c = HG_CHUNK
    n_chunks = HG_BLOCK // c
    chunk_order = list(range(n_chunks - 1, -1, -1) if reverse else range(n_chunks))
    probs = [(ci, h) for ci in chunk_order for h in range(HG_HEADS)]
    where = lambda ci, h: (slice(ci * c, (ci + 1) * c), slice(h * HG_K_DIM, (h + 1) * HG_K_DIM))

    @pl.when(pl.program_id(1) == 0)
    def _():
        st_ref[...] = jnp.zeros_like(st_ref)

    cums = []
    for p in probs:
        rows, cols = where(*p)
        lf = lf_ref[0, rows, cols]
        hi = lf.astype(BF16)
        r1 = lf - hi.astype(F32)
        mid = r1.astype(BF16)
        lo = (r1 - mid.astype(F32)).astype(BF16)
        c3 = _dot(tri, jnp.concatenate([hi, mid, lo], axis=1))
        cums.append(c3[:, :LANES] + c3[:, LANES:2 * LANES] + c3[:, 2 * LANES:])

    qf, kf, qe, ks, decay, scores = [], [], [], [], [], []
    for p, cum in zip(probs, cums):
        rows, cols = where(*p)
        total = cum[0:1] if reverse else cum[c - 1:c]
        q = q_ref[0, rows, cols]
        k = k_ref[0, rows, cols]
        qf.append(q.astype(F32))
        kf.append(k.astype(F32))
        qe.append((qf[-1] * jnp.exp2(cum)).astype(BF16))
        ks.append((kf[-1] * jnp.exp2(total - cum)).astype(BF16))
        decay.append(jnp.exp2(total))
        scores.append(jnp.where(eye, _dot_nt(q, k), 0.0))

    for m, (pair, is_q) in zip(HG_LEVELS, masks):
        for i, cum in enumerate(cums):
            d = cum - _hgrn_reference_rows(cum, m, reverse)
            x = (jnp.where(is_q, qf[i], kf[i]) * jnp.exp2(-jnp.abs(d))).astype(BF16)
            scores[i] = jnp.where(pair, _dot_nt(x, x), scores[i])

    intra, update = [], []
    for i, p in enumerate(probs):
        rows, cols = where(*p)
        v = v_ref[0, rows, cols]
        intra.append(_dot(scores[i].astype(BF16), v))
        update.append(_dot(v.astype(F32).T.astype(BF16), ks[i]))

    for h in range(HG_HEADS):
        st = st_ref[h]
        for n in range(n_chunks):
            i = n * HG_HEADS + h
            rows, cols = where(*probs[i])
            emit(rows, cols, intra[i] + _dot_nt(qe[i], st.astype(BF16)))
            st = st * decay[i] + update[i]
        st_ref[h] = st


def _hgrn_fwd_kernel(tri_ref, lf_ref, q_ref, k_ref, v_ref, o_ref, st_ref):
    def emit(rows, cols, o):
        o_ref[0, rows, cols] = o

    _hgrn_scan_body(False, tri_ref, lf_ref, q_ref, k_ref, v_ref, st_ref, emit)


def _hgrn_bwd_kernel(tri_ref, lf_ref, q_ref, k_ref, v_ref, of_ref, gate_ref, gn_ref, o_ref, st_ref):
    def emit(rows, cols, o):
        o_ref[0, rows, cols] = (_rms(o + of_ref[0, rows, cols], gn_ref[...])
                                * gate_ref[0, rows, cols].astype(F32)).astype(BF16)

    _hgrn_scan_body(True, tri_ref, lf_ref, q_ref, k_ref, v_ref, st_ref, emit)


def _hgrn_scan(reverse, lf, q, k, v, ctx_blocks, extra=()):
    b, t, w = lf.shape
    nb = t // HG_BLOCK
    tri = jnp.asarray(_hgrn_triangle(reverse), BF16)
    if reverse:
        blk = lambda j: jnp.where(j < ctx_blocks, ctx_blocks - 1 - j, nb - 1 - (j - ctx_blocks))
    else:
        blk = lambda j: j
    tok = pl.BlockSpec((1, HG_BLOCK, w), lambda i, j: (i, blk(j), 0))
    in_specs = [_const_spec(tri.shape), tok, tok, tok, tok]
    args = [tri, lf, q, k, v]
    if reverse:
        o_f, gate, gnorm = extra
        in_specs += [tok, tok, _const_spec((1, HG_V_DIM))]
        args += [o_f, gate, gnorm.reshape(1, HG_V_DIM)]
    return pl.pallas_call(
        _hgrn_bwd_kernel if reverse else _hgrn_fwd_kernel,
        grid=(b, nb),
        in_specs=in_specs,
        out_specs=tok,
        out_shape=jax.ShapeDtypeStruct((b, t, w), BF16 if reverse else F32),
        scratch_shapes=[pltpu.VMEM((HG_HEADS, HG_V_DIM, HG_K_DIM), F32)],
        compiler_params=_cparams(2),
    )(*args)


def _odd_out_kernel(x_ref, a_ref, g_ref, mod_ref, ng_ref, wo_ref, xo_ref, h2_ref):
    for g in range(x_ref.shape[0]):
        y = _dot(a_ref[g], wo_ref[:MLA_WIDTH, :]) + _dot(g_ref[g], wo_ref[MLA_WIDTH:, :])
        _residual_and_prenorm(g, x_ref, y, mod_ref, ng_ref, xo_ref, h2_ref)


def _odd_out(x, att, hg, mod_g, ng_l, w_out, ctx_tiles, t_off, group):
    b, t, d = x.shape
    n_out = t // TILE - t_off

    def mod_map(i, j):
        return (jnp.where(j + t_off < ctx_tiles, b // group, i), 0, 0)

    out_tok = lambda w: pl.BlockSpec((group, TILE, w), lambda i, j: (i, j, 0))
    return pl.pallas_call(
        _odd_out_kernel,
        grid=(b // group, n_out),
        in_specs=[
            pl.BlockSpec((group, TILE, d), lambda i, j: (i, j + t_off, 0)),
            out_tok(MLA_WIDTH),
            pl.BlockSpec((group, TILE, HG_WIDTH), lambda i, j: (i, j + t_off, 0)),
            pl.BlockSpec((group, 6, d), mod_map),
            _const_spec((4, d)),
            _const_spec((MLA_WIDTH + HG_WIDTH, d)),
        ],
        out_specs=[out_tok(d), out_tok(d)],
        out_shape=[
            jax.ShapeDtypeStruct((b, n_out * TILE, d), F32),
            jax.ShapeDtypeStruct((b, n_out * TILE, d), BF16),
        ],
        compiler_params=_cparams(2),
    )(x, att, hg, mod_g, ng_l, w_out)


def _rope_tables(n_ctx, n_lat):
    rows = n_lat // GRID_W
    pos = jnp.stack([jnp.repeat(jnp.arange(rows), GRID_W), jnp.tile(jnp.arange(GRID_W), rows)], axis=-1)
    axis_dim = DIFF_QK_DIM // 2
    inv_freq = ROPE_THETA ** (-jnp.arange(0, axis_dim, 2, dtype=F32) / axis_dim)
    ang = pos.astype(F32)[..., None] * inv_freq
    cos = jnp.cos(ang)
    sin = jnp.sin(ang)
    cos64 = jnp.concatenate([cos[:, 0], cos[:, 0], cos[:, 1], cos[:, 1]], axis=-1)
    sin64 = jnp.concatenate([-sin[:, 0], sin[:, 0], -sin[:, 1], sin[:, 1]], axis=-1)
    cos64 = jnp.concatenate([jnp.ones((n_ctx, 64), F32), cos64], axis=0)
    sin64 = jnp.concatenate([jnp.zeros((n_ctx, 64), F32), sin64], axis=0)
    return cos64, sin64


def kernel(x, c, ctx, c_ctx, ada_w, ada_b, norm_g, mix_w_out, ffn_w_gate, ffn_w_up, ffn_conv_w, ffn_conv_b,
           ffn_w_down, ev_w_in, pool_w, pool_scale, diff_lambda, diff_subln, od_w_in, mla_q_norm, mla_w_uq,
           mla_kv_norm, mla_w_ukv, hgrn_norm, hgrn_lb):
    b, n_lat, d = x.shape
    n_ctx = ctx.shape[1]
    depth = ada_w.shape[0]
    assert d == D_MODEL and n_ctx % TILE == 0 and n_lat % TILE == 0 and n_lat % GRID_W == 0
    ctx_tiles = n_ctx // TILE
    group = _group_size(b, MAX_GROUP)
    group_odd_in = _group_size(b, ODD_IN_GROUP)

    rows = -(-(b + 1) // SUBLANES) * SUBLANES
    cond = jnp.concatenate([c, c_ctx[None, :], jnp.zeros((rows - b - 1, d), F32)], axis=0)
    mod = _modulation(cond, ada_w, ada_b).reshape(depth, rows, 6, d)

    cos64, sin64 = _rope_tables(n_ctx, n_lat)
    cos_diff = jnp.tile(cos64, (1, 2 * DIFF_HEADS))
    sin_diff = jnp.tile(sin64, (1, 2 * DIFF_HEADS))
    pad = ((0, 0), (0, LANES - MLA_ROPE_DIM))
    cos_mla = jnp.pad(cos64, pad, constant_values=1.0)
    sin_mla = jnp.pad(sin64, pad)

    xs, xs_ctx = x, ctx
    for layer in range(depth):
        last = layer == depth - 1
        j = layer // 2
        t_off = ctx_tiles if last else 0
        mod_g = _group_mod(mod[layer], b, group)
        ng_l = norm_g[layer]
        w_out = mix_w_out[layer].astype(BF16)
        if layer % 2 == 0:
            lam_init = 0.8 - 0.6 * math.exp(-0.3 * layer)
            u, qt, k, vt = _even_in(xs, xs_ctx, mod_g, ng_l, cos_diff, sin_diff, ev_w_in[j].astype(BF16),
                                    ctx_tiles, group)
            att = _diff_attn(qt, k, vt, diff_lambda[j], diff_subln[j], lam_init, n_ctx)
            x_mid, h2 = _even_out(xs, xs_ctx, u, att, mod_g, ng_l, pool_w[j].astype(BF16), pool_scale[j], w_out,
                                  ctx_tiles, t_off, group)
        else:
            assert xs_ctx is None
            (qt, kc, vt, hq, hk_f, hk_b, lf_f, lf_b, hv, hgate) = _odd_in(
                xs, _group_mod(mod[layer], b, group_odd_in), ng_l, cos_mla, sin_mla, od_w_in[j].astype(BF16),
                mla_q_norm[j], mla_w_uq[j].astype(BF16), mla_kv_norm[j], mla_w_ukv[j].astype(BF16), hgrn_lb,
                layer, ctx_tiles, group_odd_in)
            att = _mla_attn(qt, kc, vt, n_ctx, t_off * (TILE // MLA_TQ))
            o_f = _hgrn_scan(False, lf_f, hq, hk_f, hv, n_ctx // HG_BLOCK)
            hg = _hgrn_scan(True, lf_b, hq, hk_b, hv, n_ctx // HG_BLOCK, (o_f, hgate, hgrn_norm[j]))
            x_mid, h2 = _odd_out(xs, att, hg, mod_g, ng_l, w_out, ctx_tiles, t_off, group)
        xs = _ffn(x_mid, h2, mod_g, ng_l, ffn_w_gate[layer].astype(BF16), ffn_w_up[layer].astype(BF16),
                  ffn_conv_w[layer], ffn_conv_b[layer], ffn_w_down[layer].astype(BF16),
                  0 if last else ctx_tiles, group)
        xs_ctx = None
    return xs
```

```python
import functools
import math

import jax
import jax.numpy as jnp
import numpy as np
from jax import lax
from jax.experimental import pallas as pl
from jax.experimental.pallas import tpu as pltpu

F32 = jnp.float32
BF16 = jnp.bfloat16

D_MODEL = 1024
GRID_W = 64
RMS_EPS = 1e-6
ROPE_THETA = 10000.0
LOG2E = math.log2(math.e)

POOL_WINDOWS = (2, 4, 8, 16)
POOL_GROUPS = 4
POOL_WIDTH = 512
POOL_GROUP_DIM = POOL_WIDTH // POOL_GROUPS

DIFF_HEADS = 4
DIFF_QK_DIM = 64
DIFF_V_DIM = 128
DIFF_QK_WIDTH = 2 * DIFF_HEADS * DIFF_QK_DIM
DIFF_WIDTH = DIFF_HEADS * DIFF_V_DIM

MLA_HEADS = 4
MLA_Q_RANK = 512
MLA_KV_RANK = 256
MLA_NOPE_DIM = 128
MLA_ROPE_DIM = 64
MLA_V_DIM = 128
MLA_QK_PAD = 256
MLA_WIDTH = MLA_HEADS * MLA_V_DIM

HG_HEADS = 4
HG_K_DIM = 128
HG_V_DIM = 128
HG_QK_WIDTH = HG_HEADS * HG_K_DIM
HG_WIDTH = HG_HEADS * HG_V_DIM
HG_CHUNK = 64
HG_LEVELS = (64, 32, 16, 8, 4, 2)

D_FF = 2816
FF_CHUNK = 256

LANES = 128
SUBLANES = 8
BF16_SUBLANES = 16
MXU_TILE = 256

TILE = 256
MAX_GROUP = 4
ODD_IN_GROUP = 4
DIFF_TQ = 128
DIFF_STEP_Q = 256
MLA_TQ = 256
MLA_STEP_TILES = 2
HG_BLOCK = 256
HG_GROUP = 2
HALO = BF16_SUBLANES
VMEM_LIMIT = 56 * 1024 * 1024
SOFTMAX_DENOM_FLOOR = 2.0 ** -100
SCORE_BOUND_SLACK = 1.0 + 2.0 ** -6


def _cparams(n_axes):
    return pltpu.CompilerParams(
        dimension_semantics=("arbitrary",) * n_axes, vmem_limit_bytes=VMEM_LIMIT)


def _dot(a, b):
    return jnp.dot(a, b, preferred_element_type=F32)


def _dot_nt(a, b):
    return lax.dot_general(a, b, (((1,), (1,)), ((), ())), preferred_element_type=F32)


def _rms(x, g):
    return x * lax.rsqrt(jnp.mean(x * x, axis=-1, keepdims=True) + RMS_EPS) * g


def _silu(x):
    return x * jax.nn.sigmoid(x)


def _const_spec(shape):
    zeros = (0,) * len(shape)
    return pl.BlockSpec(shape, lambda *_: zeros, pipeline_mode=pl.Buffered(1))


def _group_size(b, max_group):
    return max(g for g in range(1, max_group + 1) if b % g == 0)


def _group_mod(mod_l, b, group):
    return jnp.concatenate([mod_l[:b], jnp.broadcast_to(mod_l[b:b + 1], (group,) + mod_l.shape[1:])], axis=0)


def _rope(x, cos, sin_signed):
    n = x.shape[-1]
    lane = lax.broadcasted_iota(jnp.int32, x.shape, 1)
    first_half = (lane & 31) < 16
    partner = jnp.where(first_half, pltpu.roll(x, n - 16, 1), pltpu.roll(x, 16, 1))
    return x * cos + partner * sin_signed


def _prenorm(g, x_ref, mod_ref, ng_ref):
    h = _rms(x_ref[g], ng_ref[0:1, :]) * (1.0 + mod_ref[g, 1:2, :]) + mod_ref[g, 0:1, :]
    return h.astype(BF16)


def _mod_kernel(c_ref, w_ref, b_ref, o_ref):
    s = _silu(c_ref[...])
    w = w_ref[0]
    s_hi = s.astype(BF16)
    s_lo = (s - s_hi.astype(F32)).astype(BF16)
    w_hi = w.astype(BF16)
    w_lo = (w - w_hi.astype(F32)).astype(BF16)
    o_ref[0] = _dot(s_hi, w_hi) + _dot(s_hi, w_lo) + _dot(s_lo, w_hi) + b_ref[0]


def _modulation(cond, ada_w, ada_b):
    depth, d, n = ada_w.shape
    rows = cond.shape[0]
    tn = 1536
    return pl.pallas_call(
        _mod_kernel,
        grid=(depth, n // tn),
        in_specs=[
            pl.BlockSpec((rows, d), lambda l, j: (0, 0)),
            pl.BlockSpec((1, d, tn), lambda l, j: (l, 0, j)),
            pl.BlockSpec((1, 1, tn), lambda l, j: (l, 0, j)),
        ],
        out_specs=pl.BlockSpec((1, rows, tn), lambda l, j: (l, 0, j)),
        out_shape=jax.ShapeDtypeStruct((depth, rows, n), F32),
        compiler_params=_cparams(2),
    )(cond, ada_w, ada_b.reshape(depth, 1, n))


class _SplitRows:
    def __init__(self, ctx_ref, lat_ref, is_ctx):
        self.ctx_ref, self.lat_ref, self.is_ctx = ctx_ref, lat_ref, is_ctx
        self.shape = lat_ref.shape

    def __getitem__(self, g):
        return jnp.where(self.is_ctx, self.ctx_ref[g], self.lat_ref[g])


def _token_rows(split_ctx_tiles, t_off, refs):
    if split_ctx_tiles is None:
        return refs[0], refs[1:]
    return _SplitRows(refs[0], refs[1], pl.program_id(1) + t_off < split_ctx_tiles), refs[2:]


def _token_specs(x, x_ctx, group, ctx_tiles, t_off):
    d = x.shape[2]
    if x_ctx is None:
        return [x], [pl.BlockSpec((group, TILE, d), lambda i, j: (i, j + t_off, 0))]
    return [x_ctx, x], [
        pl.BlockSpec((group, TILE, d), lambda i, j: (i, jnp.minimum(j + t_off, ctx_tiles - 1), 0)),
        pl.BlockSpec((group, TILE, d), lambda i, j: (i, jnp.maximum(j + t_off - ctx_tiles, 0), 0)),
    ]


def _even_in_kernel(split_ctx_tiles, *refs):
    x_ref, (mod_ref, ng_ref, cos_ref, sin_ref, w_ref, u_ref, q_ref, k_ref, vt_ref) = _token_rows(
        split_ctx_tiles, 0, refs)
    cos = cos_ref[...]
    sin = sin_ref[...]
    o_q = POOL_WIDTH
    o_k = o_q + DIFF_QK_WIDTH
    o_v = o_k + DIFF_QK_WIDTH
    for g in range(x_ref.shape[0]):
        p = _dot(_prenorm(g, x_ref, mod_ref, ng_ref), w_ref[...])
        u_ref[g] = p[:, :o_q].astype(BF16)
        q_ref[g] = (_rope(p[:, o_q:o_k], cos, sin) * (DIFF_QK_DIM ** -0.5 * LOG2E)).astype(BF16)
        k_ref[g] = _rope(p[:, o_k:o_v], cos, sin).astype(BF16)
        vt_ref[g] = p[:, o_v:].T.astype(BF16)


def _even_in(x, x_ctx, mod_g, ng_l, cos, sin, w_in, ctx_tiles, group):
    b, t, d = x.shape
    if x_ctx is not None:
        t += x_ctx.shape[1]
    nt = t // TILE
    n_in = w_in.shape[1]

    def mod_map(i, j):
        return (jnp.where(j < ctx_tiles, b // group, i), 0, 0)

    tok = lambda w: pl.BlockSpec((group, TILE, w), lambda i, j: (i, j, 0))
    tok_t = lambda w: pl.BlockSpec((group, w, TILE), lambda i, j: (i, 0, j))
    tokens, token_specs = _token_specs(x, x_ctx, group, ctx_tiles, 0)
    return pl.pallas_call(
        functools.partial(_even_in_kernel, None if x_ctx is None else ctx_tiles),
        grid=(b // group, nt),
        in_specs=token_specs + [
            pl.BlockSpec((group, 6, d), mod_map),
            _const_spec((4, d)),
            pl.BlockSpec((TILE, DIFF_QK_WIDTH), lambda i, j: (j, 0)),
            pl.BlockSpec((TILE, DIFF_QK_WIDTH), lambda i, j: (j, 0)),
            _const_spec((d, n_in)),
        ],
        out_specs=[tok(POOL_WIDTH), tok(DIFF_QK_WIDTH), tok(DIFF_QK_WIDTH), tok_t(DIFF_WIDTH)],
        out_shape=[
            jax.ShapeDtypeStruct((b, t, POOL_WIDTH), BF16),
            jax.ShapeDtypeStruct((b, t, DIFF_QK_WIDTH), BF16),
            jax.ShapeDtypeStruct((b, t, DIFF_QK_WIDTH), BF16),
            jax.ShapeDtypeStruct((b, DIFF_WIDTH, t), BF16),
        ],
        compiler_params=_cparams(2),
    )(*tokens, mod_g, ng_l, cos, sin, w_in)


def _key_halves(nk, align):
    if nk <= MXU_TILE:
        return ((0, nk),)
    first = -(-(nk // 2) // align) * align
    return ((0, first), (first, nk))


def _attend_t(n_heads, nk, scores_operands, values_t, key_sq, s_ref, e_ref, finish):
    halves = _key_halves(nk, LANES)
    value_halves = _key_halves(nk, MXU_TILE)

    def weighted_values(h):
        vt = values_t(h)
        return sum(_dot(vt[:, a:b], e_ref[h % 2, a:b, :]) for a, b in value_halves)

    l_low = None
    for h in range(n_heads):
        keys, q = scores_operands(h)
        q_sq = (q.astype(F32) * q.astype(F32)).astype(BF16)
        q_sq = _dot_nt(jnp.ones((SUBLANES, q.shape[1]), BF16), q_sq)[0:1]
        bound = jnp.sqrt(q_sq * key_sq(h)) * SCORE_BOUND_SLACK
        l = None
        for a, b in halves:
            e = jnp.exp2(_dot_nt(keys[a:b, :], q) - bound)
            e_ref[h % 2, a:b, :] = e.astype(BF16)
            part = jnp.sum(e, axis=0, keepdims=True)
            l = part if l is None else l + part
        finish(h, weighted_values(h), l)
        l_low = l if l_low is None else jnp.minimum(l_low, l)
    accurate = jnp.min(l_low) > SOFTMAX_DENOM_FLOOR

    @pl.when(jnp.logical_not(accurate))
    def _():
        def scores(h):
            keys, q = scores_operands(h)
            tops = []
            for a, b in halves:
                s = _dot_nt(keys[a:b, :], q)
                s_ref[h % 2, a:b, :] = s
                tops.append(jnp.max(s, axis=0, keepdims=True))
            return functools.reduce(jnp.maximum, tops)

        top = scores(0)
        for h in range(n_heads):
            nxt = scores(h + 1) if h + 1 < n_heads else None
            e = jnp.exp2(s_ref[h % 2, :nk, :] - top)
            e_ref[h % 2, :nk, :] = e.astype(BF16)
            finish(h, weighted_values(h), jnp.sum(e, axis=0, keepdims=True))
            top = nxt


def _store_key_sq(ksq_ref, row, cols, keys_sq, dim_weights, n_ctx):
    n = cols.stop - cols.start
    for r, rows in ((row, slice(0, n_ctx)), (row + ksq_ref.shape[0] // 2, slice(None))):
        dim_max = jnp.max(keys_sq[rows], axis=0, keepdims=True)
        ksq_ref[r:r + 1, cols] = jnp.broadcast_to(jnp.sum(dim_max * dim_weights, axis=1, keepdims=True), (1, n))


def _diff_attn_kernel(lam_init, n_ctx, q_ref, k_ref, vt_ref, lam_ref, g_ref, o_ref, s_ref, e_ref, ksq_ref):
    lv = lam_ref[...]
    lam = (jnp.exp(jnp.sum(lv[0:1] * lv[1:2], axis=-1, keepdims=True))
           - jnp.exp(jnp.sum(lv[2:3] * lv[3:4], axis=-1, keepdims=True)) + lam_init)
    gain = g_ref[...] * (1.0 - lam_init)
    lane = lax.broadcasted_iota(jnp.int32, (DIFF_TQ, LANES), 1)

    @pl.when(pl.program_id(1) == 0)
    def _():
        first_head = (lax.broadcasted_iota(jnp.int32, (1, LANES), 1) < DIFF_QK_DIM).astype(F32)
        for h in range(DIFF_HEADS):
            kf = k_ref[0, :, h * LANES:(h + 1) * LANES].astype(F32)
            _store_key_sq(ksq_ref, h, slice(0, DIFF_TQ), kf * kf, first_head, n_ctx)
            _store_key_sq(ksq_ref, h, slice(DIFF_TQ, 2 * DIFF_TQ), kf * kf, 1.0 - first_head, n_ctx)

    q_tiles = DIFF_STEP_Q // DIFF_TQ

    def attend(nk, ctx_only):
        ksq_row = 0 if ctx_only else DIFF_HEADS

        def scores_operands(p):
            h, qi = divmod(p, q_tiles)
            cols = slice(h * LANES, (h + 1) * LANES)
            qp = q_ref[0, qi * DIFF_TQ:(qi + 1) * DIFF_TQ, cols]
            q_bd = jnp.concatenate(
                [jnp.where(lane < DIFF_QK_DIM, qp, 0), jnp.where(lane >= DIFF_QK_DIM, qp, 0)], axis=0)
            return k_ref.at[0, :nk, cols], q_bd

        def finish(p, o_t, l):
            h, qi = divmod(p, q_tiles)
            r = 1.0 / l
            o = o_t[:, :DIFF_TQ] * r[:, :DIFF_TQ] - o_t[:, DIFF_TQ:] * (lam * r[:, DIFF_TQ:])
            o = o * lax.rsqrt(jnp.mean(o * o, axis=0, keepdims=True) + RMS_EPS) * gain
            o_ref[0, qi * DIFF_TQ:(qi + 1) * DIFF_TQ, h * LANES:(h + 1) * LANES] = o.T.astype(BF16)

        def values_t(p):
            h = p // q_tiles
            return vt_ref.at[0, h * LANES:(h + 1) * LANES, :nk]

        def key_sq(p):
            r = ksq_row + p // q_tiles
            return ksq_ref[r:r + 1, :]

        _attend_t(DIFF_HEADS * q_tiles, nk, scores_operands, values_t, key_sq, s_ref, e_ref, finish)

    is_ctx = pl.program_id(1) < n_ctx // DIFF_STEP_Q

    @pl.when(is_ctx)
    def _():
        attend(n_ctx, True)

    @pl.when(jnp.logical_not(is_ctx))
    def _():
        attend(k_ref.shape[1], False)


def _diff_attn(q, k, vt, lam_vec, subln, lam_init, n_ctx):
    b, t, _ = k.shape
    per_sample = lambda shape: pl.BlockSpec(shape, lambda i, j: (i, 0, 0))
    return pl.pallas_call(
        functools.partial(_diff_attn_kernel, lam_init, n_ctx),
        grid=(b, t // DIFF_STEP_Q),
        in_specs=[
            pl.BlockSpec((1, DIFF_STEP_Q, DIFF_QK_WIDTH), lambda i, j: (i, j, 0)),
            per_sample((1, t, DIFF_QK_WIDTH)),
            per_sample((1, DIFF_WIDTH, t)),
            _const_spec((4, DIFF_QK_DIM)),
            _const_spec((DIFF_V_DIM, 1)),
        ],
        out_specs=pl.BlockSpec((1, DIFF_STEP_Q, DIFF_WIDTH), lambda i, j: (i, j, 0)),
        out_shape=jax.ShapeDtypeStruct((b, t, DIFF_WIDTH), BF16),
        scratch_shapes=[pltpu.VMEM((2, t, 2 * DIFF_TQ), F32), pltpu.VMEM((2, t, 2 * DIFF_TQ), BF16),
                        pltpu.VMEM((2 * DIFF_HEADS, 2 * DIFF_TQ), F32)],
        compiler_params=_cparams(2),
    )(q, k, vt, lam_vec, subln.reshape(DIFF_V_DIM, 1))


def _residual_and_prenorm(g, x_ref, y, mod_ref, ng_ref, xo_ref, h2_ref):
    xn = x_ref[g] + mod_ref[g, 2:3, :] * _rms(y, ng_ref[1:2, :])
    xo_ref[g] = xn
    h2 = _rms(xn, ng_ref[2:3, :]) * (1.0 + mod_ref[g, 4:5, :]) + mod_ref[g, 3:4, :]
    h2_ref[g] = h2.astype(BF16)


def _even_out_kernel(split_ctx_tiles, t_off, seg_tiles, n_tiles, *refs):
    x_ref, (u_ref, up_ref, un_ref, a_ref, mod_ref, ng_ref, pw_ref, ps_ref, wo_ref, xo_ref, h2_ref,
            ext_ref) = _token_rows(split_ctx_tiles, t_off, refs)
    j = pl.program_id(1) + t_off
    prev_ok = jnp.logical_and(j != 0, j != seg_tiles)
    next_ok = jnp.logical_and(j != n_tiles - 1, j != seg_tiles - 1)
    lo = jnp.where(prev_ok, -HALO, 0)
    hi = jnp.where(next_ok, TILE + HALO, TILE)
    row = lax.broadcasted_iota(jnp.int32, (TILE, 1), 0)
    for g in range(x_ref.shape[0]):
        ext_ref[g, 0:HALO, :] = jnp.where(prev_ok, up_ref[g].astype(F32), 0.0)
        ext_ref[g, HALO:HALO + TILE, :] = u_ref[g].astype(F32)
        ext_ref[g, HALO + TILE:, :] = jnp.where(next_ok, un_ref[g].astype(F32), 0.0)
        parts = []
        for gidx, win in enumerate(POOL_WINDOWS):
            half = win // 2
            cols = slice(gidx * POOL_GROUP_DIM, (gidx + 1) * POOL_GROUP_DIM)
            acc = ext_ref[g, HALO - half:HALO - half + TILE, cols]
            for off in range(-half + 1, half):
                acc = acc + ext_ref[g, HALO + off:HALO + off + TILE, cols]
            cnt = jnp.minimum(row + half, hi) - jnp.maximum(row - half, lo)
            dlt = acc / cnt.astype(F32) - ext_ref[g, HALO:HALO + TILE, cols]
            parts.append(_dot(dlt.astype(BF16), pw_ref[gidx]))
        yp = jnp.concatenate(parts, axis=1) * ps_ref[...]
        y = _dot(yp.astype(BF16), wo_ref[:POOL_WIDTH, :]) + _dot(a_ref[g], wo_ref[POOL_WIDTH:, :])
        _residual_and_prenorm(g, x_ref, y, mod_ref, ng_ref, xo_ref, h2_ref)


def _even_out(x, x_ctx, u, att, mod_g, ng_l, pool_w, pool_scale, w_out, ctx_tiles, t_off, group):
    b, t, d = x.shape
    if x_ctx is not None:
        t += x_ctx.shape[1]
    nt = t // TILE
    nh = TILE // HALO
    n_out = nt - t_off

    def mod_map(i, j):
        return (jnp.where(j + t_off < ctx_tiles, b // group, i), 0, 0)

    tok = lambda w: pl.BlockSpec((group, TILE, w), lambda i, j: (i, j + t_off, 0))
    out_tok = lambda: pl.BlockSpec((group, TILE, d), lambda i, j: (i, j, 0))
    tokens, token_specs = _token_specs(x, x_ctx, group, ctx_tiles, t_off)
    return pl.pallas_call(
        functools.partial(_even_out_kernel, None if x_ctx is None else ctx_tiles, t_off, ctx_tiles, nt),
        grid=(b // group, n_out),
        in_specs=token_specs + [
            tok(POOL_WIDTH),
            pl.BlockSpec((group, HALO, POOL_WIDTH), lambda i, j: (i, jnp.maximum((j + t_off) * nh - 1, 0), 0)),
            pl.BlockSpec((group, HALO, POOL_WIDTH),
                         lambda i, j: (i, jnp.minimum((j + t_off + 1) * nh, nt * nh - 1), 0)),
            tok(DIFF_WIDTH),
            pl.BlockSpec((group, 6, d), mod_map),
            _const_spec((4, d)),
            _const_spec((POOL_GROUPS, POOL_GROUP_DIM, POOL_GROUP_DIM)),
            _const_spec((1, POOL_WIDTH)),
            _const_spec((POOL_WIDTH + DIFF_WIDTH, d)),
        ],
        out_specs=[out_tok(), out_tok()],
        out_shape=[
            jax.ShapeDtypeStruct((b, n_out * TILE, d), F32),
            jax.ShapeDtypeStruct((b, n_out * TILE, d), BF16),
        ],
        scratch_shapes=[pltpu.VMEM((group, TILE + 2 * HALO, POOL_WIDTH), F32)],
        compiler_params=_cparams(2),
    )(*tokens, u, u, u, att, mod_g, ng_l, pool_w, pool_scale.reshape(1, POOL_WIDTH), w_out)


def _ffn_kernel(seg_tiles, n_tiles, x_ref, h_ref, hp_ref, hn_ref, mod_ref, ng_ref, wg_ref, wu_ref,
                cw_ref, cb_ref, wd_ref, o_ref, a_ref, act_ref):
    j = pl.program_id(1)
    prev_ok = jnp.logical_and(j != 0, j != seg_tiles)
    next_ok = jnp.logical_and(j != n_tiles - 1, j != seg_tiles - 1)
    group = x_ref.shape[0]
    ext = TILE + 2 * HALO
    zero = jnp.zeros((HALO, h_ref.shape[2]), BF16)
    pieces = []
    for g in range(group):
        pieces += [jnp.where(prev_ok, hp_ref[g], zero), h_ref[g], jnp.where(next_ok, hn_ref[g], zero)]
    h_ext = jnp.concatenate(pieces, axis=0)
    h = jnp.concatenate([h_ref[g] for g in range(group)], axis=0) if group > 1 else h_ref[0]
    for c in range(D_FF // FF_CHUNK):
        cols = slice(c * FF_CHUNK, (c + 1) * FF_CHUNK)
        a_ref[...] = _dot(h_ext, wg_ref[:, cols])
        up = _dot(h, wu_ref[:, cols])
        for g in range(group):
            base = g * ext + HALO
            a = (a_ref[base - 1:base - 1 + TILE, :] * cw_ref[0:1, cols]
                 + a_ref[base:base + TILE, :] * cw_ref[1:2, cols]
                 + a_ref[base + 1:base + 1 + TILE, :] * cw_ref[2:3, cols] + cb_ref[:, cols])
            act_ref[g * TILE:(g + 1) * TILE, cols] = (_silu(a) * up[g * TILE:(g + 1) * TILE]).astype(BF16)
    for g in range(group):
        f = _dot(act_ref[g * TILE:(g + 1) * TILE, :], wd_ref[...])
        o_ref[g] = x_ref[g] + mod_ref[g, 5:6, :] * _rms(f, ng_ref[3:4, :])


def _ffn(x, h2, mod_g, ng_l, w_gate, w_up, conv_w, conv_b, w_down, ctx_tiles, group):
    b, t, d = x.shape
    nt = t // TILE
    nh = TILE // HALO

    def mod_map(i, j):
        return (jnp.where(j < ctx_tiles, b // group, i), 0, 0)

    tok = lambda: pl.BlockSpec((group, TILE, d), lambda i, j: (i, j, 0))
    return pl.pallas_call(
        functools.partial(_ffn_kernel, ctx_tiles, nt),
        grid=(b // group, nt),
        in_specs=[
            tok(),
            tok(),
            pl.BlockSpec((group, HALO, d), lambda i, j: (i, jnp.maximum(j * nh - 1, 0), 0)),
            pl.BlockSpec((group, HALO, d), lambda i, j: (i, jnp.minimum((j + 1) * nh, nt * nh - 1), 0)),
            pl.BlockSpec((group, 6, d), mod_map),
            _const_spec((4, d)),
            _const_spec((d, D_FF)),
            _const_spec((d, D_FF)),
            _const_spec((3, D_FF)),
            _const_spec((1, D_FF)),
            _const_spec((D_FF, d)),
        ],
        out_specs=tok(),
        out_shape=jax.ShapeDtypeStruct((b, t, d), F32),
        scratch_shapes=[pltpu.VMEM((group * (TILE + 2 * HALO), FF_CHUNK), F32),
                        pltpu.VMEM((group * TILE, D_FF), BF16)],
        compiler_params=_cparams(2),
    )(x, h2, h2, h2, mod_g, ng_l, w_gate, w_up, conv_w, conv_b.reshape(1, D_FF), w_down)


def _odd_in_kernel(layer, x_ref, mod_ref, ng_ref, cos_ref, sin_ref, wq_ref, wkv_ref, wkr_ref, wh_ref,
                   qn_ref, wuq_ref, kvn_ref, wukv_ref, lb_ref,
                   q_ref, kc_ref, vt_ref, hq_ref, hkf_ref, hkb_ref, lff_ref, lfb_ref, hv_ref, hg_ref):
    cos = cos_ref[...]
    sin = sin_ref[...]
    w = HG_QK_WIDTH
    lbs = []
    for direction in range(2):
        lrows = [lb_ref[direction, i:i + 1, :] for i in range(lb_ref.shape[1])]
        top = functools.reduce(jnp.maximum, lrows)
        ex = [jnp.exp(r - top) for r in lrows]
        lbs.append(sum(ex[1:layer + 1], jnp.zeros_like(top)) / sum(ex))

    for g in range(x_ref.shape[0]):
        hb = _prenorm(g, x_ref, mod_ref, ng_ref)
        cq = _rms(_dot(hb, wq_ref[...]), qn_ref[...])
        q = _dot(cq.astype(BF16), wuq_ref[...]) * ((MLA_NOPE_DIM + MLA_ROPE_DIM) ** -0.5 * LOG2E)
        ckv = _rms(_dot(hb, wkv_ref[...]), kvn_ref[...])
        kv = _dot(ckv.astype(BF16), wukv_ref[...])
        kr = _rope(_dot(hb, wkr_ref[...]), cos, sin).astype(BF16)
        for hd in range(MLA_HEADS):
            qb = hd * MLA_QK_PAD
            q_ref[g, :, qb:qb + MLA_NOPE_DIM] = q[:, qb:qb + MLA_NOPE_DIM].astype(BF16)
            q_ref[g, :, qb + MLA_NOPE_DIM:qb + MLA_QK_PAD] = _rope(
                q[:, qb + MLA_NOPE_DIM:qb + MLA_QK_PAD], cos, sin).astype(BF16)
            kb = hd * (MLA_NOPE_DIM + MLA_V_DIM)
            kc_ref[g, hd, :, :MLA_NOPE_DIM] = kv[:, kb:kb + MLA_NOPE_DIM].astype(BF16)
            kc_ref[g, hd, :, MLA_NOPE_DIM:] = kr
            vt_ref[g, hd * MLA_V_DIM:(hd + 1) * MLA_V_DIM, :] = kv[
                :, kb + MLA_NOPE_DIM:kb + MLA_NOPE_DIM + MLA_V_DIM].T.astype(BF16)

        hq_ref[g] = _silu(_dot(hb, wh_ref[:, :w])).astype(BF16)
        hv_ref[g] = _dot(hb, wh_ref[:, 3 * w:4 * w]).astype(BF16)
        hg_ref[g] = _silu(_dot(hb, wh_ref[:, 4 * w:])).astype(BF16)
        for direction, (k_ref, lf_ref) in enumerate(((hkf_ref, lff_ref), (hkb_ref, lfb_ref))):
            lb = lbs[direction]
            f = lb + (1.0 - lb) * jax.nn.sigmoid(_dot(hb, wh_ref[:, (1 + direction) * w:(2 + direction) * w]))
            k_ref[g] = (1.0 - f).astype(BF16)
            lf_ref[g] = jnp.log2(f)


def _odd_in(x, mod_g, ng_l, cos, sin, w_in, q_norm, w_uq, kv_norm, w_ukv, hgrn_lb, layer, ctx_tiles, group):
    b, t, d = x.shape
    nt = t // TILE
    o1 = MLA_Q_RANK
    o2 = o1 + MLA_KV_RANK
    o3 = o2 + MLA_ROPE_DIM
    w_q = w_in[:, :o1]
    w_kv = w_in[:, o1:o2]
    w_kr = jnp.pad(w_in[:, o2:o3], ((0, 0), (0, LANES - MLA_ROPE_DIM)))
    w_h = w_in[:, o3:]
    qk = MLA_NOPE_DIM + MLA_ROPE_DIM
    w_uq_pad = jnp.pad(w_uq.reshape(MLA_Q_RANK, MLA_HEADS, qk),
                       ((0, 0), (0, 0), (0, MLA_QK_PAD - qk))).reshape(MLA_Q_RANK, MLA_HEADS * MLA_QK_PAD)

    def mod_map(i, j):
        return (jnp.where(j < ctx_tiles, b // group, i), 0, 0)

    tok = lambda w: pl.BlockSpec((group, TILE, w), lambda i, j: (i, j, 0))
    sds = lambda w, dt: jax.ShapeDtypeStruct((b, t, w), dt)
    hw = HG_QK_WIDTH
    return pl.pallas_call(
        functools.partial(_odd_in_kernel, layer),
        grid=(b // group, nt),
        in_specs=[
            tok(d),
            pl.BlockSpec((group, 6, d), mod_map),
            _const_spec((4, d)),
            pl.BlockSpec((TILE, LANES), lambda i, j: (j, 0)),
            pl.BlockSpec((TILE, LANES), lambda i, j: (j, 0)),
            _const_spec(w_q.shape),
            _const_spec(w_kv.shape),
            _const_spec(w_kr.shape),
            _const_spec(w_h.shape),
            _const_spec((1, MLA_Q_RANK)),
            _const_spec(w_uq_pad.shape),
            _const_spec((1, MLA_KV_RANK)),
            _const_spec(w_ukv.shape),
            _const_spec(hgrn_lb.shape),
        ],
        out_specs=[
            tok(MLA_HEADS * MLA_QK_PAD),
            pl.BlockSpec((group, MLA_HEADS, TILE, MLA_QK_PAD), lambda i, j: (i, 0, j, 0)),
            pl.BlockSpec((group, MLA_WIDTH, TILE), lambda i, j: (i, 0, j)),
            tok(hw), tok(hw), tok(hw), tok(hw), tok(hw), tok(hw), tok(hw),
        ],
        out_shape=[
            sds(MLA_HEADS * MLA_QK_PAD, BF16),
            jax.ShapeDtypeStruct((b, MLA_HEADS, t, MLA_QK_PAD), BF16),
            jax.ShapeDtypeStruct((b, MLA_WIDTH, t), BF16),
            sds(hw, BF16), sds(hw, BF16), sds(hw, BF16), sds(hw, F32), sds(hw, F32), sds(hw, BF16),
            sds(hw, BF16),
        ],
        compiler_params=_cparams(2),
    )(x, mod_g, ng_l, cos, sin, w_q, w_kv, w_kr, w_h, q_norm.reshape(1, -1), w_uq_pad,
      kv_norm.reshape(1, -1), w_ukv, hgrn_lb)


def _mla_attn_kernel(n_ctx, q_off, q_tiles, *refs):
    q_refs = refs[:q_tiles]
    kc_ref, vt_ref, o_ref, s_ref, e_ref, ksq_ref = refs[q_tiles:]

    @pl.when(pl.program_id(1) == 0)
    def _():
        all_dims = jnp.ones((1, MLA_QK_PAD), F32)
        for h in range(MLA_HEADS):
            kf = kc_ref[0, h].astype(F32)
            _store_key_sq(ksq_ref, h, slice(0, MLA_TQ), kf * kf, all_dims, n_ctx)

    def attend(nk, ctx_only):
        ksq_row = 0 if ctx_only else MLA_HEADS

        def finish(p, o_t, l):
            h, qi = divmod(p, q_tiles)
            o_ref[0, qi * MLA_TQ:(qi + 1) * MLA_TQ, h * MLA_V_DIM:(h + 1) * MLA_V_DIM] = (
                o_t * (1.0 / l)).T.astype(BF16)

        def scores_operands(p):
            h, qi = divmod(p, q_tiles)
            return kc_ref.at[0, h, :nk, :], q_refs[qi][0, :, h * MLA_QK_PAD:(h + 1) * MLA_QK_PAD]

        def values_t(p):
            h = p // q_tiles
            return vt_ref.at[0, h * MLA_V_DIM:(h + 1) * MLA_V_DIM, :nk]

        def key_sq(p):
            r = ksq_row + p // q_tiles
            return ksq_ref[r:r + 1, :]

        _attend_t(MLA_HEADS * q_tiles, nk, scores_operands, values_t, key_sq, s_ref, e_ref, finish)

    if q_off * MLA_TQ >= n_ctx:
        attend(kc_ref.shape[2], False)
    else:
        is_ctx = pl.program_id(1) + q_off < n_ctx // MLA_TQ

        @pl.when(is_ctx)
        def _():
            attend(n_ctx, True)

        @pl.when(jnp.logical_not(is_ctx))
        def _():
            attend(kc_ref.shape[2], False)


def _mla_attn(q, kc, vt, n_ctx, q_off):
    b, _, t, _ = kc.shape
    nq = t // MLA_TQ - q_off
    q_tiles = MLA_STEP_TILES if (q_off * MLA_TQ >= n_ctx and nq % MLA_STEP_TILES == 0) else 1

    def q_spec(qi):
        return pl.BlockSpec((1, MLA_TQ, MLA_HEADS * MLA_QK_PAD), lambda i, j: (i, j * q_tiles + qi + q_off, 0))

    return pl.pallas_call(
        functools.partial(_mla_attn_kernel, n_ctx, q_off, q_tiles),
        grid=(b, nq // q_tiles),
        in_specs=[q_spec(qi) for qi in range(q_tiles)] + [
            pl.BlockSpec((1, MLA_HEADS, t, MLA_QK_PAD), lambda i, j: (i, 0, 0, 0)),
            pl.BlockSpec((1, MLA_WIDTH, t), lambda i, j: (i, 0, 0)),
        ],
        out_specs=pl.BlockSpec((1, q_tiles * MLA_TQ, MLA_WIDTH), lambda i, j: (i, j, 0)),
        out_shape=jax.ShapeDtypeStruct((b, nq * MLA_TQ, MLA_WIDTH), BF16),
        scratch_shapes=[pltpu.VMEM((2, t, MLA_TQ), F32), pltpu.VMEM((2, t, MLA_TQ), BF16),
                        pltpu.VMEM((2 * MLA_HEADS, MLA_TQ), F32)],
        compiler_params=_cparams(2),
    )(*([q] * q_tiles), kc, vt)


def _hgrn_triangle(reverse):
    t = np.arange(HG_CHUNK)[:, None]
    j = np.arange(HG_CHUNK)[None, :]
    return ((j >= t) if reverse else (j <= t)).astype(np.float32)


def _hgrn_masks(reverse):
    c = HG_CHUNK
    t = lax.broadcasted_iota(jnp.int32, (c, c), 0)
    s = lax.broadcasted_iota(jnp.int32, (c, c), 1)
    tq = lax.broadcasted_iota(jnp.int32, (c, 1), 0)
    masks = []
    for m in HG_LEVELS:
        half = m // 2
        shift = int(math.log2(m))
        t_hi = (t & (m - 1)) >= half
        s_hi = (s & (m - 1)) >= half
        same = (t >> shift) == (s >> shift)
        if not reverse:
            pair = jnp.where(same, jnp.where(t_hi, jnp.where(s_hi, 0.0, 1.0), 0.0), 0.0)
            is_q = (tq & (m - 1)) >= half
        else:
            pair = jnp.where(same, jnp.where(t_hi, 0.0, jnp.where(s_hi, 1.0, 0.0)), 0.0)
            is_q = (tq & (m - 1)) < half
        masks.append((pair > 0.5, is_q))
    return masks, t == s


def _hgrn_reference_rows(cum, m, reverse):
    c = HG_CHUNK
    half = m // 2
    ref_in_block = half if reverse else half - 1
    if m >= 2 * SUBLANES:
        return jnp.concatenate(
            [jnp.broadcast_to(cum[b0 + ref_in_block:b0 + ref_in_block + 1, :], (m, LANES))
             for b0 in range(0, c, m)], axis=0)
    cum3 = cum.reshape(c // SUBLANES, SUBLANES, LANES)
    r = lax.broadcasted_iota(jnp.int32, cum3.shape, 1)
    out = None
    for b0 in reversed(range(0, SUBLANES, m)):
        pick = jnp.broadcast_to(cum3[:, b0 + ref_in_block:b0 + ref_in_block + 1, :], cum3.shape)
        out = pick if out is None else jnp.where(r < b0 + m, pick, out)
    return out.reshape(c, LANES)


def _hgrn_scan_body(reverse, tri_ref, lf_ref, q_ref, k_ref, v_ref, st_ref, emit):
    masks, eye = _hgrn_masks(reverse)
    tri = tri_ref[...]
    c = HG_CHUNK
    n_chunks = HG_BLOCK // c
    chunk_order = list(range(n_chunks - 1, -1, -1) if reverse else range(n_chunks))
    group = lf_ref.shape[0]
    probs = [(g, ci, h) for ci in chunk_order for g in range(group) for h in range(HG_HEADS)]
    where = lambda g, ci, h: (g, slice(ci * c, (ci + 1) * c), slice(h * HG_K_DIM, (h + 1) * HG_K_DIM))

    @pl.when(pl.program_id(1) == 0)
    def _():
        st_ref[...] = jnp.zeros_like(st_ref)

    cums = []
    for p in probs:
        g, rows, cols = where(*p)
        lf = lf_ref[g, rows, cols]
        hi = lf.astype(BF16)
        r1 = lf - hi.astype(F32)
        mid = r1.astype(BF16)
        lo = (r1 - mid.astype(F32)).astype(BF16)
        c3 = _dot(tri, jnp.concatenate([hi, mid, lo], axis=1))
        cums.append(c3[:, :LANES] + c3[:, LANES:2 * LANES] + c3[:, 2 * LANES:])

    qf, kf, qe, ks, decay, scores = [], [], [], [], [], []
    for p, cum in zip(probs, cums):
        g, rows, cols = where(*p)
        total = cum[0:1] if reverse else cum[c - 1:c]
        q = q_ref[g, rows, cols]
        k = k_ref[g, rows, cols]
        qf.append(q.astype(F32))
        kf.append(k.astype(F32))
        qe.append((qf[-1] * jnp.exp2(cum)).astype(BF16))
        ks.append((kf[-1] * jnp.exp2(total - cum)).astype(BF16))
        decay.append(jnp.exp2(total))
        scores.append(jnp.where(eye, _dot_nt(q, k), 0.0))

    for m, (pair, is_q) in zip(HG_LEVELS, masks):
        for i, cum in enumerate(cums):
            d = cum - _hgrn_reference_rows(cum, m, reverse)
            x = (jnp.where(is_q, qf[i], kf[i]) * jnp.exp2(-jnp.abs(d))).astype(BF16)
            scores[i] = jnp.where(pair, _dot_nt(x, x), scores[i])

    intra, update = [], []
    for i, p in enumerate(probs):
        g, rows, cols = where(*p)
        v = v_ref[g, rows, cols]
        intra.append(_dot(scores[i].astype(BF16), v))
        update.append(_dot(v.astype(F32).T.astype(BF16), ks[i]))

    for gh in range(group * HG_HEADS):
        st = st_ref[gh]
        for n in range(n_chunks):
            i = n * group * HG_HEADS + gh
            emit(*where(*probs[i]), intra[i] + _dot_nt(qe[i], st.astype(BF16)))
            st = st * decay[i] + update[i]
        st_ref[gh] = st


def _hgrn_fwd_kernel(tri_ref, lf_ref, q_ref, k_ref, v_ref, o_ref, st_ref):
    def emit(g, rows, cols, o):
        o_ref[g, rows, cols] = o

    _hgrn_scan_body(False, tri_ref, lf_ref, q_ref, k_ref, v_ref, st_ref, emit)


def _hgrn_bwd_kernel(tri_ref, lf_ref, q_ref, k_ref, v_ref, of_ref, gate_ref, gn_ref, o_ref, st_ref):
    def emit(g, rows, cols, o):
        o_ref[g, rows, cols] = (_rms(o + of_ref[g, rows, cols], gn_ref[...])
                                * gate_ref[g, rows, cols].astype(F32)).astype(BF16)

    _hgrn_scan_body(True, tri_ref, lf_ref, q_ref, k_ref, v_ref, st_ref, emit)


def _hgrn_scan(reverse, lf, q, k, v, ctx_blocks, extra=()):
    b, t, w = lf.shape
    nb = t // HG_BLOCK
    tri = jnp.asarray(_hgrn_triangle(reverse), BF16)
    if reverse:
        blk = lambda j: jnp.where(j < ctx_blocks, ctx_blocks - 1 - j, nb - 1 - (j - ctx_blocks))
    else:
        blk = lambda j: j
    group = _group_size(b, HG_GROUP)
    tok = pl.BlockSpec((group, HG_BLOCK, w), lambda i, j: (i, blk(j), 0))
    in_specs = [_const_spec(tri.shape), tok, tok, tok, tok]
    args = [tri, lf, q, k, v]
    if reverse:
        o_f, gate, gnorm = extra
        in_specs += [tok, tok, _const_spec((1, HG_V_DIM))]
        args += [o_f, gate, gnorm.reshape(1, HG_V_DIM)]
    return pl.pallas_call(
        _hgrn_bwd_kernel if reverse else _hgrn_fwd_kernel,
        grid=(b // group, nb),
        in_specs=in_specs,
        out_specs=tok,
        out_shape=jax.ShapeDtypeStruct((b, t, w), BF16 if reverse else F32),
        scratch_shapes=[pltpu.VMEM((group * HG_HEADS, HG_V_DIM, HG_K_DIM), F32)],
        compiler_params=_cparams(2),
    )(*args)


def _odd_out_kernel(x_ref, a_ref, g_ref, mod_ref, ng_ref, wo_ref, xo_ref, h2_ref):
    for g in range(x_ref.shape[0]):
        y = _dot(a_ref[g], wo_ref[:MLA_WIDTH, :]) + _dot(g_ref[g], wo_ref[MLA_WIDTH:, :])
        _residual_and_prenorm(g, x_ref, y, mod_ref, ng_ref, xo_ref, h2_ref)


def _odd_out(x, att, hg, mod_g, ng_l, w_out, ctx_tiles, t_off, group):
    b, t, d = x.shape
    n_out = t // TILE - t_off

    def mod_map(i, j):
        return (jnp.where(j + t_off < ctx_tiles, b // group, i), 0, 0)

    out_tok = lambda w: pl.BlockSpec((group, TILE, w), lambda i, j: (i, j, 0))
    return pl.pallas_call(
        _odd_out_kernel,
        grid=(b // group, n_out),
        in_specs=[
            pl.BlockSpec((group, TILE, d), lambda i, j: (i, j + t_off, 0)),
            out_tok(MLA_WIDTH),
            pl.BlockSpec((group, TILE, HG_WIDTH), lambda i, j: (i, j + t_off, 0)),
            pl.BlockSpec((group, 6, d), mod_map),
            _const_spec((4, d)),
            _const_spec((MLA_WIDTH + HG_WIDTH, d)),
        ],
        out_specs=[out_tok(d), out_tok(d)],
        out_shape=[
            jax.ShapeDtypeStruct((b, n_out * TILE, d), F32),
            jax.ShapeDtypeStruct((b, n_out * TILE, d), BF16),
        ],
        compiler_params=_cparams(2),
    )(x, att, hg, mod_g, ng_l, w_out)


def _rope_tables(n_ctx, n_lat):
    rows = n_lat // GRID_W
    pos = jnp.stack([jnp.repeat(jnp.arange(rows), GRID_W), jnp.tile(jnp.arange(GRID_W), rows)], axis=-1)
    axis_dim = DIFF_QK_DIM // 2
    inv_freq = ROPE_THETA ** (-jnp.arange(0, axis_dim, 2, dtype=F32) / axis_dim)
    ang = pos.astype(F32)[..., None] * inv_freq
    cos = jnp.cos(ang)
    sin = jnp.sin(ang)
    cos64 = jnp.concatenate([cos[:, 0], cos[:, 0], cos[:, 1], cos[:, 1]], axis=-1)
    sin64 = jnp.concatenate([-sin[:, 0], sin[:, 0], -sin[:, 1], sin[:, 1]], axis=-1)
    cos64 = jnp.concatenate([jnp.ones((n_ctx, 64), F32), cos64], axis=0)
    sin64 = jnp.concatenate([jnp.zeros((n_ctx, 64), F32), sin64], axis=0)
    return cos64, sin64


def kernel(x, c, ctx, c_ctx, ada_w, ada_b, norm_g, mix_w_out, ffn_w_gate, ffn_w_up, ffn_conv_w, ffn_conv_b,
           ffn_w_down, ev_w_in, pool_w, pool_scale, diff_lambda, diff_subln, od_w_in, mla_q_norm, mla_w_uq,
           mla_kv_norm, mla_w_ukv, hgrn_norm, hgrn_lb):
    b, n_lat, d = x.shape
    n_ctx = ctx.shape[1]
    depth = ada_w.shape[0]
    assert d == D_MODEL and n_ctx % TILE == 0 and n_lat % TILE == 0 and n_lat % GRID_W == 0
    ctx_tiles = n_ctx // TILE
    group = _group_size(b, MAX_GROUP)
    group_odd_in = _group_size(b, ODD_IN_GROUP)

    rows = -(-(b + 1) // SUBLANES) * SUBLANES
    cond = jnp.concatenate([c, c_ctx[None, :], jnp.zeros((rows - b - 1, d), F32)], axis=0)
    mod = _modulation(cond, ada_w, ada_b).reshape(depth, rows, 6, d)

    cos64, sin64 = _rope_tables(n_ctx, n_lat)
    cos_diff = jnp.tile(cos64, (1, 2 * DIFF_HEADS))
    sin_diff = jnp.tile(sin64, (1, 2 * DIFF_HEADS))
    pad = ((0, 0), (0, LANES - MLA_ROPE_DIM))
    cos_mla = jnp.pad(cos64, pad, constant_values=1.0)
    sin_mla = jnp.pad(sin64, pad)

    xs, xs_ctx = x, ctx
    for layer in range(depth):
        last = layer == depth - 1
        j = layer // 2
        t_off = ctx_tiles if last else 0
        mod_g = _group_mod(mod[layer], b, group)
        ng_l = norm_g[layer]
        w_out = mix_w_out[layer].astype(BF16)
        if layer % 2 == 0:
            lam_init = 0.8 - 0.6 * math.exp(-0.3 * layer)
            u, qt, k, vt = _even_in(xs, xs_ctx, mod_g, ng_l, cos_diff, sin_diff, ev_w_in[j].astype(BF16),
                                    ctx_tiles, group)
            att = _diff_attn(qt, k, vt, diff_lambda[j], diff_subln[j], lam_init, n_ctx)
            x_mid, h2 = _even_out(xs, xs_ctx, u, att, mod_g, ng_l, pool_w[j].astype(BF16), pool_scale[j], w_out,
                                  ctx_tiles, t_off, group)
        else:
            assert xs_ctx is None
            (qt, kc, vt, hq, hk_f, hk_b, lf_f, lf_b, hv, hgate) = _odd_in(
                xs, _group_mod(mod[layer], b, group_odd_in), ng_l, cos_mla, sin_mla, od_w_in[j].astype(BF16),
                mla_q_norm[j], mla_w_uq[j].astype(BF16), mla_kv_norm[j], mla_w_ukv[j].astype(BF16), hgrn_lb,
                layer, ctx_tiles, group_odd_in)
            att = _mla_attn(qt, kc, vt, n_ctx, t_off * (TILE // MLA_TQ))
            o_f = _hgrn_scan(False, lf_f, hq, hk_f, hv, n_ctx // HG_BLOCK)
            hg = _hgrn_scan(True, lf_b, hq, hk_b, hv, n_ctx // HG_BLOCK, (o_f, hgate, hgrn_norm[j]))
            x_mid, h2 = _odd_out(xs, att, hg, mod_g, ng_l, w_out, ctx_tiles, t_off, group)
        xs = _ffn(x_mid, h2, mod_g, ng_l, ffn_w_gate[layer].astype(BF16), ffn_w_up[layer].astype(BF16),
                  ffn_conv_w[layer], ffn_conv_b[layer], ffn_w_down[layer].astype(BF16),
                  0 if last else ctx_tiles, group)
        xs_ctx = None
    return xs
```

```python
import functools
import math

import jax
import jax.numpy as jnp
import numpy as np
from jax import lax
from jax.experimental import pallas as pl
from jax.experimental.pallas import tpu as pltpu

F32 = jnp.float32
BF16 = jnp.bfloat16

D_MODEL = 1024
GRID_W = 64
RMS_EPS = 1e-6
ROPE_THETA = 10000.0
LOG2E = math.log2(math.e)

POOL_WINDOWS = (2, 4, 8, 16)
POOL_GROUPS = 4
POOL_WIDTH = 512
POOL_GROUP_DIM = POOL_WIDTH // POOL_GROUPS

DIFF_HEADS = 4
DIFF_QK_DIM = 64
DIFF_V_DIM = 128
DIFF_QK_WIDTH = 2 * DIFF_HEADS * DIFF_QK_DIM
DIFF_WIDTH = DIFF_HEADS * DIFF_V_DIM

MLA_HEADS = 4
MLA_Q_RANK = 512
MLA_KV_RANK = 256
MLA_NOPE_DIM = 128
MLA_ROPE_DIM = 64
MLA_V_DIM = 128
MLA_QK_PAD = 256
MLA_WIDTH = MLA_HEADS * MLA_V_DIM

HG_HEADS = 4
HG_K_DIM = 128
HG_V_DIM = 128
HG_QK_WIDTH = HG_HEADS * HG_K_DIM
HG_WIDTH = HG_HEADS * HG_V_DIM
HG_CHUNK = 64
HG_LEVELS = (64, 32, 16, 8, 4, 2)

D_FF = 2816
FF_CHUNK = 256

LANES = 128
SUBLANES = 8
BF16_SUBLANES = 16
MXU_TILE = 256

TILE = 256
MAX_GROUP = 4
ODD_IN_GROUP = 4
DIFF_TQ = 128
DIFF_STEP_Q = 256
MLA_TQ = 256
MLA_STEP_TILES = 4
HG_BLOCK = 256
HG_GROUP = 2
HALO = BF16_SUBLANES
VMEM_LIMIT = 56 * 1024 * 1024
SOFTMAX_DENOM_FLOOR = 2.0 ** -100
SCORE_BOUND_SLACK = 1.0 + 2.0 ** -6


def _cparams(n_axes):
    return pltpu.CompilerParams(
        dimension_semantics=("arbitrary",) * n_axes, vmem_limit_bytes=VMEM_LIMIT)


def _dot(a, b):
    return jnp.dot(a, b, preferred_element_type=F32)


def _dot_nt(a, b):
    return lax.dot_general(a, b, (((1,), (1,)), ((), ())), preferred_element_type=F32)


def _rms(x, g):
    return x * lax.rsqrt(jnp.mean(x * x, axis=-1, keepdims=True) + RMS_EPS) * g


def _silu(x):
    return x * jax.nn.sigmoid(x)


def _const_spec(shape):
    zeros = (0,) * len(shape)
    return pl.BlockSpec(shape, lambda *_: zeros, pipeline_mode=pl.Buffered(1))


def _group_size(b, max_group):
    return max(g for g in range(1, max_group + 1) if b % g == 0)


def _group_mod(mod_l, b, group):
    return jnp.concatenate([mod_l[:b], jnp.broadcast_to(mod_l[b:b + 1], (group,) + mod_l.shape[1:])], axis=0)


def _rope(x, cos, sin_signed):
    n = x.shape[-1]
    lane = lax.broadcasted_iota(jnp.int32, x.shape, 1)
    first_half = (lane & 31) < 16
    partner = jnp.where(first_half, pltpu.roll(x, n - 16, 1), pltpu.roll(x, 16, 1))
    return x * cos + partner * sin_signed


def _prenorm(g, x_ref, mod_ref, ng_ref):
    h = _rms(x_ref[g], ng_ref[0:1, :] * (1.0 + mod_ref[g, 1:2, :])) + mod_ref[g, 0:1, :]
    return h.astype(BF16)


def _mod_kernel(c_ref, w_ref, b_ref, o_ref):
    s = _silu(c_ref[...])
    w = w_ref[0]
    s_hi = s.astype(BF16)
    s_lo = (s - s_hi.astype(F32)).astype(BF16)
    w_hi = w.astype(BF16)
    w_lo = (w - w_hi.astype(F32)).astype(BF16)
    o_ref[0] = _dot(s_hi, w_hi) + _dot(s_hi, w_lo) + _dot(s_lo, w_hi) + b_ref[0]


def _modulation(cond, ada_w, ada_b):
    depth, d, n = ada_w.shape
    rows = cond.shape[0]
    tn = 1536
    return pl.pallas_call(
        _mod_kernel,
        grid=(depth, n // tn),
        in_specs=[
            pl.BlockSpec((rows, d), lambda l, j: (0, 0)),
            pl.BlockSpec((1, d, tn), lambda l, j: (l, 0, j)),
            pl.BlockSpec((1, 1, tn), lambda l, j: (l, 0, j)),
        ],
        out_specs=pl.BlockSpec((1, rows, tn), lambda l, j: (l, 0, j)),
        out_shape=jax.ShapeDtypeStruct((depth, rows, n), F32),
        compiler_params=_cparams(2),
    )(cond, ada_w, ada_b.reshape(depth, 1, n))


class _SplitRows:
    def __init__(self, ctx_ref, lat_ref, is_ctx):
        self.ctx_ref, self.lat_ref, self.is_ctx = ctx_ref, lat_ref, is_ctx
        self.shape = lat_ref.shape

    def __getitem__(self, g):
        return jnp.where(self.is_ctx, self.ctx_ref[g], self.lat_ref[g])


def _token_rows(split_ctx_tiles, t_off, refs):
    if split_ctx_tiles is None:
        return refs[0], refs[1:]
    return _SplitRows(refs[0], refs[1], pl.program_id(1) + t_off < split_ctx_tiles), refs[2:]


def _token_specs(x, x_ctx, group, ctx_tiles, t_off):
    d = x.shape[2]
    if x_ctx is None:
        return [x], [pl.BlockSpec((group, TILE, d), lambda i, j: (i, j + t_off, 0))]
    return [x_ctx, x], [
        pl.BlockSpec((group, TILE, d), lambda i, j: (i, jnp.minimum(j + t_off, ctx_tiles - 1), 0)),
        pl.BlockSpec((group, TILE, d), lambda i, j: (i, jnp.maximum(j + t_off - ctx_tiles, 0), 0)),
    ]


def _even_in_kernel(split_ctx_tiles, *refs):
    x_ref, (mod_ref, ng_ref, cos_ref, sin_ref, w_ref, u_ref, q_ref, k_ref, vt_ref) = _token_rows(
        split_ctx_tiles, 0, refs)
    cos = cos_ref[...]
    sin = sin_ref[...]
    o_q = POOL_WIDTH
    o_k = o_q + DIFF_QK_WIDTH
    o_v = o_k + DIFF_QK_WIDTH
    for g in range(x_ref.shape[0]):
        p = _dot(_prenorm(g, x_ref, mod_ref, ng_ref), w_ref[...])
        u_ref[g] = p[:, :o_q].astype(BF16)
        q_ref[g] = (_rope(p[:, o_q:o_k], cos, sin) * (DIFF_QK_DIM ** -0.5 * LOG2E)).astype(BF16)
        k_ref[g] = _rope(p[:, o_k:o_v], cos, sin).astype(BF16)
        vt_ref[g] = p[:, o_v:].T.astype(BF16)


def _even_in(x, x_ctx, mod_g, ng_l, cos, sin, w_in, ctx_tiles, group):
    b, t, d = x.shape
    if x_ctx is not None:
        t += x_ctx.shape[1]
    nt = t // TILE
    n_in = w_in.shape[1]

    def mod_map(i, j):
        return (jnp.where(j < ctx_tiles, b // group, i), 0, 0)

    tok = lambda w: pl.BlockSpec((group, TILE, w), lambda i, j: (i, j, 0))
    tok_t = lambda w: pl.BlockSpec((group, w, TILE), lambda i, j: (i, 0, j))
    tokens, token_specs = _token_specs(x, x_ctx, group, ctx_tiles, 0)
    return pl.pallas_call(
        functools.partial(_even_in_kernel, None if x_ctx is None else ctx_tiles),
        grid=(b // group, nt),
        in_specs=token_specs + [
            pl.BlockSpec((group, 6, d), mod_map),
            _const_spec((4, d)),
            pl.BlockSpec((TILE, DIFF_QK_WIDTH), lambda i, j: (j, 0)),
            pl.BlockSpec((TILE, DIFF_QK_WIDTH), lambda i, j: (j, 0)),
            _const_spec((d, n_in)),
        ],
        out_specs=[tok(POOL_WIDTH), tok(DIFF_QK_WIDTH), tok(DIFF_QK_WIDTH), tok_t(DIFF_WIDTH)],
        out_shape=[
            jax.ShapeDtypeStruct((b, t, POOL_WIDTH), BF16),
            jax.ShapeDtypeStruct((b, t, DIFF_QK_WIDTH), BF16),
            jax.ShapeDtypeStruct((b, t, DIFF_QK_WIDTH), BF16),
            jax.ShapeDtypeStruct((b, DIFF_WIDTH, t), BF16),
        ],
        compiler_params=_cparams(2),
    )(*tokens, mod_g, ng_l, cos, sin, w_in)


def _key_halves(nk, align):
    if nk <= MXU_TILE:
        return ((0, nk),)
    first = -(-(nk // 2) // align) * align
    return ((0, first), (first, nk))


def _attend_t(n_heads, nk, scores_operands, values_t, key_sq, s_ref, e_ref, finish):
    halves = _key_halves(nk, LANES)
    value_halves = _key_halves(nk, MXU_TILE)

    def weighted_values(h):
        vt = values_t(h)
        return sum(_dot(vt[:, a:b], e_ref[h % 2, a:b, :]) for a, b in value_halves)

    l_low = None
    for h in range(n_heads):
        keys, q = scores_operands(h)
        q_sq = (q.astype(F32) * q.astype(F32)).astype(BF16)
        q_sq = _dot_nt(jnp.ones((SUBLANES, q.shape[1]), BF16), q_sq)[0:1]
        bound = jnp.sqrt(q_sq * key_sq(h)) * SCORE_BOUND_SLACK
        l = None
        for a, b in halves:
            e = jnp.exp2(_dot_nt(keys[a:b, :], q) - bound)
            e_ref[h % 2, a:b, :] = e.astype(BF16)
            part = jnp.sum(e, axis=0, keepdims=True)
            l = part if l is None else l + part
        finish(h, weighted_values(h), l)
        l_low = l if l_low is None else jnp.minimum(l_low, l)
    accurate = jnp.min(l_low) > SOFTMAX_DENOM_FLOOR

    @pl.when(jnp.logical_not(accurate))
    def _():
        def scores(h):
            keys, q = scores_operands(h)
            tops = []
            for a, b in halves:
                s = _dot_nt(keys[a:b, :], q)
                s_ref[h % 2, a:b, :] = s
                tops.append(jnp.max(s, axis=0, keepdims=True))
            return functools.reduce(jnp.maximum, tops)

        top = scores(0)
        for h in range(n_heads):
            nxt = scores(h + 1) if h + 1 < n_heads else None
            e = jnp.exp2(s_ref[h % 2, :nk, :] - top)
            e_ref[h % 2, :nk, :] = e.astype(BF16)
            finish(h, weighted_values(h), jnp.sum(e, axis=0, keepdims=True))
            top = nxt


def _store_key_sq(ksq_ref, row, cols, keys_sq, dim_weights, n_ctx):
    n = cols.stop - cols.start
    for r, rows in ((row, slice(0, n_ctx)), (row + ksq_ref.shape[0] // 2, slice(None))):
        dim_max = jnp.max(keys_sq[rows], axis=0, keepdims=True)
        ksq_ref[r:r + 1, cols] = jnp.broadcast_to(jnp.sum(dim_max * dim_weights, axis=1, keepdims=True), (1, n))


def _diff_attn_kernel(lam_init, n_ctx, q_ref, k_ref, vt_ref, lam_ref, g_ref, o_ref, s_ref, e_ref, ksq_ref):
    lv = lam_ref[...]
    lam = (jnp.exp(jnp.sum(lv[0:1] * lv[1:2], axis=-1, keepdims=True))
           - jnp.exp(jnp.sum(lv[2:3] * lv[3:4], axis=-1, keepdims=True)) + lam_init)
    gain = g_ref[...] * (1.0 - lam_init)
    lane = lax.broadcasted_iota(jnp.int32, (DIFF_TQ, LANES), 1)

    @pl.when(pl.program_id(1) == 0)
    def _():
        first_head = (lax.broadcasted_iota(jnp.int32, (1, LANES), 1) < DIFF_QK_DIM).astype(F32)
        for h in range(DIFF_HEADS):
            kf = k_ref[0, :, h * LANES:(h + 1) * LANES].astype(F32)
            _store_key_sq(ksq_ref, h, slice(0, DIFF_TQ), kf * kf, first_head, n_ctx)
            _store_key_sq(ksq_ref, h, slice(DIFF_TQ, 2 * DIFF_TQ), kf * kf, 1.0 - first_head, n_ctx)

    q_tiles = DIFF_STEP_Q // DIFF_TQ

    def attend(nk, ctx_only):
        ksq_row = 0 if ctx_only else DIFF_HEADS

        def scores_operands(p):
            h, qi = divmod(p, q_tiles)
            cols = slice(h * LANES, (h + 1) * LANES)
            qp = q_ref[0, qi * DIFF_TQ:(qi + 1) * DIFF_TQ, cols]
            q_bd = jnp.concatenate(
                [jnp.where(lane < DIFF_QK_DIM, qp, 0), jnp.where(lane >= DIFF_QK_DIM, qp, 0)], axis=0)
            return k_ref.at[0, :nk, cols], q_bd

        def finish(p, o_t, l):
            h, qi = divmod(p, q_tiles)
            r = 1.0 / l
            o = o_t[:, :DIFF_TQ] * r[:, :DIFF_TQ] - o_t[:, DIFF_TQ:] * (lam * r[:, DIFF_TQ:])
            o = o * lax.rsqrt(jnp.mean(o * o, axis=0, keepdims=True) + RMS_EPS) * gain
            o_ref[0, qi * DIFF_TQ:(qi + 1) * DIFF_TQ, h * LANES:(h + 1) * LANES] = o.T.astype(BF16)

        def values_t(p):
            h = p // q_tiles
            return vt_ref.at[0, h * LANES:(h + 1) * LANES, :nk]

        def key_sq(p):
            r = ksq_row + p // q_tiles
            return ksq_ref[r:r + 1, :]

        _attend_t(DIFF_HEADS * q_tiles, nk, scores_operands, values_t, key_sq, s_ref, e_ref, finish)

    is_ctx = pl.program_id(1) < n_ctx // DIFF_STEP_Q

    @pl.when(is_ctx)
    def _():
        attend(n_ctx, True)

    @pl.when(jnp.logical_not(is_ctx))
    def _():
        attend(k_ref.shape[1], False)


def _diff_attn(q, k, vt, lam_vec, subln, lam_init, n_ctx):
    b, t, _ = k.shape
    per_sample = lambda shape: pl.BlockSpec(shape, lambda i, j: (i, 0, 0))
    return pl.pallas_call(
        functools.partial(_diff_attn_kernel, lam_init, n_ctx),
        grid=(b, t // DIFF_STEP_Q),
        in_specs=[
            pl.BlockSpec((1, DIFF_STEP_Q, DIFF_QK_WIDTH), lambda i, j: (i, j, 0)),
            per_sample((1, t, DIFF_QK_WIDTH)),
            per_sample((1, DIFF_WIDTH, t)),
            _const_spec((4, DIFF_QK_DIM)),
            _const_spec((DIFF_V_DIM, 1)),
        ],
        out_specs=pl.BlockSpec((1, DIFF_STEP_Q, DIFF_WIDTH), lambda i, j: (i, j, 0)),
        out_shape=jax.ShapeDtypeStruct((b, t, DIFF_WIDTH), BF16),
        scratch_shapes=[pltpu.VMEM((2, t, 2 * DIFF_TQ), F32), pltpu.VMEM((2, t, 2 * DIFF_TQ), BF16),
                        pltpu.VMEM((2 * DIFF_HEADS, 2 * DIFF_TQ), F32)],
        compiler_params=_cparams(2),
    )(q, k, vt, lam_vec, subln.reshape(DIFF_V_DIM, 1))


def _residual_and_prenorm(g, x_ref, y, mod_ref, ng_ref, xo_ref, h2_ref):
    xn = x_ref[g] + _rms(y, ng_ref[1:2, :] * mod_ref[g, 2:3, :])
    xo_ref[g] = xn
    h2 = _rms(xn, ng_ref[2:3, :] * (1.0 + mod_ref[g, 4:5, :])) + mod_ref[g, 3:4, :]
    h2_ref[g] = h2.astype(BF16)


def _even_out_kernel(split_ctx_tiles, t_off, seg_tiles, n_tiles, *refs):
    x_ref, (u_ref, up_ref, un_ref, a_ref, mod_ref, ng_ref, pw_ref, ps_ref, wo_ref, xo_ref, h2_ref,
            ext_ref) = _token_rows(split_ctx_tiles, t_off, refs)
    j = pl.program_id(1) + t_off
    prev_ok = jnp.logical_and(j != 0, j != seg_tiles)
    next_ok = jnp.logical_and(j != n_tiles - 1, j != seg_tiles - 1)
    lo = jnp.where(prev_ok, -HALO, 0)
    hi = jnp.where(next_ok, TILE + HALO, TILE)
    row = lax.broadcasted_iota(jnp.int32, (TILE, 1), 0)
    for g in range(x_ref.shape[0]):
        ext_ref[g, 0:HALO, :] = jnp.where(prev_ok, up_ref[g].astype(F32), 0.0)
        ext_ref[g, HALO:HALO + TILE, :] = u_ref[g].astype(F32)
        ext_ref[g, HALO + TILE:, :] = jnp.where(next_ok, un_ref[g].astype(F32), 0.0)
        parts = []
        for gidx, win in enumerate(POOL_WINDOWS):
            half = win // 2
            cols = slice(gidx * POOL_GROUP_DIM, (gidx + 1) * POOL_GROUP_DIM)
            acc = ext_ref[g, HALO - half:HALO - half + TILE, cols]
            for off in range(-half + 1, half):
                acc = acc + ext_ref[g, HALO + off:HALO + off + TILE, cols]
            cnt = jnp.minimum(row + half, hi) - jnp.maximum(row - half, lo)
            dlt = acc / cnt.astype(F32) - ext_ref[g, HALO:HALO + TILE, cols]
            parts.append(_dot(dlt.astype(BF16), pw_ref[gidx]))
        yp = jnp.concatenate(parts, axis=1) * ps_ref[...]
        y = _dot(yp.astype(BF16), wo_ref[:POOL_WIDTH, :]) + _dot(a_ref[g], wo_ref[POOL_WIDTH:, :])
        _residual_and_prenorm(g, x_ref, y, mod_ref, ng_ref, xo_ref, h2_ref)


def _even_out(x, x_ctx, u, att, mod_g, ng_l, pool_w, pool_scale, w_out, ctx_tiles, t_off, group):
    b, t, d = x.shape
    if x_ctx is not None:
        t += x_ctx.shape[1]
    nt = t // TILE
    nh = TILE // HALO
    n_out = nt - t_off

    def mod_map(i, j):
        return (jnp.where(j + t_off < ctx_tiles, b // group, i), 0, 0)

    tok = lambda w: pl.BlockSpec((group, TILE, w), lambda i, j: (i, j + t_off, 0))
    out_tok = lambda: pl.BlockSpec((group, TILE, d), lambda i, j: (i, j, 0))
    tokens, token_specs = _token_specs(x, x_ctx, group, ctx_tiles, t_off)
    return pl.pallas_call(
        functools.partial(_even_out_kernel, None if x_ctx is None else ctx_tiles, t_off, ctx_tiles, nt),
        grid=(b // group, n_out),
        in_specs=token_specs + [
            tok(POOL_WIDTH),
            pl.BlockSpec((group, HALO, POOL_WIDTH), lambda i, j: (i, jnp.maximum((j + t_off) * nh - 1, 0), 0)),
            pl.BlockSpec((group, HALO, POOL_WIDTH),
                         lambda i, j: (i, jnp.minimum((j + t_off + 1) * nh, nt * nh - 1), 0)),
            tok(DIFF_WIDTH),
            pl.BlockSpec((group, 6, d), mod_map),
            _const_spec((4, d)),
            _const_spec((POOL_GROUPS, POOL_GROUP_DIM, POOL_GROUP_DIM)),
            _const_spec((1, POOL_WIDTH)),
            _const_spec((POOL_WIDTH + DIFF_WIDTH, d)),
        ],
        out_specs=[out_tok(), out_tok()],
        out_shape=[
            jax.ShapeDtypeStruct((b, n_out * TILE, d), F32),
            jax.ShapeDtypeStruct((b, n_out * TILE, d), BF16),
        ],
        scratch_shapes=[pltpu.VMEM((group, TILE + 2 * HALO, POOL_WIDTH), F32)],
        compiler_params=_cparams(2),
    )(*tokens, u, u, u, att, mod_g, ng_l, pool_w, pool_scale.reshape(1, POOL_WIDTH), w_out)


def _ffn_kernel(seg_tiles, n_tiles, x_ref, h_ref, hp_ref, hn_ref, mod_ref, ng_ref, wg_ref, wu_ref,
                cw_ref, cb_ref, wd_ref, o_ref, a_ref, act_ref):
    j = pl.program_id(1)
    prev_ok = jnp.logical_and(j != 0, j != seg_tiles)
    next_ok = jnp.logical_and(j != n_tiles - 1, j != seg_tiles - 1)
    group = x_ref.shape[0]
    ext = TILE + 2 * HALO
    zero = jnp.zeros((HALO, h_ref.shape[2]), BF16)
    pieces = []
    for g in range(group):
        pieces += [jnp.where(prev_ok, hp_ref[g], zero), h_ref[g], jnp.where(next_ok, hn_ref[g], zero)]
    h_ext = jnp.concatenate(pieces, axis=0)
    h = jnp.concatenate([h_ref[g] for g in range(group)], axis=0) if group > 1 else h_ref[0]
    for c in range(D_FF // FF_CHUNK):
        cols = slice(c * FF_CHUNK, (c + 1) * FF_CHUNK)
        a_ref[...] = _dot(h_ext, wg_ref[:, cols])
        up = _dot(h, wu_ref[:, cols])
        for g in range(group):
            base = g * ext + HALO
            a = (a_ref[base - 1:base - 1 + TILE, :] * cw_ref[0:1, cols]
                 + a_ref[base:base + TILE, :] * cw_ref[1:2, cols]
                 + a_ref[base + 1:base + 1 + TILE, :] * cw_ref[2:3, cols] + cb_ref[:, cols])
            act_ref[g * TILE:(g + 1) * TILE, cols] = (_silu(a) * up[g * TILE:(g + 1) * TILE]).astype(BF16)
    for g in range(group):
        f = _dot(act_ref[g * TILE:(g + 1) * TILE, :], wd_ref[...])
        o_ref[g] = x_ref[g] + _rms(f, ng_ref[3:4, :] * mod_ref[g, 5:6, :])


def _ffn(x, h2, mod_g, ng_l, w_gate, w_up, conv_w, conv_b, w_down, ctx_tiles, group):
    b, t, d = x.shape
    nt = t // TILE
    nh = TILE // HALO

    def mod_map(i, j):
        return (jnp.where(j < ctx_tiles, b // group, i), 0, 0)

    tok = lambda: pl.BlockSpec((group, TILE, d), lambda i, j: (i, j, 0))
    return pl.pallas_call(
        functools.partial(_ffn_kernel, ctx_tiles, nt),
        grid=(b // group, nt),
        in_specs=[
            tok(),
            tok(),
            pl.BlockSpec((group, HALO, d), lambda i, j: (i, jnp.maximum(j * nh - 1, 0), 0)),
            pl.BlockSpec((group, HALO, d), lambda i, j: (i, jnp.minimum((j + 1) * nh, nt * nh - 1), 0)),
            pl.BlockSpec((group, 6, d), mod_map),
            _const_spec((4, d)),
            _const_spec((d, D_FF)),
            _const_spec((d, D_FF)),
            _const_spec((3, D_FF)),
            _const_spec((1, D_FF)),
            _const_spec((D_FF, d)),
        ],
        out_specs=tok(),
        out_shape=jax.ShapeDtypeStruct((b, t, d), F32),
        scratch_shapes=[pltpu.VMEM((group * (TILE + 2 * HALO), FF_CHUNK), F32),
                        pltpu.VMEM((group * TILE, D_FF), BF16)],
        compiler_params=_cparams(2),
    )(x, h2, h2, h2, mod_g, ng_l, w_gate, w_up, conv_w, conv_b.reshape(1, D_FF), w_down)


def _odd_in_kernel(layer, x_ref, mod_ref, ng_ref, cos_ref, sin_ref, wq_ref, wkv_ref, wkr_ref, wh_ref,
                   qn_ref, wuq_ref, kvn_ref, wukv_ref, lb_ref,
                   q_ref, kc_ref, vt_ref, hq_ref, hkf_ref, hkb_ref, lff_ref, lfb_ref, hv_ref, hg_ref):
    cos = cos_ref[...]
    sin = sin_ref[...]
    w = HG_QK_WIDTH
    lbs = []
    for direction in range(2):
        lrows = [lb_ref[direction, i:i + 1, :] for i in range(lb_ref.shape[1])]
        top = functools.reduce(jnp.maximum, lrows)
        ex = [jnp.exp(r - top) for r in lrows]
        lbs.append(sum(ex[1:layer + 1], jnp.zeros_like(top)) / sum(ex))

    for g in range(x_ref.shape[0]):
        hb = _prenorm(g, x_ref, mod_ref, ng_ref)
        cq = _rms(_dot(hb, wq_ref[...]), qn_ref[...])
        q = _dot(cq.astype(BF16), wuq_ref[...]) * ((MLA_NOPE_DIM + MLA_ROPE_DIM) ** -0.5 * LOG2E)
        ckv = _rms(_dot(hb, wkv_ref[...]), kvn_ref[...])
        kv = _dot(ckv.astype(BF16), wukv_ref[...])
        kr = _rope(_dot(hb, wkr_ref[...]), cos, sin).astype(BF16)
        for hd in range(MLA_HEADS):
            qb = hd * MLA_QK_PAD
            q_ref[g, :, qb:qb + MLA_NOPE_DIM] = q[:, qb:qb + MLA_NOPE_DIM].astype(BF16)
            q_ref[g, :, qb + MLA_NOPE_DIM:qb + MLA_QK_PAD] = _rope(
                q[:, qb + MLA_NOPE_DIM:qb + MLA_QK_PAD], cos, sin).astype(BF16)
            kb = hd * (MLA_NOPE_DIM + MLA_V_DIM)
            kc_ref[g, hd, :, :MLA_NOPE_DIM] = kv[:, kb:kb + MLA_NOPE_DIM].astype(BF16)
            kc_ref[g, hd, :, MLA_NOPE_DIM:] = kr
            vt_ref[g, hd * MLA_V_DIM:(hd + 1) * MLA_V_DIM, :] = kv[
                :, kb + MLA_NOPE_DIM:kb + MLA_NOPE_DIM + MLA_V_DIM].T.astype(BF16)

        hq_ref[g] = _silu(_dot(hb, wh_ref[:, :w])).astype(BF16)
        hv_ref[g] = _dot(hb, wh_ref[:, 3 * w:4 * w]).astype(BF16)
        hg_ref[g] = _silu(_dot(hb, wh_ref[:, 4 * w:])).astype(BF16)
        for direction, (k_ref, lf_ref) in enumerate(((hkf_ref, lff_ref), (hkb_ref, lfb_ref))):
            lb = lbs[direction]
            f = lb + (1.0 - lb) * jax.nn.sigmoid(_dot(hb, wh_ref[:, (1 + direction) * w:(2 + direction) * w]))
            k_ref[g] = (1.0 - f).astype(BF16)
            lf_ref[g] = jnp.log2(f)


def _odd_in(x, mod_g, ng_l, cos, sin, w_in, q_norm, w_uq, kv_norm, w_ukv, hgrn_lb, layer, ctx_tiles, group):
    b, t, d = x.shape
    nt = t // TILE
    o1 = MLA_Q_RANK
    o2 = o1 + MLA_KV_RANK
    o3 = o2 + MLA_ROPE_DIM
    w_q = w_in[:, :o1]
    w_kv = w_in[:, o1:o2]
    w_kr = jnp.pad(w_in[:, o2:o3], ((0, 0), (0, LANES - MLA_ROPE_DIM)))
    w_h = w_in[:, o3:]
    qk = MLA_NOPE_DIM + MLA_ROPE_DIM
    w_uq_pad = jnp.pad(w_uq.reshape(MLA_Q_RANK, MLA_HEADS, qk),
                       ((0, 0), (0, 0), (0, MLA_QK_PAD - qk))).reshape(MLA_Q_RANK, MLA_HEADS * MLA_QK_PAD)

    def mod_map(i, j):
        return (jnp.where(j < ctx_tiles, b // group, i), 0, 0)

    tok = lambda w: pl.BlockSpec((group, TILE, w), lambda i, j: (i, j, 0))
    sds = lambda w, dt: jax.ShapeDtypeStruct((b, t, w), dt)
    hw = HG_QK_WIDTH
    return pl.pallas_call(
        functools.partial(_odd_in_kernel, layer),
        grid=(b // group, nt),
        in_specs=[
            tok(d),
            pl.BlockSpec((group, 6, d), mod_map),
            _const_spec((4, d)),
            pl.BlockSpec((TILE, LANES), lambda i, j: (j, 0)),
            pl.BlockSpec((TILE, LANES), lambda i, j: (j, 0)),
            _const_spec(w_q.shape),
            _const_spec(w_kv.shape),
            _const_spec(w_kr.shape),
            _const_spec(w_h.shape),
            _const_spec((1, MLA_Q_RANK)),
            _const_spec(w_uq_pad.shape),
            _const_spec((1, MLA_KV_RANK)),
            _const_spec(w_ukv.shape),
            _const_spec(hgrn_lb.shape),
        ],
        out_specs=[
            tok(MLA_HEADS * MLA_QK_PAD),
            pl.BlockSpec((group, MLA_HEADS, TILE, MLA_QK_PAD), lambda i, j: (i, 0, j, 0)),
            pl.BlockSpec((group, MLA_WIDTH, TILE), lambda i, j: (i, 0, j)),
            tok(hw), tok(hw), tok(hw), tok(hw), tok(hw), tok(hw), tok(hw),
        ],
        out_shape=[
            sds(MLA_HEADS * MLA_QK_PAD, BF16),
            jax.ShapeDtypeStruct((b, MLA_HEADS, t, MLA_QK_PAD), BF16),
            jax.ShapeDtypeStruct((b, MLA_WIDTH, t), BF16),
            sds(hw, BF16), sds(hw, BF16), sds(hw, BF16), sds(hw, F32), sds(hw, F32), sds(hw, BF16),
            sds(hw, BF16),
        ],
        compiler_params=_cparams(2),
    )(x, mod_g, ng_l, cos, sin, w_q, w_kv, w_kr, w_h, q_norm.reshape(1, -1), w_uq_pad,
      kv_norm.reshape(1, -1), w_ukv, hgrn_lb)


def _mla_attn_kernel(n_ctx, q_off, q_tiles, *refs):
    q_refs = refs[:q_tiles]
    kc_ref, vt_ref, o_ref, s_ref, e_ref, ksq_ref = refs[q_tiles:]

    @pl.when(pl.program_id(1) == 0)
    def _():
        all_dims = jnp.ones((1, MLA_QK_PAD), F32)
        for h in range(MLA_HEADS):
            kf = kc_ref[0, h].astype(F32)
            _store_key_sq(ksq_ref, h, slice(0, MLA_TQ), kf * kf, all_dims, n_ctx)

    def attend(nk, ctx_only):
        ksq_row = 0 if ctx_only else MLA_HEADS

        def finish(p, o_t, l):
            h, qi = divmod(p, q_tiles)
            o_ref[0, qi * MLA_TQ:(qi + 1) * MLA_TQ, h * MLA_V_DIM:(h + 1) * MLA_V_DIM] = (
                o_t * (1.0 / l)).T.astype(BF16)

        def scores_operands(p):
            h, qi = divmod(p, q_tiles)
            return kc_ref.at[0, h, :nk, :], q_refs[qi][0, :, h * MLA_QK_PAD:(h + 1) * MLA_QK_PAD]

        def values_t(p):
            h = p // q_tiles
            return vt_ref.at[0, h * MLA_V_DIM:(h + 1) * MLA_V_DIM, :nk]

        def key_sq(p):
            r = ksq_row + p // q_tiles
            return ksq_ref[r:r + 1, :]

        _attend_t(MLA_HEADS * q_tiles, nk, scores_operands, values_t, key_sq, s_ref, e_ref, finish)

    if q_off * MLA_TQ >= n_ctx:
        attend(kc_ref.shape[2], False)
    else:
        is_ctx = pl.program_id(1) + q_off < n_ctx // MLA_TQ

        @pl.when(is_ctx)
        def _():
            attend(n_ctx, True)

        @pl.when(jnp.logical_not(is_ctx))
        def _():
            attend(kc_ref.shape[2], False)


def _mla_attn(q, kc, vt, n_ctx, q_off):
    b, _, t, _ = kc.shape
    nq = t // MLA_TQ - q_off
    q_tiles = MLA_STEP_TILES if (q_off * MLA_TQ >= n_ctx and nq % MLA_STEP_TILES == 0) else 1

    def q_spec(qi):
        return pl.BlockSpec((1, MLA_TQ, MLA_HEADS * MLA_QK_PAD), lambda i, j: (i, j * q_tiles + qi + q_off, 0))

    return pl.pallas_call(
        functools.partial(_mla_attn_kernel, n_ctx, q_off, q_tiles),
        grid=(b, nq // q_tiles),
        in_specs=[q_spec(qi) for qi in range(q_tiles)] + [
            pl.BlockSpec((1, MLA_HEADS, t, MLA_QK_PAD), lambda i, j: (i, 0, 0, 0)),
            pl.BlockSpec((1, MLA_WIDTH, t), lambda i, j: (i, 0, 0)),
        ],
        out_specs=pl.BlockSpec((1, q_tiles * MLA_TQ, MLA_WIDTH), lambda i, j: (i, j, 0)),
        out_shape=jax.ShapeDtypeStruct((b, nq * MLA_TQ, MLA_WIDTH), BF16),
        scratch_shapes=[pltpu.VMEM((2, t, MLA_TQ), F32), pltpu.VMEM((2, t, MLA_TQ), BF16),
                        pltpu.VMEM((2 * MLA_HEADS, MLA_TQ), F32)],
        compiler_params=_cparams(2),
    )(*([q] * q_tiles), kc, vt)


def _hgrn_triangle(reverse):
    t = np.arange(HG_CHUNK)[:, None]
    j = np.arange(HG_CHUNK)[None, :]
    return ((j >= t) if reverse else (j <= t)).astype(np.float32)


def _hgrn_masks(reverse):
    c = HG_CHUNK
    t = lax.broadcasted_iota(jnp.int32, (c, c), 0)
    s = lax.broadcasted_iota(jnp.int32, (c, c), 1)
    tq = lax.broadcasted_iota(jnp.int32, (c, 1), 0)
    masks = []
    for m in HG_LEVELS:
        half = m // 2
        shift = int(math.log2(m))
        t_hi = (t & (m - 1)) >= half
        s_hi = (s & (m - 1)) >= half
        same = (t >> shift) == (s >> shift)
        if not reverse:
            pair = jnp.where(same, jnp.where(t_hi, jnp.where(s_hi, 0.0, 1.0), 0.0), 0.0)
            is_q = (tq & (m - 1)) >= half
        else:
            pair = jnp.where(same, jnp.where(t_hi, 0.0, jnp.where(s_hi, 1.0, 0.0)), 0.0)
            is_q = (tq & (m - 1)) < half
        masks.append((pair > 0.5, is_q))
    return masks, t == s


def _hgrn_reference_rows(cum, m, reverse):
    c = HG_CHUNK
    half = m // 2
    ref_in_block = half if reverse else half - 1
    if m >= 2 * SUBLANES:
        return jnp.concatenate(
            [jnp.broadcast_to(cum[b0 + ref_in_block:b0 + ref_in_block + 1, :], (m, LANES))
             for b0 in range(0, c, m)], axis=0)
    cum3 = cum.reshape(c // SUBLANES, SUBLANES, LANES)
    r = lax.broadcasted_iota(jnp.int32, cum3.shape, 1)
    out = None
    for b0 in reversed(range(0, SUBLANES, m)):
        pick = jnp.broadcast_to(cum3[:, b0 + ref_in_block:b0 + ref_in_block + 1, :], cum3.shape)
        out = pick if out is None else jnp.where(r < b0 + m, pick, out)
    return out.reshape(c, LANES)


def _hgrn_scan_body(reverse, tri_ref, lf_ref, q_ref, k_ref, v_ref, st_ref, emit):
    masks, eye = _hgrn_masks(reverse)
    tri = tri_ref[...]
    c = HG_CHUNK
    n_chunks = HG_BLOCK // c
    chunk_order = list(range(n_chunks - 1, -1, -1) if reverse else range(n_chunks))
    group = lf_ref.shape[0]
    probs = [(g, ci, h) for ci in chunk_order for g in range(group) for h in range(HG_HEADS)]
    where = lambda g, ci, h: (g, slice(ci * c, (ci + 1) * c), slice(h * HG_K_DIM, (h + 1) * HG_K_DIM))

    @pl.when(pl.program_id(1) == 0)
    def _():
        st_ref[...] = jnp.zeros_like(st_ref)

    cums, lfs = [], []
    for p in probs:
        g, rows, cols = where(*p)
        lf = lf_ref[g, rows, cols]
        lfs.append(lf)
        hi = lf.astype(BF16)
        r1 = lf - hi.astype(F32)
        mid = r1.astype(BF16)
        lo = (r1 - mid.astype(F32)).astype(BF16)
        c3 = _dot(tri, jnp.concatenate([hi, mid, lo], axis=1))
        cums.append(c3[:, :LANES] + c3[:, LANES:2 * LANES] + c3[:, 2 * LANES:])

    qf, kf, qe, ks, decay, scores = [], [], [], [], [], []
    for p, cum in zip(probs, cums):
        g, rows, cols = where(*p)
        total = cum[0:1] if reverse else cum[c - 1:c]
        q = q_ref[g, rows, cols]
        k = k_ref[g, rows, cols]
        qf.append(q.astype(F32))
        kf.append(k.astype(F32))
        qe.append((qf[-1] * jnp.exp2(cum)).astype(BF16))
        ks.append((kf[-1] * jnp.exp2(total - cum)).astype(BF16))
        decay.append(jnp.exp2(total))
        scores.append(jnp.where(eye, _dot_nt(q, k), 0.0))

    for m, (pair, is_q) in zip(HG_LEVELS, masks):
        for i, cum in enumerate(cums):
            if m == 2:
                x = jnp.where(is_q, qf[i] * jnp.exp2(lfs[i]), kf[i]).astype(BF16)
            else:
                d = cum - _hgrn_reference_rows(cum, m, reverse)
                x = (jnp.where(is_q, qf[i], kf[i]) * jnp.exp2(-jnp.abs(d))).astype(BF16)
            scores[i] = jnp.where(pair, _dot_nt(x, x), scores[i])

    intra, update = [], []
    for i, p in enumerate(probs):
        g, rows, cols = where(*p)
        v = v_ref[g, rows, cols]
        intra.append(_dot(scores[i].astype(BF16), v))
        update.append(_dot(v.astype(F32).T.astype(BF16), ks[i]))

    for gh in range(group * HG_HEADS):
        st = st_ref[gh]
        for n in range(n_chunks):
            i = n * group * HG_HEADS + gh
            emit(*where(*probs[i]), intra[i] + _dot_nt(qe[i], st.astype(BF16)))
            st = st * decay[i] + update[i]
        st_ref[gh] = st


def _hgrn_fwd_kernel(tri_ref, lf_ref, q_ref, k_ref, v_ref, o_ref, st_ref):
    def emit(g, rows, cols, o):
        o_ref[g, rows, cols] = o

    _hgrn_scan_body(False, tri_ref, lf_ref, q_ref, k_ref, v_ref, st_ref, emit)


def _hgrn_bwd_kernel(tri_ref, lf_ref, q_ref, k_ref, v_ref, of_ref, gate_ref, gn_ref, o_ref, st_ref):
    def emit(g, rows, cols, o):
        o_ref[g, rows, cols] = (_rms(o + of_ref[g, rows, cols], gn_ref[...])
                                * gate_ref[g, rows, cols].astype(F32)).astype(BF16)

    _hgrn_scan_body(True, tri_ref, lf_ref, q_ref, k_ref, v_ref, st_ref, emit)


def _hgrn_scan(reverse, lf, q, k, v, ctx_blocks, extra=()):
    b, t, w = lf.shape
    nb = t // HG_BLOCK
    tri = jnp.asarray(_hgrn_triangle(reverse), BF16)
    if reverse:
        blk = lambda j: jnp.where(j < ctx_blocks, ctx_blocks - 1 - j, nb - 1 - (j - ctx_blocks))
    else:
        blk = lambda j: j
    group = _group_size(b, HG_GROUP)
    tok = pl.BlockSpec((group, HG_BLOCK, w), lambda i, j: (i, blk(j), 0))
    in_specs = [_const_spec(tri.shape), tok, tok, tok, tok]
    args = [tri, lf, q, k, v]
    if reverse:
        o_f, gate, gnorm = extra
        in_specs += [tok, tok, _const_spec((1, HG_V_DIM))]
        args += [o_f, gate, gnorm.reshape(1, HG_V_DIM)]
    return pl.pallas_call(
        _hgrn_bwd_kernel if reverse else _hgrn_fwd_kernel,
        grid=(b // group, nb),
        in_specs=in_specs,
        out_specs=tok,
        out_shape=jax.ShapeDtypeStruct((b, t, w), BF16 if reverse else F32),
        scratch_shapes=[pltpu.VMEM((group * HG_HEADS, HG_V_DIM, HG_K_DIM), F32)],
        compiler_params=_cparams(2),
    )(*args)


def _odd_out_kernel(x_ref, a_ref, g_ref, mod_ref, ng_ref, wo_ref, xo_ref, h2_ref):
    for g in range(x_ref.shape[0]):
        y = _dot(a_ref[g], wo_ref[:MLA_WIDTH, :]) + _dot(g_ref[g], wo_ref[MLA_WIDTH:, :])
        _residual_and_prenorm(g, x_ref, y, mod_ref, ng_ref, xo_ref, h2_ref)


def _odd_out(x, att, hg, mod_g, ng_l, w_out, ctx_tiles, t_off, group):
    b, t, d = x.shape
    n_out = t // TILE - t_off

    def mod_map(i, j):
        return (jnp.where(j + t_off < ctx_tiles, b // group, i), 0, 0)

    out_tok = lambda w: pl.BlockSpec((group, TILE, w), lambda i, j: (i, j, 0))
    return pl.pallas_call(
        _odd_out_kernel,
        grid=(b // group, n_out),
        in_specs=[
            pl.BlockSpec((group, TILE, d), lambda i, j: (i, j + t_off, 0)),
            out_tok(MLA_WIDTH),
            pl.BlockSpec((group, TILE, HG_WIDTH), lambda i, j: (i, j + t_off, 0)),
            pl.BlockSpec((group, 6, d), mod_map),
            _const_spec((4, d)),
            _const_spec((MLA_WIDTH + HG_WIDTH, d)),
        ],
        out_specs=[out_tok(d), out_tok(d)],
        out_shape=[
            jax.ShapeDtypeStruct((b, n_out * TILE, d), F32),
            jax.ShapeDtypeStruct((b, n_out * TILE, d), BF16),
        ],
        compiler_params=_cparams(2),
    )(x, att, hg, mod_g, ng_l, w_out)


def _rope_tables(n_ctx, n_lat):
    rows = n_lat // GRID_W
    pos = jnp.stack([jnp.repeat(jnp.arange(rows), GRID_W), jnp.tile(jnp.arange(GRID_W), rows)], axis=-1)
    axis_dim = DIFF_QK_DIM // 2
    inv_freq = ROPE_THETA ** (-jnp.arange(0, axis_dim, 2, dtype=F32) / axis_dim)
    ang = pos.astype(F32)[..., None] * inv_freq
    cos = jnp.cos(ang)
    sin = jnp.sin(ang)
    cos64 = jnp.concatenate([cos[:, 0], cos[:, 0], cos[:, 1], cos[:, 1]], axis=-1)
    sin64 = jnp.concatenate([-sin[:, 0], sin[:, 0], -sin[:, 1], sin[:, 1]], axis=-1)
    cos64 = jnp.concatenate([jnp.ones((n_ctx, 64), F32), cos64], axis=0)
    sin64 = jnp.concatenate([jnp.zeros((n_ctx, 64), F32), sin64], axis=0)
    return cos64, sin64


def kernel(x, c, ctx, c_ctx, ada_w, ada_b, norm_g, mix_w_out, ffn_w_gate, ffn_w_up, ffn_conv_w, ffn_conv_b,
           ffn_w_down, ev_w_in, pool_w, pool_scale, diff_lambda, diff_subln, od_w_in, mla_q_norm, mla_w_uq,
           mla_kv_norm, mla_w_ukv, hgrn_norm, hgrn_lb):
    b, n_lat, d = x.shape
    n_ctx = ctx.shape[1]
    depth = ada_w.shape[0]
    assert d == D_MODEL and n_ctx % TILE == 0 and n_lat % TILE == 0 and n_lat % GRID_W == 0
    ctx_tiles = n_ctx // TILE
    group = _group_size(b, MAX_GROUP)
    group_odd_in = _group_size(b, ODD_IN_GROUP)

    rows = -(-(b + 1) // SUBLANES) * SUBLANES
    cond = jnp.concatenate([c, c_ctx[None, :], jnp.zeros((rows - b - 1, d), F32)], axis=0)
    mod = _modulation(cond, ada_w, ada_b).reshape(depth, rows, 6, d)

    cos64, sin64 = _rope_tables(n_ctx, n_lat)
    cos_diff = jnp.tile(cos64, (1, 2 * DIFF_HEADS))
    sin_diff = jnp.tile(sin64, (1, 2 * DIFF_HEADS))
    pad = ((0, 0), (0, LANES - MLA_ROPE_DIM))
    cos_mla = jnp.pad(cos64, pad, constant_values=1.0)
    sin_mla = jnp.pad(sin64, pad)

    xs, xs_ctx = x, ctx
    for layer in range(depth):
        last = layer == depth - 1
        j = layer // 2
        t_off = ctx_tiles if last else 0
        mod_g = _group_mod(mod[layer], b, group)
        ng_l = norm_g[layer]
        w_out = mix_w_out[layer].astype(BF16)
        if layer % 2 == 0:
            lam_init = 0.8 - 0.6 * math.exp(-0.3 * layer)
            u, qt, k, vt = _even_in(xs, xs_ctx, mod_g, ng_l, cos_diff, sin_diff, ev_w_in[j].astype(BF16),
                                    ctx_tiles, group)
            att = _diff_attn(qt, k, vt, diff_lambda[j], diff_subln[j], lam_init, n_ctx)
            x_mid, h2 = _even_out(xs, xs_ctx, u, att, mod_g, ng_l, pool_w[j].astype(BF16), pool_scale[j], w_out,
                                  ctx_tiles, t_off, group)
        else:
            assert xs_ctx is None
            (qt, kc, vt, hq, hk_f, hk_b, lf_f, lf_b, hv, hgate) = _odd_in(
                xs, _group_mod(mod[layer], b, group_odd_in), ng_l, cos_mla, sin_mla, od_w_in[j].astype(BF16),
                mla_q_norm[j], mla_w_uq[j].astype(BF16), mla_kv_norm[j], mla_w_ukv[j].astype(BF16), hgrn_lb,
                layer, ctx_tiles, group_odd_in)
            att = _mla_attn(qt, kc, vt, n_ctx, t_off * (TILE // MLA_TQ))
            o_f = _hgrn_scan(False, lf_f, hq, hk_f, hv, n_ctx // HG_BLOCK)
            hg = _hgrn_scan(True, lf_b, hq, hk_b, hv, n_ctx // HG_BLOCK, (o_f, hgate, hgrn_norm[j]))
            x_mid, h2 = _odd_out(xs, att, hg, mod_g, ng_l, w_out, ctx_tiles, t_off, group)
        xs = _ffn(x_mid, h2, mod_g, ng_l, ffn_w_gate[layer].astype(BF16), ffn_w_up[layer].astype(BF16),
                  ffn_conv_w[layer], ffn_conv_b[layer], ffn_w_down[layer].astype(BF16),
                  0 if last else ctx_tiles, group)
        xs_ctx = None
    return xs
```

```python
import functools
import math

import jax
import jax.numpy as jnp
import numpy as np
from jax import lax
from jax.experimental import pallas as pl
from jax.experimental.pallas import tpu as pltpu

F32 = jnp.float32
BF16 = jnp.bfloat16

D_MODEL = 1024
GRID_W = 64
RMS_EPS = 1e-6
ROPE_THETA = 10000.0
LOG2E = math.log2(math.e)

POOL_WINDOWS = (2, 4, 8, 16)
POOL_GROUPS = 4
POOL_WIDTH = 512
POOL_GROUP_DIM = POOL_WIDTH // POOL_GROUPS

DIFF_HEADS = 4
DIFF_QK_DIM = 64
DIFF_V_DIM = 128
DIFF_QK_WIDTH = 2 * DIFF_HEADS * DIFF_QK_DIM
DIFF_WIDTH = DIFF_HEADS * DIFF_V_DIM

MLA_HEADS = 4
MLA_Q_RANK = 512
MLA_KV_RANK = 256
MLA_NOPE_DIM = 128
MLA_ROPE_DIM = 64
MLA_V_DIM = 128
MLA_QK_PAD = 256
MLA_WIDTH = MLA_HEADS * MLA_V_DIM

HG_HEADS = 4
HG_K_DIM = 128
HG_V_DIM = 128
HG_QK_WIDTH = HG_HEADS * HG_K_DIM
HG_WIDTH = HG_HEADS * HG_V_DIM
HG_CHUNK = 64
HG_LEVELS = (64, 32, 16, 8, 4, 2)

D_FF = 2816
FF_CHUNK = 256

LANES = 128
SUBLANES = 8
BF16_SUBLANES = 16
MXU_TILE = 256

TILE = 256
MAX_GROUP = 4
ODD_IN_GROUP = 4
DIFF_TQ = 128
DIFF_STEP_Q = 256
MLA_TQ = 256
MLA_STEP_TILES = 2
HG_BLOCK = 256
HG_GROUP = 2
HG_STACK = 2
HALO = BF16_SUBLANES
VMEM_LIMIT = 56 * 1024 * 1024
SOFTMAX_DENOM_FLOOR = 2.0 ** -100
SCORE_BOUND_SLACK = 1.0 + 2.0 ** -6


def _cparams(n_axes):
    return pltpu.CompilerParams(
        dimension_semantics=("arbitrary",) * n_axes, vmem_limit_bytes=VMEM_LIMIT)


def _dot(a, b):
    return jnp.dot(a, b, preferred_element_type=F32)


def _dot_nt(a, b):
    return lax.dot_general(a, b, (((1,), (1,)), ((), ())), preferred_element_type=F32)


def _rms(x, g):
    return x * lax.rsqrt(jnp.mean(x * x, axis=-1, keepdims=True) + RMS_EPS) * g


def _silu(x):
    return x * jax.nn.sigmoid(x)


def _const_spec(shape):
    zeros = (0,) * len(shape)
    return pl.BlockSpec(shape, lambda *_: zeros, pipeline_mode=pl.Buffered(1))


def _group_size(b, max_group):
    return max(g for g in range(1, max_group + 1) if b % g == 0)


def _group_mod(mod_l, b, group):
    return jnp.concatenate([mod_l[:b], jnp.broadcast_to(mod_l[b:b + 1], (group,) + mod_l.shape[1:])], axis=0)


def _rope(x, cos, sin_signed):
    n = x.shape[-1]
    lane = lax.broadcasted_iota(jnp.int32, x.shape, 1)
    first_half = (lane & 31) < 16
    partner = jnp.where(first_half, pltpu.roll(x, n - 16, 1), pltpu.roll(x, 16, 1))
    return x * cos + partner * sin_signed


def _prenorm(g, x_ref, mod_ref, ng_ref):
    h = _rms(x_ref[g], ng_ref[0:1, :] * (1.0 + mod_ref[g, 1:2, :])) + mod_ref[g, 0:1, :]
    return h.astype(BF16)


def _mod_kernel(c_ref, w_ref, b_ref, o_ref):
    s = _silu(c_ref[...])
    w = w_ref[0]
    s_hi = s.astype(BF16)
    s_lo = (s - s_hi.astype(F32)).astype(BF16)
    w_hi = w.astype(BF16)
    w_lo = (w - w_hi.astype(F32)).astype(BF16)
    o_ref[0] = _dot(s_hi, w_hi) + _dot(s_hi, w_lo) + _dot(s_lo, w_hi) + b_ref[0]


def _modulation(cond, ada_w, ada_b):
    depth, d, n = ada_w.shape
    rows = cond.shape[0]
    tn = 1536
    return pl.pallas_call(
        _mod_kernel,
        grid=(depth, n // tn),
        in_specs=[
            pl.BlockSpec((rows, d), lambda l, j: (0, 0)),
            pl.BlockSpec((1, d, tn), lambda l, j: (l, 0, j)),
            pl.BlockSpec((1, 1, tn), lambda l, j: (l, 0, j)),
        ],
        out_specs=pl.BlockSpec((1, rows, tn), lambda l, j: (l, 0, j)),
        out_shape=jax.ShapeDtypeStruct((depth, rows, n), F32),
        compiler_params=_cparams(2),
    )(cond, ada_w, ada_b.reshape(depth, 1, n))


class _SplitRows:
    def __init__(self, ctx_ref, lat_ref, is_ctx):
        self.ctx_ref, self.lat_ref, self.is_ctx = ctx_ref, lat_ref, is_ctx
        self.shape = lat_ref.shape

    def __getitem__(self, g):
        return jnp.where(self.is_ctx, self.ctx_ref[g], self.lat_ref[g])


def _token_rows(split_ctx_tiles, t_off, refs):
    if split_ctx_tiles is None:
        return refs[0], refs[1:]
    return _SplitRows(refs[0], refs[1], pl.program_id(1) + t_off < split_ctx_tiles), refs[2:]


def _token_specs(x, x_ctx, group, ctx_tiles, t_off):
    d = x.shape[2]
    if x_ctx is None:
        return [x], [pl.BlockSpec((group, TILE, d), lambda i, j: (i, j + t_off, 0))]
    return [x_ctx, x], [
        pl.BlockSpec((group, TILE, d), lambda i, j: (i, jnp.minimum(j + t_off, ctx_tiles - 1), 0)),
        pl.BlockSpec((group, TILE, d), lambda i, j: (i, jnp.maximum(j + t_off - ctx_tiles, 0), 0)),
    ]


def _even_in_kernel(split_ctx_tiles, *refs):
    x_ref, (mod_ref, ng_ref, cos_ref, sin_ref, w_ref, u_ref, q_ref, k_ref, vt_ref) = _token_rows(
        split_ctx_tiles, 0, refs)
    cos = cos_ref[...]
    sin = sin_ref[...]
    o_q = POOL_WIDTH
    o_k = o_q + DIFF_QK_WIDTH
    o_v = o_k + DIFF_QK_WIDTH
    for g in range(x_ref.shape[0]):
        p = _dot(_prenorm(g, x_ref, mod_ref, ng_ref), w_ref[...])
        u_ref[g] = p[:, :o_q].astype(BF16)
        q_ref[g] = (_rope(p[:, o_q:o_k], cos, sin) * (DIFF_QK_DIM ** -0.5 * LOG2E)).astype(BF16)
        k_ref[g] = _rope(p[:, o_k:o_v], cos, sin).astype(BF16)
        vt_ref[g] = p[:, o_v:].T.astype(BF16)


def _even_in(x, x_ctx, mod_g, ng_l, cos, sin, w_in, ctx_tiles, group):
    b, t, d = x.shape
    if x_ctx is not None:
        t += x_ctx.shape[1]
    nt = t // TILE
    n_in = w_in.shape[1]

    def mod_map(i, j):
        return (jnp.where(j < ctx_tiles, b // group, i), 0, 0)

    tok = lambda w: pl.BlockSpec((group, TILE, w), lambda i, j: (i, j, 0))
    tok_t = lambda w: pl.BlockSpec((group, w, TILE), lambda i, j: (i, 0, j))
    tokens, token_specs = _token_specs(x, x_ctx, group, ctx_tiles, 0)
    return pl.pallas_call(
        functools.partial(_even_in_kernel, None if x_ctx is None else ctx_tiles),
        grid=(b // group, nt),
        in_specs=token_specs + [
            pl.BlockSpec((group, 6, d), mod_map),
            _const_spec((4, d)),
            pl.BlockSpec((TILE, DIFF_QK_WIDTH), lambda i, j: (j, 0)),
            pl.BlockSpec((TILE, DIFF_QK_WIDTH), lambda i, j: (j, 0)),
            _const_spec((d, n_in)),
        ],
        out_specs=[tok(POOL_WIDTH), tok(DIFF_QK_WIDTH), tok(DIFF_QK_WIDTH), tok_t(DIFF_WIDTH)],
        out_shape=[
            jax.ShapeDtypeStruct((b, t, POOL_WIDTH), BF16),
            jax.ShapeDtypeStruct((b, t, DIFF_QK_WIDTH), BF16),
            jax.ShapeDtypeStruct((b, t, DIFF_QK_WIDTH), BF16),
            jax.ShapeDtypeStruct((b, DIFF_WIDTH, t), BF16),
        ],
        compiler_params=_cparams(2),
    )(*tokens, mod_g, ng_l, cos, sin, w_in)


def _key_halves(nk, align):
    if nk <= MXU_TILE:
        return ((0, nk),)
    first = -(-(nk // 2) // align) * align
    return ((0, first), (first, nk))


def _attend_t(n_heads, nk, scores_operands, values_t, key_sq, s_ref, e_ref, finish):
    halves = _key_halves(nk, LANES)
    value_halves = _key_halves(nk, MXU_TILE)

    def weighted_values(h):
        vt = values_t(h)
        return sum(_dot(vt[:, a:b], e_ref[h % 2, a:b, :]) for a, b in value_halves)

    l_low = None
    for h in range(n_heads):
        keys, q = scores_operands(h)
        q_sq = (q.astype(F32) * q.astype(F32)).astype(BF16)
        q_sq = _dot_nt(jnp.ones((SUBLANES, q.shape[1]), BF16), q_sq)[0:1]
        bound = jnp.sqrt(q_sq * key_sq(h)) * SCORE_BOUND_SLACK
        l = None
        for a, b in halves:
            e = jnp.exp2(_dot_nt(keys[a:b, :], q) - bound)
            e_ref[h % 2, a:b, :] = e.astype(BF16)
            part = jnp.sum(e, axis=0, keepdims=True)
            l = part if l is None else l + part
        finish(h, weighted_values(h), l)
        l_low = l if l_low is None else jnp.minimum(l_low, l)
    accurate = jnp.min(l_low) > SOFTMAX_DENOM_FLOOR

    @pl.when(jnp.logical_not(accurate))
    def _():
        def scores(h):
            keys, q = scores_operands(h)
            tops = []
            for a, b in halves:
                s = _dot_nt(keys[a:b, :], q)
                s_ref[h % 2, a:b, :] = s
                tops.append(jnp.max(s, axis=0, keepdims=True))
            return functools.reduce(jnp.maximum, tops)

        top = scores(0)
        for h in range(n_heads):
            nxt = scores(h + 1) if h + 1 < n_heads else None
            e = jnp.exp2(s_ref[h % 2, :nk, :] - top)
            e_ref[h % 2, :nk, :] = e.astype(BF16)
            finish(h, weighted_values(h), jnp.sum(e, axis=0, keepdims=True))
            top = nxt


def _store_key_sq(ksq_ref, row, cols, keys_sq, dim_weights, n_ctx):
    n = cols.stop - cols.start
    for r, rows in ((row, slice(0, n_ctx)), (row + ksq_ref.shape[0] // 2, slice(None))):
        dim_max = jnp.max(keys_sq[rows], axis=0, keepdims=True)
        ksq_ref[r:r + 1, cols] = jnp.broadcast_to(jnp.sum(dim_max * dim_weights, axis=1, keepdims=True), (1, n))


def _diff_attn_kernel(lam_init, n_ctx, q_ref, k_ref, vt_ref, lam_ref, g_ref, o_ref, s_ref, e_ref, ksq_ref):
    lv = lam_ref[...]
    lam = (jnp.exp(jnp.sum(lv[0:1] * lv[1:2], axis=-1, keepdims=True))
           - jnp.exp(jnp.sum(lv[2:3] * lv[3:4], axis=-1, keepdims=True)) + lam_init)
    gain = g_ref[...] * (1.0 - lam_init)
    lane = lax.broadcasted_iota(jnp.int32, (DIFF_TQ, LANES), 1)

    @pl.when(pl.program_id(1) == 0)
    def _():
        first_head = (lax.broadcasted_iota(jnp.int32, (1, LANES), 1) < DIFF_QK_DIM).astype(F32)
        for h in range(DIFF_HEADS):
            kf = k_ref[0, :, h * LANES:(h + 1) * LANES].astype(F32)
            _store_key_sq(ksq_ref, h, slice(0, DIFF_TQ), kf * kf, first_head, n_ctx)
            _store_key_sq(ksq_ref, h, slice(DIFF_TQ, 2 * DIFF_TQ), kf * kf, 1.0 - first_head, n_ctx)

    q_tiles = DIFF_STEP_Q // DIFF_TQ

    def attend(nk, ctx_only):
        ksq_row = 0 if ctx_only else DIFF_HEADS

        def scores_operands(p):
            h, qi = divmod(p, q_tiles)
            cols = slice(h * LANES, (h + 1) * LANES)
            qp = q_ref[0, qi * DIFF_TQ:(qi + 1) * DIFF_TQ, cols]
            q_bd = jnp.concatenate(
                [jnp.where(lane < DIFF_QK_DIM, qp, 0), jnp.where(lane >= DIFF_QK_DIM, qp, 0)], axis=0)
            return k_ref.at[0, :nk, cols], q_bd

        def finish(p, o_t, l):
            h, qi = divmod(p, q_tiles)
            r = 1.0 / l
            o = o_t[:, :DIFF_TQ] * r[:, :DIFF_TQ] - o_t[:, DIFF_TQ:] * (lam * r[:, DIFF_TQ:])
            o = o * lax.rsqrt(jnp.mean(o * o, axis=0, keepdims=True) + RMS_EPS) * gain
            o_ref[0, qi * DIFF_TQ:(qi + 1) * DIFF_TQ, h * LANES:(h + 1) * LANES] = o.T.astype(BF16)

        def values_t(p):
            h = p // q_tiles
            return vt_ref.at[0, h * LANES:(h + 1) * LANES, :nk]

        def key_sq(p):
            r = ksq_row + p // q_tiles
            return ksq_ref[r:r + 1, :]

        _attend_t(DIFF_HEADS * q_tiles, nk, scores_operands, values_t, key_sq, s_ref, e_ref, finish)

    is_ctx = pl.program_id(1) < n_ctx // DIFF_STEP_Q

    @pl.when(is_ctx)
    def _():
        attend(n_ctx, True)

    @pl.when(jnp.logical_not(is_ctx))
    def _():
        attend(k_ref.shape[1], False)


def _diff_attn(q, k, vt, lam_vec, subln, lam_init, n_ctx):
    b, t, _ = k.shape
    per_sample = lambda shape: pl.BlockSpec(shape, lambda i, j: (i, 0, 0))
    return pl.pallas_call(
        functools.partial(_diff_attn_kernel, lam_init, n_ctx),
        grid=(b, t // DIFF_STEP_Q),
        in_specs=[
            pl.BlockSpec((1, DIFF_STEP_Q, DIFF_QK_WIDTH), lambda i, j: (i, j, 0)),
            per_sample((1, t, DIFF_QK_WIDTH)),
            per_sample((1, DIFF_WIDTH, t)),
            _const_spec((4, DIFF_QK_DIM)),
            _const_spec((DIFF_V_DIM, 1)),
        ],
        out_specs=pl.BlockSpec((1, DIFF_STEP_Q, DIFF_WIDTH), lambda i, j: (i, j, 0)),
        out_shape=jax.ShapeDtypeStruct((b, t, DIFF_WIDTH), BF16),
        scratch_shapes=[pltpu.VMEM((2, t, 2 * DIFF_TQ), F32), pltpu.VMEM((2, t, 2 * DIFF_TQ), BF16),
                        pltpu.VMEM((2 * DIFF_HEADS, 2 * DIFF_TQ), F32)],
        compiler_params=_cparams(2),
    )(q, k, vt, lam_vec, subln.reshape(DIFF_V_DIM, 1))


def _residual_and_prenorm(g, x_ref, y, mod_ref, ng_ref, xo_ref, h2_ref):
    xn = x_ref[g] + _rms(y, ng_ref[1:2, :] * mod_ref[g, 2:3, :])
    xo_ref[g] = xn
    h2 = _rms(xn, ng_ref[2:3, :] * (1.0 + mod_ref[g, 4:5, :])) + mod_ref[g, 3:4, :]
    h2_ref[g] = h2.astype(BF16)


def _even_out_kernel(split_ctx_tiles, t_off, seg_tiles, n_tiles, *refs):
    x_ref, (u_ref, up_ref, un_ref, a_ref, mod_ref, ng_ref, pw_ref, ps_ref, wo_ref, xo_ref, h2_ref,
            ext_ref) = _token_rows(split_ctx_tiles, t_off, refs)
    j = pl.program_id(1) + t_off
    prev_ok = jnp.logical_and(j != 0, j != seg_tiles)
    next_ok = jnp.logical_and(j != n_tiles - 1, j != seg_tiles - 1)
    lo = jnp.where(prev_ok, -HALO, 0)
    hi = jnp.where(next_ok, TILE + HALO, TILE)
    row = lax.broadcasted_iota(jnp.int32, (TILE, 1), 0)
    for g in range(x_ref.shape[0]):
        ext_ref[g, 0:HALO, :] = jnp.where(prev_ok, up_ref[g].astype(F32), 0.0)
        ext_ref[g, HALO:HALO + TILE, :] = u_ref[g].astype(F32)
        ext_ref[g, HALO + TILE:, :] = jnp.where(next_ok, un_ref[g].astype(F32), 0.0)
        parts = []
        for gidx, win in enumerate(POOL_WINDOWS):
            half = win // 2
            cols = slice(gidx * POOL_GROUP_DIM, (gidx + 1) * POOL_GROUP_DIM)
            acc = ext_ref[g, HALO - half:HALO - half + TILE, cols]
            for off in range(-half + 1, half):
                acc = acc + ext_ref[g, HALO + off:HALO + off + TILE, cols]
            cnt = jnp.minimum(row + half, hi) - jnp.maximum(row - half, lo)
            dlt = acc / cnt.astype(F32) - ext_ref[g, HALO:HALO + TILE, cols]
            parts.append(_dot(dlt.astype(BF16), pw_ref[gidx]))
        yp = jnp.concatenate(parts, axis=1) * ps_ref[...]
        y = _dot(yp.astype(BF16), wo_ref[:POOL_WIDTH, :]) + _dot(a_ref[g], wo_ref[POOL_WIDTH:, :])
        _residual_and_prenorm(g, x_ref, y, mod_ref, ng_ref, xo_ref, h2_ref)


def _even_out(x, x_ctx, u, att, mod_g, ng_l, pool_w, pool_scale, w_out, ctx_tiles, t_off, group):
    b, t, d = x.shape
    if x_ctx is not None:
        t += x_ctx.shape[1]
    nt = t // TILE
    nh = TILE // HALO
    n_out = nt - t_off

    def mod_map(i, j):
        return (jnp.where(j + t_off < ctx_tiles, b // group, i), 0, 0)

    tok = lambda w: pl.BlockSpec((group, TILE, w), lambda i, j: (i, j + t_off, 0))
    out_tok = lambda: pl.BlockSpec((group, TILE, d), lambda i, j: (i, j, 0))
    tokens, token_specs = _token_specs(x, x_ctx, group, ctx_tiles, t_off)
    return pl.pallas_call(
        functools.partial(_even_out_kernel, None if x_ctx is None else ctx_tiles, t_off, ctx_tiles, nt),
        grid=(b // group, n_out),
        in_specs=token_specs + [
            tok(POOL_WIDTH),
            pl.BlockSpec((group, HALO, POOL_WIDTH), lambda i, j: (i, jnp.maximum((j + t_off) * nh - 1, 0), 0)),
            pl.BlockSpec((group, HALO, POOL_WIDTH),
                         lambda i, j: (i, jnp.minimum((j + t_off + 1) * nh, nt * nh - 1), 0)),
            tok(DIFF_WIDTH),
            pl.BlockSpec((group, 6, d), mod_map),
            _const_spec((4, d)),
            _const_spec((POOL_GROUPS, POOL_GROUP_DIM, POOL_GROUP_DIM)),
            _const_spec((1, POOL_WIDTH)),
            _const_spec((POOL_WIDTH + DIFF_WIDTH, d)),
        ],
        out_specs=[out_tok(), out_tok()],
        out_shape=[
            jax.ShapeDtypeStruct((b, n_out * TILE, d), F32),
            jax.ShapeDtypeStruct((b, n_out * TILE, d), BF16),
        ],
        scratch_shapes=[pltpu.VMEM((group, TILE + 2 * HALO, POOL_WIDTH), F32)],
        compiler_params=_cparams(2),
    )(*tokens, u, u, u, att, mod_g, ng_l, pool_w, pool_scale.reshape(1, POOL_WIDTH), w_out)


def _ffn_kernel(seg_tiles, n_tiles, x_ref, h_ref, hp_ref, hn_ref, mod_ref, ng_ref, wg_ref, wu_ref,
                cw_ref, cb_ref, wd_ref, o_ref, a_ref, act_ref):
    j = pl.program_id(1)
    prev_ok = jnp.logical_and(j != 0, j != seg_tiles)
    next_ok = jnp.logical_and(j != n_tiles - 1, j != seg_tiles - 1)
    group = x_ref.shape[0]
    ext = TILE + 2 * HALO
    zero = jnp.zeros((HALO, h_ref.shape[2]), BF16)
    pieces = []
    for g in range(group):
        pieces += [jnp.where(prev_ok, hp_ref[g], zero), h_ref[g], jnp.where(next_ok, hn_ref[g], zero)]
    h_ext = jnp.concatenate(pieces, axis=0)
    h = jnp.concatenate([h_ref[g] for g in range(group)], axis=0) if group > 1 else h_ref[0]
    for c in range(D_FF // FF_CHUNK):
        cols = slice(c * FF_CHUNK, (c + 1) * FF_CHUNK)
        a_ref[...] = _dot(h_ext, wg_ref[:, cols])
        up = _dot(h, wu_ref[:, cols])
        for g in range(group):
            base = g * ext + HALO
            a = (a_ref[base - 1:base - 1 + TILE, :] * cw_ref[0:1, cols]
                 + a_ref[base:base + TILE, :] * cw_ref[1:2, cols]
                 + a_ref[base + 1:base + 1 + TILE, :] * cw_ref[2:3, cols] + cb_ref[:, cols])
            act_ref[g * TILE:(g + 1) * TILE, cols] = (_silu(a) * up[g * TILE:(g + 1) * TILE]).astype(BF16)
    for g in range(group):
        f = _dot(act_ref[g * TILE:(g + 1) * TILE, :], wd_ref[...])
        o_ref[g] = x_ref[g] + _rms(f, ng_ref[3:4, :] * mod_ref[g, 5:6, :])


def _ffn(x, h2, mod_g, ng_l, w_gate, w_up, conv_w, conv_b, w_down, ctx_tiles, group):
    b, t, d = x.shape
    nt = t // TILE
    nh = TILE // HALO

    def mod_map(i, j):
        return (jnp.where(j < ctx_tiles, b // group, i), 0, 0)

    tok = lambda: pl.BlockSpec((group, TILE, d), lambda i, j: (i, j, 0))
    return pl.pallas_call(
        functools.partial(_ffn_kernel, ctx_tiles, nt),
        grid=(b // group, nt),
        in_specs=[
            tok(),
            tok(),
            pl.BlockSpec((group, HALO, d), lambda i, j: (i, jnp.maximum(j * nh - 1, 0), 0)),
            pl.BlockSpec((group, HALO, d), lambda i, j: (i, jnp.minimum((j + 1) * nh, nt * nh - 1), 0)),
            pl.BlockSpec((group, 6, d), mod_map),
            _const_spec((4, d)),
            _const_spec((d, D_FF)),
            _const_spec((d, D_FF)),
            _const_spec((3, D_FF)),
            _const_spec((1, D_FF)),
            _const_spec((D_FF, d)),
        ],
        out_specs=tok(),
        out_shape=jax.ShapeDtypeStruct((b, t, d), F32),
        scratch_shapes=[pltpu.VMEM((group * (TILE + 2 * HALO), FF_CHUNK), F32),
                        pltpu.VMEM((group * TILE, D_FF), BF16)],
        compiler_params=_cparams(2),
    )(x, h2, h2, h2, mod_g, ng_l, w_gate, w_up, conv_w, conv_b.reshape(1, D_FF), w_down)


def _odd_in_kernel(layer, x_ref, mod_ref, ng_ref, cos_ref, sin_ref, wq_ref, wkv_ref, wkr_ref, wh_ref,
                   qn_ref, wuq_ref, kvn_ref, wukv_ref, lb_ref,
                   q_ref, kc_ref, vt_ref, hq_ref, hkf_ref, hkb_ref, lff_ref, lfb_ref, hv_ref, hg_ref):
    cos = cos_ref[...]
    sin = sin_ref[...]
    w = HG_QK_WIDTH
    lbs = []
    for direction in range(2):
        lrows = [lb_ref[direction, i:i + 1, :] for i in range(lb_ref.shape[1])]
        top = functools.reduce(jnp.maximum, lrows)
        ex = [jnp.exp(r - top) for r in lrows]
        lbs.append(sum(ex[1:layer + 1], jnp.zeros_like(top)) / sum(ex))

    for g in range(x_ref.shape[0]):
        hb = _prenorm(g, x_ref, mod_ref, ng_ref)
        cq = _rms(_dot(hb, wq_ref[...]), qn_ref[...])
        q = _dot(cq.astype(BF16), wuq_ref[...]) * ((MLA_NOPE_DIM + MLA_ROPE_DIM) ** -0.5 * LOG2E)
        ckv = _rms(_dot(hb, wkv_ref[...]), kvn_ref[...])
        kv = _dot(ckv.astype(BF16), wukv_ref[...])
        kr = _rope(_dot(hb, wkr_ref[...]), cos, sin).astype(BF16)
        for hd in range(MLA_HEADS):
            qb = hd * MLA_QK_PAD
            q_ref[g, :, qb:qb + MLA_NOPE_DIM] = q[:, qb:qb + MLA_NOPE_DIM].astype(BF16)
            q_ref[g, :, qb + MLA_NOPE_DIM:qb + MLA_QK_PAD] = _rope(
                q[:, qb + MLA_NOPE_DIM:qb + MLA_QK_PAD], cos, sin).astype(BF16)
            kb = hd * (MLA_NOPE_DIM + MLA_V_DIM)
            kc_ref[g, hd, :, :MLA_NOPE_DIM] = kv[:, kb:kb + MLA_NOPE_DIM].astype(BF16)
            kc_ref[g, hd, :, MLA_NOPE_DIM:] = kr
            vt_ref[g, hd * MLA_V_DIM:(hd + 1) * MLA_V_DIM, :] = kv[
                :, kb + MLA_NOPE_DIM:kb + MLA_NOPE_DIM + MLA_V_DIM].T.astype(BF16)

        hq_ref[g] = _silu(_dot(hb, wh_ref[:, :w])).astype(BF16)
        hv_ref[g] = _dot(hb, wh_ref[:, 3 * w:4 * w]).astype(BF16)
        hg_ref[g] = _silu(_dot(hb, wh_ref[:, 4 * w:])).astype(BF16)
        for direction, (k_ref, lf_ref) in enumerate(((hkf_ref, lff_ref), (hkb_ref, lfb_ref))):
            lb = lbs[direction]
            f = lb + (1.0 - lb) * jax.nn.sigmoid(_dot(hb, wh_ref[:, (1 + direction) * w:(2 + direction) * w]))
            k_ref[g] = (1.0 - f).astype(BF16)
            lf_ref[g] = jnp.log2(f)


def _odd_in(x, mod_g, ng_l, cos, sin, w_in, q_norm, w_uq, kv_norm, w_ukv, hgrn_lb, layer, ctx_tiles, group):
    b, t, d = x.shape
    nt = t // TILE
    o1 = MLA_Q_RANK
    o2 = o1 + MLA_KV_RANK
    o3 = o2 + MLA_ROPE_DIM
    w_q = w_in[:, :o1]
    w_kv = w_in[:, o1:o2]
    w_kr = jnp.pad(w_in[:, o2:o3], ((0, 0), (0, LANES - MLA_ROPE_DIM)))
    w_h = w_in[:, o3:]
    qk = MLA_NOPE_DIM + MLA_ROPE_DIM
    w_uq_pad = jnp.pad(w_uq.reshape(MLA_Q_RANK, MLA_HEADS, qk),
                       ((0, 0), (0, 0), (0, MLA_QK_PAD - qk))).reshape(MLA_Q_RANK, MLA_HEADS * MLA_QK_PAD)

    def mod_map(i, j):
        return (jnp.where(j < ctx_tiles, b // group, i), 0, 0)

    tok = lambda w: pl.BlockSpec((group, TILE, w), lambda i, j: (i, j, 0))
    sds = lambda w, dt: jax.ShapeDtypeStruct((b, t, w), dt)
    hw = HG_QK_WIDTH
    return pl.pallas_call(
        functools.partial(_odd_in_kernel, layer),
        grid=(b // group, nt),
        in_specs=[
            tok(d),
            pl.BlockSpec((group, 6, d), mod_map),
            _const_spec((4, d)),
            pl.BlockSpec((TILE, LANES), lambda i, j: (j, 0)),
            pl.BlockSpec((TILE, LANES), lambda i, j: (j, 0)),
            _const_spec(w_q.shape),
            _const_spec(w_kv.shape),
            _const_spec(w_kr.shape),
            _const_spec(w_h.shape),
            _const_spec((1, MLA_Q_RANK)),
            _const_spec(w_uq_pad.shape),
            _const_spec((1, MLA_KV_RANK)),
            _const_spec(w_ukv.shape),
            _const_spec(hgrn_lb.shape),
        ],
        out_specs=[
            tok(MLA_HEADS * MLA_QK_PAD),
            pl.BlockSpec((group, MLA_HEADS, TILE, MLA_QK_PAD), lambda i, j: (i, 0, j, 0)),
            pl.BlockSpec((group, MLA_WIDTH, TILE), lambda i, j: (i, 0, j)),
            tok(hw), tok(hw), tok(hw), tok(hw), tok(hw), tok(hw), tok(hw),
        ],
        out_shape=[
            sds(MLA_HEADS * MLA_QK_PAD, BF16),
            jax.ShapeDtypeStruct((b, MLA_HEADS, t, MLA_QK_PAD), BF16),
            jax.ShapeDtypeStruct((b, MLA_WIDTH, t), BF16),
            sds(hw, BF16), sds(hw, BF16), sds(hw, BF16), sds(hw, F32), sds(hw, F32), sds(hw, BF16),
            sds(hw, BF16),
        ],
        compiler_params=_cparams(2),
    )(x, mod_g, ng_l, cos, sin, w_q, w_kv, w_kr, w_h, q_norm.reshape(1, -1), w_uq_pad,
      kv_norm.reshape(1, -1), w_ukv, hgrn_lb)


def _mla_attn_kernel(n_ctx, q_off, q_tiles, *refs):
    q_refs = refs[:q_tiles]
    kc_ref, vt_ref, o_ref, s_ref, e_ref, ksq_ref = refs[q_tiles:]

    @pl.when(pl.program_id(1) == 0)
    def _():
        all_dims = jnp.ones((1, MLA_QK_PAD), F32)
        for h in range(MLA_HEADS):
            kf = kc_ref[0, h].astype(F32)
            _store_key_sq(ksq_ref, h, slice(0, MLA_TQ), kf * kf, all_dims, n_ctx)

    def attend(nk, ctx_only):
        ksq_row = 0 if ctx_only else MLA_HEADS

        def finish(p, o_t, l):
            h, qi = divmod(p, q_tiles)
            o_ref[0, qi * MLA_TQ:(qi + 1) * MLA_TQ, h * MLA_V_DIM:(h + 1) * MLA_V_DIM] = (
                o_t * (1.0 / l)).T.astype(BF16)

        def scores_operands(p):
            h, qi = divmod(p, q_tiles)
            return kc_ref.at[0, h, :nk, :], q_refs[qi][0, :, h * MLA_QK_PAD:(h + 1) * MLA_QK_PAD]

        def values_t(p):
            h = p // q_tiles
            return vt_ref.at[0, h * MLA_V_DIM:(h + 1) * MLA_V_DIM, :nk]

        def key_sq(p):
            r = ksq_row + p // q_tiles
            return ksq_ref[r:r + 1, :]

        _attend_t(MLA_HEADS * q_tiles, nk, scores_operands, values_t, key_sq, s_ref, e_ref, finish)

    if q_off * MLA_TQ >= n_ctx:
        attend(kc_ref.shape[2], False)
    else:
        is_ctx = pl.program_id(1) + q_off < n_ctx // MLA_TQ

        @pl.when(is_ctx)
        def _():
            attend(n_ctx, True)

        @pl.when(jnp.logical_not(is_ctx))
        def _():
            attend(kc_ref.shape[2], False)


def _mla_attn(q, kc, vt, n_ctx, q_off):
    b, _, t, _ = kc.shape
    nq = t // MLA_TQ - q_off
    q_tiles = MLA_STEP_TILES if (q_off * MLA_TQ >= n_ctx and nq % MLA_STEP_TILES == 0) else 1

    def q_spec(qi):
        return pl.BlockSpec((1, MLA_TQ, MLA_HEADS * MLA_QK_PAD), lambda i, j: (i, j * q_tiles + qi + q_off, 0))

    return pl.pallas_call(
        functools.partial(_mla_attn_kernel, n_ctx, q_off, q_tiles),
        grid=(b, nq // q_tiles),
        in_specs=[q_spec(qi) for qi in range(q_tiles)] + [
            pl.BlockSpec((1, MLA_HEADS, t, MLA_QK_PAD), lambda i, j: (i, 0, 0, 0)),
            pl.BlockSpec((1, MLA_WIDTH, t), lambda i, j: (i, 0, 0)),
        ],
        out_specs=pl.BlockSpec((1, q_tiles * MLA_TQ, MLA_WIDTH), lambda i, j: (i, j, 0)),
        out_shape=jax.ShapeDtypeStruct((b, nq * MLA_TQ, MLA_WIDTH), BF16),
        scratch_shapes=[pltpu.VMEM((2, t, MLA_TQ), F32), pltpu.VMEM((2, t, MLA_TQ), BF16),
                        pltpu.VMEM((2 * MLA_HEADS, MLA_TQ), F32)],
        compiler_params=_cparams(2),
    )(*([q] * q_tiles), kc, vt)


def _hgrn_triangle(reverse):
    t = np.arange(HG_CHUNK)[:, None]
    j = np.arange(HG_CHUNK)[None, :]
    tri = ((j >= t) if reverse else (j <= t)).astype(np.float32)
    return np.kron(np.eye(HG_STACK, dtype=np.float32), tri)


def _hgrn_masks(reverse):
    c = HG_STACK * HG_CHUNK
    t = lax.broadcasted_iota(jnp.int32, (c, c), 0)
    s = lax.broadcasted_iota(jnp.int32, (c, c), 1)
    tq = lax.broadcasted_iota(jnp.int32, (c, 1), 0)
    masks = []
    for m in HG_LEVELS:
        half = m // 2
        shift = int(math.log2(m))
        t_hi = (t & (m - 1)) >= half
        s_hi = (s & (m - 1)) >= half
        same = (t >> shift) == (s >> shift)
        if not reverse:
            pair = jnp.where(same, jnp.where(t_hi, jnp.where(s_hi, 0.0, 1.0), 0.0), 0.0)
            is_q = (tq & (m - 1)) >= half
        else:
            pair = jnp.where(same, jnp.where(t_hi, 0.0, jnp.where(s_hi, 1.0, 0.0)), 0.0)
            is_q = (tq & (m - 1)) < half
        masks.append((pair > 0.5, is_q))
    return masks, t == s


def _hgrn_reference_rows(cum, m, reverse):
    c = cum.shape[0]
    half = m // 2
    ref_in_block = half if reverse else half - 1
    if m >= 2 * SUBLANES:
        return jnp.concatenate(
            [jnp.broadcast_to(cum[b0 + ref_in_block:b0 + ref_in_block + 1, :], (m, LANES))
             for b0 in range(0, c, m)], axis=0)
    cum3 = cum.reshape(c // SUBLANES, SUBLANES, LANES)
    r = lax.broadcasted_iota(jnp.int32, cum3.shape, 1)
    out = None
    for b0 in reversed(range(0, SUBLANES, m)):
        pick = jnp.broadcast_to(cum3[:, b0 + ref_in_block:b0 + ref_in_block + 1, :], cum3.shape)
        out = pick if out is None else jnp.where(r < b0 + m, pick, out)
    return out.reshape(c, LANES)


def _hgrn_scan_body(reverse, tri_ref, lf_ref, q_ref, k_ref, v_ref, st_ref, emit):
    masks, eye = _hgrn_masks(reverse)
    tri = tri_ref[...]
    c = HG_CHUNK
    pr = HG_STACK * c
    n_stacks = HG_BLOCK // pr
    back_to_front = lambda n: list(range(n - 1, -1, -1) if reverse else range(n))
    group = lf_ref.shape[0]
    probs = [(g, si, h) for si in back_to_front(n_stacks) for g in range(group) for h in range(HG_HEADS)]
    where = lambda g, si, h: (g, slice(si * pr, (si + 1) * pr), slice(h * HG_K_DIM, (h + 1) * HG_K_DIM))
    sub = lambda j: slice(j * c, (j + 1) * c)

    @pl.when(pl.program_id(1) == 0)
    def _():
        st_ref[...] = jnp.zeros_like(st_ref)

    cums, lfs = [], []
    for p in probs:
        g, rows, cols = where(*p)
        lf = lf_ref[g, rows, cols]
        lfs.append(lf)
        hi = lf.astype(BF16)
        r1 = lf - hi.astype(F32)
        mid = r1.astype(BF16)
        lo = (r1 - mid.astype(F32)).astype(BF16)
        c3 = _dot(tri, jnp.concatenate([hi, mid, lo], axis=1))
        cums.append(c3[:, :LANES] + c3[:, LANES:2 * LANES] + c3[:, 2 * LANES:])

    qf, kf, qe, ks, decay, scores = [], [], [], [], [], []
    for p, cum in zip(probs, cums):
        g, rows, cols = where(*p)
        totals = [cum[j * c:j * c + 1] if reverse else cum[(j + 1) * c - 1:(j + 1) * c] for j in range(HG_STACK)]
        total_rows = jnp.concatenate([jnp.broadcast_to(tj, (c, LANES)) for tj in totals], axis=0)
        q = q_ref[g, rows, cols]
        k = k_ref[g, rows, cols]
        qf.append(q.astype(F32))
        kf.append(k.astype(F32))
        qe.append((qf[-1] * jnp.exp2(cum)).astype(BF16))
        ks.append((kf[-1] * jnp.exp2(total_rows - cum)).astype(BF16))
        decay.append([jnp.exp2(tj) for tj in totals])
        scores.append(jnp.where(eye, _dot_nt(q, k), 0.0))

    for m, (pair, is_q) in zip(HG_LEVELS, masks):
        for i, cum in enumerate(cums):
            if m == 2:
                x = jnp.where(is_q, qf[i] * jnp.exp2(lfs[i]), kf[i]).astype(BF16)
            else:
                d = cum - _hgrn_reference_rows(cum, m, reverse)
                x = (jnp.where(is_q, qf[i], kf[i]) * jnp.exp2(-jnp.abs(d))).astype(BF16)
            scores[i] = jnp.where(pair, _dot_nt(x, x), scores[i])

    intra, update = [], []
    for i, p in enumerate(probs):
        g, rows, cols = where(*p)
        v = v_ref[g, rows, cols]
        intra.append(_dot(scores[i].astype(BF16), v))
        v_t = v.astype(F32).T.astype(BF16)
        update.append([_dot(v_t[:, sub(j)], ks[i][sub(j), :]) for j in range(HG_STACK)])

    for gh in range(group * HG_HEADS):
        st = st_ref[gh]
        for n in range(n_stacks):
            i = n * group * HG_HEADS + gh
            g, rows, cols = where(*probs[i])
            for j in back_to_front(HG_STACK):
                out_rows = slice(rows.start + j * c, rows.start + (j + 1) * c)
                emit(g, out_rows, cols, intra[i][sub(j), :] + _dot_nt(qe[i][sub(j), :], st.astype(BF16)))
                st = st * decay[i][j] + update[i][j]
        st_ref[gh] = st


def _hgrn_fwd_kernel(tri_ref, lf_ref, q_ref, k_ref, v_ref, o_ref, st_ref):
    def emit(g, rows, cols, o):
        o_ref[g, rows, cols] = o

    _hgrn_scan_body(False, tri_ref, lf_ref, q_ref, k_ref, v_ref, st_ref, emit)


def _hgrn_bwd_kernel(tri_ref, lf_ref, q_ref, k_ref, v_ref, of_ref, gate_ref, gn_ref, o_ref, st_ref):
    def emit(g, rows, cols, o):
        o_ref[g, rows, cols] = (_rms(o + of_ref[g, rows, cols], gn_ref[...])
                                * gate_ref[g, rows, cols].astype(F32)).astype(BF16)

    _hgrn_scan_body(True, tri_ref, lf_ref, q_ref, k_ref, v_ref, st_ref, emit)


def _hgrn_scan(reverse, lf, q, k, v, ctx_blocks, extra=()):
    b, t, w = lf.shape
    nb = t // HG_BLOCK
    tri = jnp.asarray(_hgrn_triangle(reverse), BF16)
    if reverse:
        blk = lambda j: jnp.where(j < ctx_blocks, ctx_blocks - 1 - j, nb - 1 - (j - ctx_blocks))
    else:
        blk = lambda j: j
    group = _group_size(b, HG_GROUP)
    tok = pl.BlockSpec((group, HG_BLOCK, w), lambda i, j: (i, blk(j), 0))
    in_specs = [_const_spec(tri.shape), tok, tok, tok, tok]
    args = [tri, lf, q, k, v]
    if reverse:
        o_f, gate, gnorm = extra
        in_specs += [tok, tok, _const_spec((1, HG_V_DIM))]
        args += [o_f, gate, gnorm.reshape(1, HG_V_DIM)]
    return pl.pallas_call(
        _hgrn_bwd_kernel if reverse else _hgrn_fwd_kernel,
        grid=(b // group, nb),
        in_specs=in_specs,
        out_specs=tok,
        out_shape=jax.ShapeDtypeStruct((b, t, w), BF16 if reverse else F32),
        scratch_shapes=[pltpu.VMEM((group * HG_HEADS, HG_V_DIM, HG_K_DIM), F32)],
        compiler_params=_cparams(2),
    )(*args)


def _odd_out_kernel(x_ref, a_ref, g_ref, mod_ref, ng_ref, wo_ref, xo_ref, h2_ref):
    for g in range(x_ref.shape[0]):
        y = _dot(a_ref[g], wo_ref[:MLA_WIDTH, :]) + _dot(g_ref[g], wo_ref[MLA_WIDTH:, :])
        _residual_and_prenorm(g, x_ref, y, mod_ref, ng_ref, xo_ref, h2_ref)


def _odd_out(x, att, hg, mod_g, ng_l, w_out, ctx_tiles, t_off, group):
    b, t, d = x.shape
    n_out = t // TILE - t_off

    def mod_map(i, j):
        return (jnp.where(j + t_off < ctx_tiles, b // group, i), 0, 0)

    out_tok = lambda w: pl.BlockSpec((group, TILE, w), lambda i, j: (i, j, 0))
    return pl.pallas_call(
        _odd_out_kernel,
        grid=(b // group, n_out),
        in_specs=[
            pl.BlockSpec((group, TILE, d), lambda i, j: (i, j + t_off, 0)),
            out_tok(MLA_WIDTH),
            pl.BlockSpec((group, TILE, HG_WIDTH), lambda i, j: (i, j + t_off, 0)),
            pl.BlockSpec((group, 6, d), mod_map),
            _const_spec((4, d)),
            _const_spec((MLA_WIDTH + HG_WIDTH, d)),
        ],
        out_specs=[out_tok(d), out_tok(d)],
        out_shape=[
            jax.ShapeDtypeStruct((b, n_out * TILE, d), F32),
            jax.ShapeDtypeStruct((b, n_out * TILE, d), BF16),
        ],
        compiler_params=_cparams(2),
    )(x, att, hg, mod_g, ng_l, w_out)


def _rope_tables(n_ctx, n_lat):
    rows = n_lat // GRID_W
    pos = jnp.stack([jnp.repeat(jnp.arange(rows), GRID_W), jnp.tile(jnp.arange(GRID_W), rows)], axis=-1)
    axis_dim = DIFF_QK_DIM // 2
    inv_freq = ROPE_THETA ** (-jnp.arange(0, axis_dim, 2, dtype=F32) / axis_dim)
    ang = pos.astype(F32)[..., None] * inv_freq
    cos = jnp.cos(ang)
    sin = jnp.sin(ang)
    cos64 = jnp.concatenate([cos[:, 0], cos[:, 0], cos[:, 1], cos[:, 1]], axis=-1)
    sin64 = jnp.concatenate([-sin[:, 0], sin[:, 0], -sin[:, 1], sin[:, 1]], axis=-1)
    cos64 = jnp.concatenate([jnp.ones((n_ctx, 64), F32), cos64], axis=0)
    sin64 = jnp.concatenate([jnp.zeros((n_ctx, 64), F32), sin64], axis=0)
    return cos64, sin64


def kernel(x, c, ctx, c_ctx, ada_w, ada_b, norm_g, mix_w_out, ffn_w_gate, ffn_w_up, ffn_conv_w, ffn_conv_b,
           ffn_w_down, ev_w_in, pool_w, pool_scale, diff_lambda, diff_subln, od_w_in, mla_q_norm, mla_w_uq,
           mla_kv_norm, mla_w_ukv, hgrn_norm, hgrn_lb):
    b, n_lat, d = x.shape
    n_ctx = ctx.shape[1]
    depth = ada_w.shape[0]
    assert d == D_MODEL and n_ctx % TILE == 0 and n_lat % TILE == 0 and n_lat % GRID_W == 0
    ctx_tiles = n_ctx // TILE
    group = _group_size(b, MAX_GROUP)
    group_odd_in = _group_size(b, ODD_IN_GROUP)

    rows = -(-(b + 1) // SUBLANES) * SUBLANES
    cond = jnp.concatenate([c, c_ctx[None, :], jnp.zeros((rows - b - 1, d), F32)], axis=0)
    mod = _modulation(cond, ada_w, ada_b).reshape(depth, rows, 6, d)

    cos64, sin64 = _rope_tables(n_ctx, n_lat)
    cos_diff = jnp.tile(cos64, (1, 2 * DIFF_HEADS))
    sin_diff = jnp.tile(sin64, (1, 2 * DIFF_HEADS))
    pad = ((0, 0), (0, LANES - MLA_ROPE_DIM))
    cos_mla = jnp.pad(cos64, pad, constant_values=1.0)
    sin_mla = jnp.pad(sin64, pad)

    xs, xs_ctx = x, ctx
    for layer in range(depth):
        last = layer == depth - 1
        j = layer // 2
        t_off = ctx_tiles if last else 0
        mod_g = _group_mod(mod[layer], b, group)
        ng_l = norm_g[layer]
        w_out = mix_w_out[layer].astype(BF16)
        if layer % 2 == 0:
            lam_init = 0.8 - 0.6 * math.exp(-0.3 * layer)
            u, qt, k, vt = _even_in(xs, xs_ctx, mod_g, ng_l, cos_diff, sin_diff, ev_w_in[j].astype(BF16),
                                    ctx_tiles, group)
            att = _diff_attn(qt, k, vt, diff_lambda[j], diff_subln[j], lam_init, n_ctx)
            x_mid, h2 = _even_out(xs, xs_ctx, u, att, mod_g, ng_l, pool_w[j].astype(BF16), pool_scale[j], w_out,
                                  ctx_tiles, t_off, group)
        else:
            assert xs_ctx is None
            (qt, kc, vt, hq, hk_f, hk_b, lf_f, lf_b, hv, hgate) = _odd_in(
                xs, _group_mod(mod[layer], b, group_odd_in), ng_l, cos_mla, sin_mla, od_w_in[j].astype(BF16),
                mla_q_norm[j], mla_w_uq[j].astype(BF16), mla_kv_norm[j], mla_w_ukv[j].astype(BF16), hgrn_lb,
                layer, ctx_tiles, group_odd_in)
            att = _mla_attn(qt, kc, vt, n_ctx, t_off * (TILE // MLA_TQ))
            o_f = _hgrn_scan(False, lf_f, hq, hk_f, hv, n_ctx // HG_BLOCK)
            hg = _hgrn_scan(True, lf_b, hq, hk_b, hv, n_ctx // HG_BLOCK, (o_f, hgate, hgrn_norm[j]))
            x_mid, h2 = _odd_out(xs, att, hg, mod_g, ng_l, w_out, ctx_tiles, t_off, group)
        xs = _ffn(x_mid, h2, mod_g, ng_l, ffn_w_gate[layer].astype(BF16), ffn_w_up[layer].astype(BF16),
                  ffn_conv_w[layer], ffn_conv_b[layer], ffn_w_down[layer].astype(BF16),
                  0 if last else ctx_tiles, group)
        xs_ctx = None
    return xs
```

```python
import functools
import math

import jax
import jax.numpy as jnp
import numpy as np
from jax import lax
from jax.experimental import pallas as pl
from jax.experimental.pallas import tpu as pltpu

F32 = jnp.float32
BF16 = jnp.bfloat16

D_MODEL = 1024
GRID_W = 64
RMS_EPS = 1e-6
ROPE_THETA = 10000.0
LOG2E = math.log2(math.e)

POOL_WINDOWS = (2, 4, 8, 16)
POOL_GROUPS = 4
POOL_WIDTH = 512
POOL_GROUP_DIM = POOL_WIDTH // POOL_GROUPS

DIFF_HEADS = 4
DIFF_QK_DIM = 64
DIFF_V_DIM = 128
DIFF_QK_WIDTH = 2 * DIFF_HEADS * DIFF_QK_DIM
DIFF_WIDTH = DIFF_HEADS * DIFF_V_DIM

MLA_HEADS = 4
MLA_Q_RANK = 512
MLA_KV_RANK = 256
MLA_NOPE_DIM = 128
MLA_ROPE_DIM = 64
MLA_V_DIM = 128
MLA_QK_PAD = 256
MLA_WIDTH = MLA_HEADS * MLA_V_DIM

HG_HEADS = 4
HG_K_DIM = 128
HG_V_DIM = 128
HG_QK_WIDTH = HG_HEADS * HG_K_DIM
HG_WIDTH = HG_HEADS * HG_V_DIM
HG_CHUNK = 64
HG_LEVELS = (64, 32, 16, 8, 4, 2)

D_FF = 2816
FF_CHUNK = 256

LANES = 128
SUBLANES = 8
BF16_SUBLANES = 16
MXU_TILE = 256

TILE = 256
MAX_GROUP = 4
ODD_IN_GROUP = 4
DIFF_TQ = 128
DIFF_STEP_Q = 256
MLA_TQ = 256
MLA_STEP_TILES = 2
HG_BLOCK = 256
HG_GROUP = 2
HG_STACK = 2
HALO = BF16_SUBLANES
VMEM_LIMIT = 56 * 1024 * 1024
SOFTMAX_DENOM_FLOOR = 2.0 ** -100
SCORE_BOUND_SLACK = 1.0 + 2.0 ** -6


def _cparams(n_axes):
    return pltpu.CompilerParams(
        dimension_semantics=("arbitrary",) * n_axes, vmem_limit_bytes=VMEM_LIMIT)


def _dot(a, b):
    return jnp.dot(a, b, preferred_element_type=F32)


def _dot_nt(a, b):
    return lax.dot_general(a, b, (((1,), (1,)), ((), ())), preferred_element_type=F32)


def _rms(x, g):
    return x * lax.rsqrt(jnp.mean(x * x, axis=-1, keepdims=True) + RMS_EPS) * g


def _silu(x):
    return x * jax.nn.sigmoid(x)


def _const_spec(shape):
    zeros = (0,) * len(shape)
    return pl.BlockSpec(shape, lambda *_: zeros, pipeline_mode=pl.Buffered(1))


def _group_size(b, max_group):
    return max(g for g in range(1, max_group + 1) if b % g == 0)


def _group_mod(mod_l, b, group):
    return jnp.concatenate([mod_l[:b], jnp.broadcast_to(mod_l[b:b + 1], (group,) + mod_l.shape[1:])], axis=0)


def _rope(x, cos, sin_signed):
    n = x.shape[-1]
    lane = lax.broadcasted_iota(jnp.int32, x.shape, 1)
    first_half = (lane & 31) < 16
    partner = jnp.where(first_half, pltpu.roll(x, n - 16, 1), pltpu.roll(x, 16, 1))
    return x * cos + partner * sin_signed


def _prenorm(g, x_ref, mod_ref, ng_ref):
    h = _rms(x_ref[g], ng_ref[0:1, :] * (1.0 + mod_ref[g, 1:2, :])) + mod_ref[g, 0:1, :]
    return h.astype(BF16)


def _mod_kernel(c_ref, w_ref, b_ref, o_ref):
    s = _silu(c_ref[...])
    w = w_ref[0]
    s_hi = s.astype(BF16)
    s_lo = (s - s_hi.astype(F32)).astype(BF16)
    w_hi = w.astype(BF16)
    w_lo = (w - w_hi.astype(F32)).astype(BF16)
    o_ref[0] = _dot(s_hi, w_hi) + _dot(s_hi, w_lo) + _dot(s_lo, w_hi) + b_ref[0]


def _modulation(cond, ada_w, ada_b):
    depth, d, n = ada_w.shape
    rows = cond.shape[0]
    tn = 1536
    return pl.pallas_call(
        _mod_kernel,
        grid=(depth, n // tn),
        in_specs=[
            pl.BlockSpec((rows, d), lambda l, j: (0, 0)),
            pl.BlockSpec((1, d, tn), lambda l, j: (l, 0, j)),
            pl.BlockSpec((1, 1, tn), lambda l, j: (l, 0, j)),
        ],
        out_specs=pl.BlockSpec((1, rows, tn), lambda l, j: (l, 0, j)),
        out_shape=jax.ShapeDtypeStruct((depth, rows, n), F32),
        compiler_params=_cparams(2),
    )(cond, ada_w, ada_b.reshape(depth, 1, n))


class _SplitRows:
    def __init__(self, ctx_ref, lat_ref, is_ctx):
        self.ctx_ref, self.lat_ref, self.is_ctx = ctx_ref, lat_ref, is_ctx
        self.shape = lat_ref.shape

    def __getitem__(self, g):
        return jnp.where(self.is_ctx, self.ctx_ref[g], self.lat_ref[g])


def _token_rows(split_ctx_tiles, t_off, refs):
    if split_ctx_tiles is None:
        return refs[0], refs[1:]
    return _SplitRows(refs[0], refs[1], pl.program_id(1) + t_off < split_ctx_tiles), refs[2:]


def _token_specs(x, x_ctx, group, ctx_tiles, t_off):
    d = x.shape[2]
    if x_ctx is None:
        return [x], [pl.BlockSpec((group, TILE, d), lambda i, j: (i, j + t_off, 0))]
    return [x_ctx, x], [
        pl.BlockSpec((group, TILE, d), lambda i, j: (i, jnp.minimum(j + t_off, ctx_tiles - 1), 0)),
        pl.BlockSpec((group, TILE, d), lambda i, j: (i, jnp.maximum(j + t_off - ctx_tiles, 0), 0)),
    ]


def _even_in_kernel(split_ctx_tiles, *refs):
    x_ref, (mod_ref, ng_ref, cos_ref, sin_ref, w_ref, u_ref, q_ref, k_ref, vt_ref) = _token_rows(
        split_ctx_tiles, 0, refs)
    cos = cos_ref[...]
    sin = sin_ref[...]
    o_q = POOL_WIDTH
    o_k = o_q + DIFF_QK_WIDTH
    o_v = o_k + DIFF_QK_WIDTH
    for g in range(x_ref.shape[0]):
        p = _dot(_prenorm(g, x_ref, mod_ref, ng_ref), w_ref[...])
        u_ref[g] = p[:, :o_q].astype(BF16)
        q_ref[g] = (_rope(p[:, o_q:o_k], cos, sin) * (DIFF_QK_DIM ** -0.5 * LOG2E)).astype(BF16)
        k_ref[g] = _rope(p[:, o_k:o_v], cos, sin).astype(BF16)
        vt_ref[g] = p[:, o_v:].T.astype(BF16)


def _even_in(x, x_ctx, mod_g, ng_l, cos, sin, w_in, ctx_tiles, group):
    b, t, d = x.shape
    if x_ctx is not None:
        t += x_ctx.shape[1]
    nt = t // TILE
    n_in = w_in.shape[1]

    def mod_map(i, j):
        return (jnp.where(j < ctx_tiles, b // group, i), 0, 0)

    tok = lambda w: pl.BlockSpec((group, TILE, w), lambda i, j: (i, j, 0))
    tok_t = lambda w: pl.BlockSpec((group, w, TILE), lambda i, j: (i, 0, j))
    tokens, token_specs = _token_specs(x, x_ctx, group, ctx_tiles, 0)
    return pl.pallas_call(
        functools.partial(_even_in_kernel, None if x_ctx is None else ctx_tiles),
        grid=(b // group, nt),
        in_specs=token_specs + [
            pl.BlockSpec((group, 6, d), mod_map),
            _const_spec((4, d)),
            pl.BlockSpec((TILE, DIFF_QK_WIDTH), lambda i, j: (j, 0)),
            pl.BlockSpec((TILE, DIFF_QK_WIDTH), lambda i, j: (j, 0)),
            _const_spec((d, n_in)),
        ],
        out_specs=[tok(POOL_WIDTH), tok(DIFF_QK_WIDTH), tok(DIFF_QK_WIDTH), tok_t(DIFF_WIDTH)],
        out_shape=[
            jax.ShapeDtypeStruct((b, t, POOL_WIDTH), BF16),
            jax.ShapeDtypeStruct((b, t, DIFF_QK_WIDTH), BF16),
            jax.ShapeDtypeStruct((b, t, DIFF_QK_WIDTH), BF16),
            jax.ShapeDtypeStruct((b, DIFF_WIDTH, t), BF16),
        ],
        compiler_params=_cparams(2),
    )(*tokens, mod_g, ng_l, cos, sin, w_in)


def _key_halves(nk, align):
    if nk <= MXU_TILE:
        return ((0, nk),)
    first = -(-(nk // 2) // align) * align
    return ((0, first), (first, nk))


def _attend_t(n_heads, nk, scores_operands, values_t, key_sq, s_ref, e_ref, finish):
    halves = _key_halves(nk, LANES)
    value_halves = _key_halves(nk, MXU_TILE)

    def weighted_values(h):
        vt = values_t(h)
        return sum(_dot(vt[:, a:b], e_ref[h % 2, a:b, :]) for a, b in value_halves)

    l_low = None
    for h in range(n_heads):
        keys, q = scores_operands(h)
        q_sq = (q.astype(F32) * q.astype(F32)).astype(BF16)
        q_sq = _dot_nt(jnp.ones((SUBLANES, q.shape[1]), BF16), q_sq)[0:1]
        bound = jnp.sqrt(q_sq * key_sq(h)) * SCORE_BOUND_SLACK
        l = None
        for a, b in halves:
            e = jnp.exp2(_dot_nt(keys[a:b, :], q) - bound)
            e_ref[h % 2, a:b, :] = e.astype(BF16)
            part = jnp.sum(e, axis=0, keepdims=True)
            l = part if l is None else l + part
        finish(h, weighted_values(h), l)
        l_low = l if l_low is None else jnp.minimum(l_low, l)
    accurate = jnp.min(l_low) > SOFTMAX_DENOM_FLOOR

    @pl.when(jnp.logical_not(accurate))
    def _():
        def scores(h):
            keys, q = scores_operands(h)
            tops = []
            for a, b in halves:
                s = _dot_nt(keys[a:b, :], q)
                s_ref[h % 2, a:b, :] = s
                tops.append(jnp.max(s, axis=0, keepdims=True))
            return functools.reduce(jnp.maximum, tops)

        top = scores(0)
        for h in range(n_heads):
            nxt = scores(h + 1) if h + 1 < n_heads else None
            e = jnp.exp2(s_ref[h % 2, :nk, :] - top)
            e_ref[h % 2, :nk, :] = e.astype(BF16)
            finish(h, weighted_values(h), jnp.sum(e, axis=0, keepdims=True))
            top = nxt


def _store_key_sq(ksq_ref, row, cols, keys_sq, dim_weights, n_ctx):
    n = cols.stop - cols.start
    for r, rows in ((row, slice(0, n_ctx)), (row + ksq_ref.shape[0] // 2, slice(None))):
        dim_max = jnp.max(keys_sq[rows], axis=0, keepdims=True)
        ksq_ref[r:r + 1, cols] = jnp.broadcast_to(jnp.sum(dim_max * dim_weights, axis=1, keepdims=True), (1, n))


def _diff_attn_kernel(lam_init, n_ctx, q_ref, k_ref, vt_ref, lam_ref, g_ref, o_ref, s_ref, e_ref, ksq_ref):
    lv = lam_ref[...]
    lam = (jnp.exp(jnp.sum(lv[0:1] * lv[1:2], axis=-1, keepdims=True))
           - jnp.exp(jnp.sum(lv[2:3] * lv[3:4], axis=-1, keepdims=True)) + lam_init)
    gain = g_ref[...] * (1.0 - lam_init)
    lane = lax.broadcasted_iota(jnp.int32, (DIFF_TQ, LANES), 1)

    @pl.when(pl.program_id(1) == 0)
    def _():
        first_head = (lax.broadcasted_iota(jnp.int32, (1, LANES), 1) < DIFF_QK_DIM).astype(F32)
        for h in range(DIFF_HEADS):
            kf = k_ref[0, :, h * LANES:(h + 1) * LANES].astype(F32)
            _store_key_sq(ksq_ref, h, slice(0, DIFF_TQ), kf * kf, first_head, n_ctx)
            _store_key_sq(ksq_ref, h, slice(DIFF_TQ, 2 * DIFF_TQ), kf * kf, 1.0 - first_head, n_ctx)

    q_tiles = DIFF_STEP_Q // DIFF_TQ

    def attend(nk, ctx_only):
        ksq_row = 0 if ctx_only else DIFF_HEADS

        def scores_operands(p):
            h, qi = divmod(p, q_tiles)
            cols = slice(h * LANES, (h + 1) * LANES)
            qp = q_ref[0, qi * DIFF_TQ:(qi + 1) * DIFF_TQ, cols]
            q_bd = jnp.concatenate(
                [jnp.where(lane < DIFF_QK_DIM, qp, 0), jnp.where(lane >= DIFF_QK_DIM, qp, 0)], axis=0)
            return k_ref.at[0, :nk, cols], q_bd

        def finish(p, o_t, l):
            h, qi = divmod(p, q_tiles)
            r = 1.0 / l
            o = o_t[:, :DIFF_TQ] * r[:, :DIFF_TQ] - o_t[:, DIFF_TQ:] * (lam * r[:, DIFF_TQ:])
            o = o * lax.rsqrt(jnp.mean(o * o, axis=0, keepdims=True) + RMS_EPS) * gain
            o_ref[0, qi * DIFF_TQ:(qi + 1) * DIFF_TQ, h * LANES:(h + 1) * LANES] = o.T.astype(BF16)

        def values_t(p):
            h = p // q_tiles
            return vt_ref.at[0, h * LANES:(h + 1) * LANES, :nk]

        def key_sq(p):
            r = ksq_row + p // q_tiles
            return ksq_ref[r:r + 1, :]

        _attend_t(DIFF_HEADS * q_tiles, nk, scores_operands, values_t, key_sq, s_ref, e_ref, finish)

    is_ctx = pl.program_id(1) < n_ctx // DIFF_STEP_Q

    @pl.when(is_ctx)
    def _():
        attend(n_ctx, True)

    @pl.when(jnp.logical_not(is_ctx))
    def _():
        attend(k_ref.shape[1], False)


def _diff_attn(q, k, vt, lam_vec, subln, lam_init, n_ctx):
    b, t, _ = k.shape
    per_sample = lambda shape: pl.BlockSpec(shape, lambda i, j: (i, 0, 0))
    return pl.pallas_call(
        functools.partial(_diff_attn_kernel, lam_init, n_ctx),
        grid=(b, t // DIFF_STEP_Q),
        in_specs=[
            pl.BlockSpec((1, DIFF_STEP_Q, DIFF_QK_WIDTH), lambda i, j: (i, j, 0)),
            per_sample((1, t, DIFF_QK_WIDTH)),
            per_sample((1, DIFF_WIDTH, t)),
            _const_spec((4, DIFF_QK_DIM)),
            _const_spec((DIFF_V_DIM, 1)),
        ],
        out_specs=pl.BlockSpec((1, DIFF_STEP_Q, DIFF_WIDTH), lambda i, j: (i, j, 0)),
        out_shape=jax.ShapeDtypeStruct((b, t, DIFF_WIDTH), BF16),
        scratch_shapes=[pltpu.VMEM((2, t, 2 * DIFF_TQ), F32), pltpu.VMEM((2, t, 2 * DIFF_TQ), BF16),
                        pltpu.VMEM((2 * DIFF_HEADS, 2 * DIFF_TQ), F32)],
        compiler_params=_cparams(2),
    )(q, k, vt, lam_vec, subln.reshape(DIFF_V_DIM, 1))


def _residual_and_prenorm(g, x_ref, y, mod_ref, ng_ref, xo_ref, h2_ref, rows=None):
    x = x_ref[g] if rows is None else x_ref[g, rows]
    rows = slice(None) if rows is None else rows
    xn = x + _rms(y, ng_ref[1:2, :] * mod_ref[g, 2:3, :])
    xo_ref[g, rows] = xn
    h2 = _rms(xn, ng_ref[2:3, :] * (1.0 + mod_ref[g, 4:5, :])) + mod_ref[g, 3:4, :]
    h2_ref[g, rows] = h2.astype(BF16)


def _even_out_kernel(split_ctx_tiles, t_off, seg_tiles, n_tiles, *refs):
    x_ref, (u_ref, up_ref, un_ref, a_ref, mod_ref, ng_ref, pw_ref, ps_ref, wo_ref, xo_ref, h2_ref,
            ext_ref) = _token_rows(split_ctx_tiles, t_off, refs)
    j = pl.program_id(1) + t_off
    prev_ok = jnp.logical_and(j != 0, j != seg_tiles)
    next_ok = jnp.logical_and(j != n_tiles - 1, j != seg_tiles - 1)
    lo = jnp.where(prev_ok, -HALO, 0)
    hi = jnp.where(next_ok, TILE + HALO, TILE)
    row = lax.broadcasted_iota(jnp.int32, (TILE, 1), 0)
    for g in range(x_ref.shape[0]):
        ext_ref[g, 0:HALO, :] = jnp.where(prev_ok, up_ref[g].astype(F32), 0.0)
        ext_ref[g, HALO:HALO + TILE, :] = u_ref[g].astype(F32)
        ext_ref[g, HALO + TILE:, :] = jnp.where(next_ok, un_ref[g].astype(F32), 0.0)
        parts = []
        for gidx, win in enumerate(POOL_WINDOWS):
            half = win // 2
            cols = slice(gidx * POOL_GROUP_DIM, (gidx + 1) * POOL_GROUP_DIM)
            acc = ext_ref[g, HALO - half:HALO - half + TILE, cols]
            for off in range(-half + 1, half):
                acc = acc + ext_ref[g, HALO + off:HALO + off + TILE, cols]
            cnt = jnp.minimum(row + half, hi) - jnp.maximum(row - half, lo)
            dlt = acc / cnt.astype(F32) - ext_ref[g, HALO:HALO + TILE, cols]
            parts.append(_dot(dlt.astype(BF16), pw_ref[gidx]))
        yp = jnp.concatenate(parts, axis=1) * ps_ref[...]
        y = _dot(yp.astype(BF16), wo_ref[:POOL_WIDTH, :]) + _dot(a_ref[g], wo_ref[POOL_WIDTH:, :])
        _residual_and_prenorm(g, x_ref, y, mod_ref, ng_ref, xo_ref, h2_ref)


def _even_out(x, x_ctx, u, att, mod_g, ng_l, pool_w, pool_scale, w_out, ctx_tiles, t_off, group):
    b, t, d = x.shape
    if x_ctx is not None:
        t += x_ctx.shape[1]
    nt = t // TILE
    nh = TILE // HALO
    n_out = nt - t_off

    def mod_map(i, j):
        return (jnp.where(j + t_off < ctx_tiles, b // group, i), 0, 0)

    tok = lambda w: pl.BlockSpec((group, TILE, w), lambda i, j: (i, j + t_off, 0))
    out_tok = lambda: pl.BlockSpec((group, TILE, d), lambda i, j: (i, j, 0))
    tokens, token_specs = _token_specs(x, x_ctx, group, ctx_tiles, t_off)
    return pl.pallas_call(
        functools.partial(_even_out_kernel, None if x_ctx is None else ctx_tiles, t_off, ctx_tiles, nt),
        grid=(b // group, n_out),
        in_specs=token_specs + [
            tok(POOL_WIDTH),
            pl.BlockSpec((group, HALO, POOL_WIDTH), lambda i, j: (i, jnp.maximum((j + t_off) * nh - 1, 0), 0)),
            pl.BlockSpec((group, HALO, POOL_WIDTH),
                         lambda i, j: (i, jnp.minimum((j + t_off + 1) * nh, nt * nh - 1), 0)),
            tok(DIFF_WIDTH),
            pl.BlockSpec((group, 6, d), mod_map),
            _const_spec((4, d)),
            _const_spec((POOL_GROUPS, POOL_GROUP_DIM, POOL_GROUP_DIM)),
            _const_spec((1, POOL_WIDTH)),
            _const_spec((POOL_WIDTH + DIFF_WIDTH, d)),
        ],
        out_specs=[out_tok(), out_tok()],
        out_shape=[
            jax.ShapeDtypeStruct((b, n_out * TILE, d), F32),
            jax.ShapeDtypeStruct((b, n_out * TILE, d), BF16),
        ],
        scratch_shapes=[pltpu.VMEM((group, TILE + 2 * HALO, POOL_WIDTH), F32)],
        compiler_params=_cparams(2),
    )(*tokens, u, u, u, att, mod_g, ng_l, pool_w, pool_scale.reshape(1, POOL_WIDTH), w_out)


def _ffn_kernel(seg_tiles, n_tiles, x_ref, h_ref, hp_ref, hn_ref, mod_ref, ng_ref, wg_ref, wu_ref,
                cw_ref, cb_ref, wd_ref, o_ref, a_ref, act_ref):
    j = pl.program_id(1)
    prev_ok = jnp.logical_and(j != 0, j != seg_tiles)
    next_ok = jnp.logical_and(j != n_tiles - 1, j != seg_tiles - 1)
    group = x_ref.shape[0]
    ext = TILE + 2 * HALO
    zero = jnp.zeros((HALO, h_ref.shape[2]), BF16)
    pieces = []
    for g in range(group):
        pieces += [jnp.where(prev_ok, hp_ref[g], zero), h_ref[g], jnp.where(next_ok, hn_ref[g], zero)]
    h_ext = jnp.concatenate(pieces, axis=0)
    h = jnp.concatenate([h_ref[g] for g in range(group)], axis=0) if group > 1 else h_ref[0]
    for c in range(D_FF // FF_CHUNK):
        cols = slice(c * FF_CHUNK, (c + 1) * FF_CHUNK)
        a_ref[...] = _dot(h_ext, wg_ref[:, cols])
        up = _dot(h, wu_ref[:, cols])
        for g in range(group):
            base = g * ext + HALO
            a = (a_ref[base - 1:base - 1 + TILE, :] * cw_ref[0:1, cols]
                 + a_ref[base:base + TILE, :] * cw_ref[1:2, cols]
                 + a_ref[base + 1:base + 1 + TILE, :] * cw_ref[2:3, cols] + cb_ref[:, cols])
            act_ref[g * TILE:(g + 1) * TILE, cols] = (_silu(a) * up[g * TILE:(g + 1) * TILE]).astype(BF16)
    for g in range(group):
        f = _dot(act_ref[g * TILE:(g + 1) * TILE, :], wd_ref[...])
        o_ref[g] = x_ref[g] + _rms(f, ng_ref[3:4, :] * mod_ref[g, 5:6, :])


def _ffn(x, h2, mod_g, ng_l, w_gate, w_up, conv_w, conv_b, w_down, ctx_tiles, group):
    b, t, d = x.shape
    nt = t // TILE
    nh = TILE // HALO

    def mod_map(i, j):
        return (jnp.where(j < ctx_tiles, b // group, i), 0, 0)

    tok = lambda: pl.BlockSpec((group, TILE, d), lambda i, j: (i, j, 0))
    return pl.pallas_call(
        functools.partial(_ffn_kernel, ctx_tiles, nt),
        grid=(b // group, nt),
        in_specs=[
            tok(),
            tok(),
            pl.BlockSpec((group, HALO, d), lambda i, j: (i, jnp.maximum(j * nh - 1, 0), 0)),
            pl.BlockSpec((group, HALO, d), lambda i, j: (i, jnp.minimum((j + 1) * nh, nt * nh - 1), 0)),
            pl.BlockSpec((group, 6, d), mod_map),
            _const_spec((4, d)),
            _const_spec((d, D_FF)),
            _const_spec((d, D_FF)),
            _const_spec((3, D_FF)),
            _const_spec((1, D_FF)),
            _const_spec((D_FF, d)),
        ],
        out_specs=tok(),
        out_shape=jax.ShapeDtypeStruct((b, t, d), F32),
        scratch_shapes=[pltpu.VMEM((group * (TILE + 2 * HALO), FF_CHUNK), F32),
                        pltpu.VMEM((group * TILE, D_FF), BF16)],
        compiler_params=_cparams(2),
    )(x, h2, h2, h2, mod_g, ng_l, w_gate, w_up, conv_w, conv_b.reshape(1, D_FF), w_down)


def _odd_in_kernel(layer, x_ref, mod_ref, ng_ref, cos_ref, sin_ref, wq_ref, wkv_ref, wkr_ref, wh_ref,
                   qn_ref, wuq_ref, kvn_ref, wukv_ref, lb_ref,
                   q_ref, kc_ref, vt_ref, hq_ref, hkf_ref, hkb_ref, lff_ref, lfb_ref, hv_ref, hg_ref):
    cos = cos_ref[...]
    sin = sin_ref[...]
    w = HG_QK_WIDTH
    lbs = []
    for direction in range(2):
        lrows = [lb_ref[direction, i:i + 1, :] for i in range(lb_ref.shape[1])]
        top = functools.reduce(jnp.maximum, lrows)
        ex = [jnp.exp(r - top) for r in lrows]
        lbs.append(sum(ex[1:layer + 1], jnp.zeros_like(top)) / sum(ex))

    for g in range(x_ref.shape[0]):
        hb = _prenorm(g, x_ref, mod_ref, ng_ref)
        cq = _rms(_dot(hb, wq_ref[...]), qn_ref[...])
        q = _dot(cq.astype(BF16), wuq_ref[...]) * ((MLA_NOPE_DIM + MLA_ROPE_DIM) ** -0.5 * LOG2E)
        ckv = _rms(_dot(hb, wkv_ref[...]), kvn_ref[...])
        kv = _dot(ckv.astype(BF16), wukv_ref[...])
        kr = _rope(_dot(hb, wkr_ref[...]), cos, sin).astype(BF16)
        for hd in range(MLA_HEADS):
            qb = hd * MLA_QK_PAD
            q_ref[g, :, qb:qb + MLA_NOPE_DIM] = q[:, qb:qb + MLA_NOPE_DIM].astype(BF16)
            q_ref[g, :, qb + MLA_NOPE_DIM:qb + MLA_QK_PAD] = _rope(
                q[:, qb + MLA_NOPE_DIM:qb + MLA_QK_PAD], cos, sin).astype(BF16)
            kb = hd * (MLA_NOPE_DIM + MLA_V_DIM)
            kc_ref[g, hd, :, :MLA_NOPE_DIM] = kv[:, kb:kb + MLA_NOPE_DIM].astype(BF16)
            kc_ref[g, hd, :, MLA_NOPE_DIM:] = kr
            vt_ref[g, hd * MLA_V_DIM:(hd + 1) * MLA_V_DIM, :] = kv[
                :, kb + MLA_NOPE_DIM:kb + MLA_NOPE_DIM + MLA_V_DIM].T.astype(BF16)

        hq_ref[g] = _silu(_dot(hb, wh_ref[:, :w])).astype(BF16)
        hv_ref[g] = _dot(hb, wh_ref[:, 3 * w:4 * w]).astype(BF16)
        hg_ref[g] = _silu(_dot(hb, wh_ref[:, 4 * w:])).astype(BF16)
        for direction, (k_ref, lf_ref) in enumerate(((hkf_ref, lff_ref), (hkb_ref, lfb_ref))):
            lb = lbs[direction]
            f = lb + (1.0 - lb) * jax.nn.sigmoid(_dot(hb, wh_ref[:, (1 + direction) * w:(2 + direction) * w]))
            k_ref[g] = (1.0 - f).astype(BF16)
            lf_ref[g] = jnp.log2(f)


def _odd_in(x, mod_g, ng_l, cos, sin, w_in, q_norm, w_uq, kv_norm, w_ukv, hgrn_lb, layer, ctx_tiles, group):
    b, t, d = x.shape
    nt = t // TILE
    o1 = MLA_Q_RANK
    o2 = o1 + MLA_KV_RANK
    o3 = o2 + MLA_ROPE_DIM
    w_q = w_in[:, :o1]
    w_kv = w_in[:, o1:o2]
    w_kr = jnp.pad(w_in[:, o2:o3], ((0, 0), (0, LANES - MLA_ROPE_DIM)))
    w_h = w_in[:, o3:]
    qk = MLA_NOPE_DIM + MLA_ROPE_DIM
    w_uq_pad = jnp.pad(w_uq.reshape(MLA_Q_RANK, MLA_HEADS, qk),
                       ((0, 0), (0, 0), (0, MLA_QK_PAD - qk))).reshape(MLA_Q_RANK, MLA_HEADS * MLA_QK_PAD)

    def mod_map(i, j):
        return (jnp.where(j < ctx_tiles, b // group, i), 0, 0)

    tok = lambda w: pl.BlockSpec((group, TILE, w), lambda i, j: (i, j, 0))
    sds = lambda w, dt: jax.ShapeDtypeStruct((b, t, w), dt)
    hw = HG_QK_WIDTH
    return pl.pallas_call(
        functools.partial(_odd_in_kernel, layer),
        grid=(b // group, nt),
        in_specs=[
            tok(d),
            pl.BlockSpec((group, 6, d), mod_map),
            _const_spec((4, d)),
            pl.BlockSpec((TILE, LANES), lambda i, j: (j, 0)),
            pl.BlockSpec((TILE, LANES), lambda i, j: (j, 0)),
            _const_spec(w_q.shape),
            _const_spec(w_kv.shape),
            _const_spec(w_kr.shape),
            _const_spec(w_h.shape),
            _const_spec((1, MLA_Q_RANK)),
            _const_spec(w_uq_pad.shape),
            _const_spec((1, MLA_KV_RANK)),
            _const_spec(w_ukv.shape),
            _const_spec(hgrn_lb.shape),
        ],
        out_specs=[
            tok(MLA_HEADS * MLA_QK_PAD),
            pl.BlockSpec((group, MLA_HEADS, TILE, MLA_QK_PAD), lambda i, j: (i, 0, j, 0)),
            pl.BlockSpec((group, MLA_WIDTH, TILE), lambda i, j: (i, 0, j)),
            tok(hw), tok(hw), tok(hw), tok(hw), tok(hw), tok(hw), tok(hw),
        ],
        out_shape=[
            sds(MLA_HEADS * MLA_QK_PAD, BF16),
            jax.ShapeDtypeStruct((b, MLA_HEADS, t, MLA_QK_PAD), BF16),
            jax.ShapeDtypeStruct((b, MLA_WIDTH, t), BF16),
            sds(hw, BF16), sds(hw, BF16), sds(hw, BF16), sds(hw, F32), sds(hw, F32), sds(hw, BF16),
            sds(hw, BF16),
        ],
        compiler_params=_cparams(2),
    )(x, mod_g, ng_l, cos, sin, w_q, w_kv, w_kr, w_h, q_norm.reshape(1, -1), w_uq_pad,
      kv_norm.reshape(1, -1), w_ukv, hgrn_lb)


def _mla_attn_kernel(n_ctx, q_off, q_tiles, *refs):
    q_refs = refs[:q_tiles]
    kc_ref, vt_ref, o_ref, s_ref, e_ref, ksq_ref = refs[q_tiles:]

    @pl.when(pl.program_id(1) == 0)
    def _():
        all_dims = jnp.ones((1, MLA_QK_PAD), F32)
        for h in range(MLA_HEADS):
            kf = kc_ref[0, h].astype(F32)
            _store_key_sq(ksq_ref, h, slice(0, MLA_TQ), kf * kf, all_dims, n_ctx)

    def attend(nk, ctx_only):
        ksq_row = 0 if ctx_only else MLA_HEADS

        def finish(p, o_t, l):
            h, qi = divmod(p, q_tiles)
            o_ref[0, qi * MLA_TQ:(qi + 1) * MLA_TQ, h * MLA_V_DIM:(h + 1) * MLA_V_DIM] = (
                o_t * (1.0 / l)).T.astype(BF16)

        def scores_operands(p):
            h, qi = divmod(p, q_tiles)
            return kc_ref.at[0, h, :nk, :], q_refs[qi][0, :, h * MLA_QK_PAD:(h + 1) * MLA_QK_PAD]

        def values_t(p):
            h = p // q_tiles
            return vt_ref.at[0, h * MLA_V_DIM:(h + 1) * MLA_V_DIM, :nk]

        def key_sq(p):
            r = ksq_row + p // q_tiles
            return ksq_ref[r:r + 1, :]

        _attend_t(MLA_HEADS * q_tiles, nk, scores_operands, values_t, key_sq, s_ref, e_ref, finish)

    if q_off * MLA_TQ >= n_ctx:
        attend(kc_ref.shape[2], False)
    else:
        is_ctx = pl.program_id(1) + q_off < n_ctx // MLA_TQ

        @pl.when(is_ctx)
        def _():
            attend(n_ctx, True)

        @pl.when(jnp.logical_not(is_ctx))
        def _():
            attend(kc_ref.shape[2], False)


def _mla_attn(q, kc, vt, n_ctx, q_off):
    b, _, t, _ = kc.shape
    nq = t // MLA_TQ - q_off
    q_tiles = MLA_STEP_TILES if (q_off * MLA_TQ >= n_ctx and nq % MLA_STEP_TILES == 0) else 1

    def q_spec(qi):
        return pl.BlockSpec((1, MLA_TQ, MLA_HEADS * MLA_QK_PAD), lambda i, j: (i, j * q_tiles + qi + q_off, 0))

    return pl.pallas_call(
        functools.partial(_mla_attn_kernel, n_ctx, q_off, q_tiles),
        grid=(b, nq // q_tiles),
        in_specs=[q_spec(qi) for qi in range(q_tiles)] + [
            pl.BlockSpec((1, MLA_HEADS, t, MLA_QK_PAD), lambda i, j: (i, 0, 0, 0)),
            pl.BlockSpec((1, MLA_WIDTH, t), lambda i, j: (i, 0, 0)),
        ],
        out_specs=pl.BlockSpec((1, q_tiles * MLA_TQ, MLA_WIDTH), lambda i, j: (i, j, 0)),
        out_shape=jax.ShapeDtypeStruct((b, nq * MLA_TQ, MLA_WIDTH), BF16),
        scratch_shapes=[pltpu.VMEM((2, t, MLA_TQ), F32), pltpu.VMEM((2, t, MLA_TQ), BF16),
                        pltpu.VMEM((2 * MLA_HEADS, MLA_TQ), F32)],
        compiler_params=_cparams(2),
    )(*([q] * q_tiles), kc, vt)


def _hgrn_triangle(reverse):
    t = np.arange(HG_CHUNK)[:, None]
    j = np.arange(HG_CHUNK)[None, :]
    tri = ((j >= t) if reverse else (j <= t)).astype(np.float32)
    return np.kron(np.eye(HG_STACK, dtype=np.float32), tri)


def _hgrn_masks(reverse):
    c = HG_STACK * HG_CHUNK
    t = lax.broadcasted_iota(jnp.int32, (c, c), 0)
    s = lax.broadcasted_iota(jnp.int32, (c, c), 1)
    tq = lax.broadcasted_iota(jnp.int32, (c, 1), 0)
    masks = []
    for m in HG_LEVELS:
        half = m // 2
        shift = int(math.log2(m))
        t_hi = (t & (m - 1)) >= half
        s_hi = (s & (m - 1)) >= half
        same = (t >> shift) == (s >> shift)
        if not reverse:
            pair = jnp.where(same, jnp.where(t_hi, jnp.where(s_hi, 0.0, 1.0), 0.0), 0.0)
            is_q = (tq & (m - 1)) >= half
        else:
            pair = jnp.where(same, jnp.where(t_hi, 0.0, jnp.where(s_hi, 1.0, 0.0)), 0.0)
            is_q = (tq & (m - 1)) < half
        masks.append((pair > 0.5, is_q))
    return masks, t == s


def _hgrn_reference_rows(cum, m, reverse):
    c = cum.shape[0]
    half = m // 2
    ref_in_block = half if reverse else half - 1
    if m >= 2 * SUBLANES:
        return jnp.concatenate(
            [jnp.broadcast_to(cum[b0 + ref_in_block:b0 + ref_in_block + 1, :], (m, LANES))
             for b0 in range(0, c, m)], axis=0)
    cum3 = cum.reshape(c // SUBLANES, SUBLANES, LANES)
    r = lax.broadcasted_iota(jnp.int32, cum3.shape, 1)
    out = None
    for b0 in reversed(range(0, SUBLANES, m)):
        pick = jnp.broadcast_to(cum3[:, b0 + ref_in_block:b0 + ref_in_block + 1, :], cum3.shape)
        out = pick if out is None else jnp.where(r < b0 + m, pick, out)
    return out.reshape(c, LANES)


def _hgrn_scan_body(reverse, tri_ref, lf_ref, q_ref, k_ref, v_ref, st_ref, emit):
    masks, eye = _hgrn_masks(reverse)
    tri = tri_ref[...]
    c = HG_CHUNK
    pr = HG_STACK * c
    n_stacks = HG_BLOCK // pr
    back_to_front = lambda n: list(range(n - 1, -1, -1) if reverse else range(n))
    group = lf_ref.shape[0]
    probs = [(g, si, h) for si in back_to_front(n_stacks) for g in range(group) for h in range(HG_HEADS)]
    where = lambda g, si, h: (g, slice(si * pr, (si + 1) * pr), slice(h * HG_K_DIM, (h + 1) * HG_K_DIM))
    sub = lambda j: slice(j * c, (j + 1) * c)

    @pl.when(pl.program_id(1) == 0)
    def _():
        st_ref[...] = jnp.zeros_like(st_ref)

    cums, lfs = [], []
    for p in probs:
        g, rows, cols = where(*p)
        lf = lf_ref[g, rows, cols]
        lfs.append(lf)
        hi = lf.astype(BF16)
        r1 = lf - hi.astype(F32)
        mid = r1.astype(BF16)
        lo = (r1 - mid.astype(F32)).astype(BF16)
        c3 = _dot(tri, jnp.concatenate([hi, mid, lo], axis=1))
        cums.append(c3[:, :LANES] + c3[:, LANES:2 * LANES] + c3[:, 2 * LANES:])

    qf, kf, qe, ks, decay, scores = [], [], [], [], [], []
    for p, cum in zip(probs, cums):
        g, rows, cols = where(*p)
        totals = [cum[j * c:j * c + 1] if reverse else cum[(j + 1) * c - 1:(j + 1) * c] for j in range(HG_STACK)]
        total_rows = jnp.concatenate([jnp.broadcast_to(tj, (c, LANES)) for tj in totals], axis=0)
        q = q_ref[g, rows, cols]
        k = k_ref[g, rows, cols]
        qf.append(q.astype(F32))
        kf.append(k.astype(F32))
        qe.append((qf[-1] * jnp.exp2(cum)).astype(BF16))
        ks.append((kf[-1] * jnp.exp2(total_rows - cum)).astype(BF16))
        decay.append([jnp.exp2(tj) for tj in totals])
        scores.append(jnp.where(eye, _dot_nt(q, k), 0.0))

    for m, (pair, is_q) in zip(HG_LEVELS, masks):
        for i, cum in enumerate(cums):
            if m == 2:
                x = jnp.where(is_q, qf[i] * jnp.exp2(lfs[i]), kf[i]).astype(BF16)
            else:
                d = cum - _hgrn_reference_rows(cum, m, reverse)
                x = (jnp.where(is_q, qf[i], kf[i]) * jnp.exp2(-jnp.abs(d))).astype(BF16)
            scores[i] = jnp.where(pair, _dot_nt(x, x), scores[i])

    intra, update = [], []
    for i, p in enumerate(probs):
        g, rows, cols = where(*p)
        v = v_ref[g, rows, cols]
        intra.append(_dot(scores[i].astype(BF16), v))
        v_t = v.astype(F32).T.astype(BF16)
        update.append([_dot(v_t[:, sub(j)], ks[i][sub(j), :]) for j in range(HG_STACK)])

    for gh in range(group * HG_HEADS):
        st = st_ref[gh]
        for n in range(n_stacks):
            i = n * group * HG_HEADS + gh
            g, rows, cols = where(*probs[i])
            for j in back_to_front(HG_STACK):
                out_rows = slice(rows.start + j * c, rows.start + (j + 1) * c)
                emit(g, out_rows, cols, intra[i][sub(j), :] + _dot_nt(qe[i][sub(j), :], st.astype(BF16)))
                st = st * decay[i][j] + update[i][j]
        st_ref[gh] = st


def _hgrn_fwd_kernel(tri_ref, lf_ref, q_ref, k_ref, v_ref, o_ref, st_ref):
    def emit(g, rows, cols, o):
        o_ref[g, rows, cols] = o

    _hgrn_scan_body(False, tri_ref, lf_ref, q_ref, k_ref, v_ref, st_ref, emit)


def _hgrn_bwd_kernel(tri_ref, lf_ref, q_ref, k_ref, v_ref, of_ref, gate_ref, gn_ref, o_ref, st_ref):
    def emit(g, rows, cols, o):
        o_ref[g, rows, cols] = (_rms(o + of_ref[g, rows, cols], gn_ref[...])
                                * gate_ref[g, rows, cols].astype(F32)).astype(BF16)

    _hgrn_scan_body(True, tri_ref, lf_ref, q_ref, k_ref, v_ref, st_ref, emit)


def _hgrn_scan(reverse, lf, q, k, v, ctx_blocks, extra=()):
    b, t, w = lf.shape
    nb = t // HG_BLOCK
    tri = jnp.asarray(_hgrn_triangle(reverse), BF16)
    if reverse:
        blk = lambda j: jnp.where(j < ctx_blocks, ctx_blocks - 1 - j, nb - 1 - (j - ctx_blocks))
    else:
        blk = lambda j: j
    group = _group_size(b, HG_GROUP)
    tok = pl.BlockSpec((group, HG_BLOCK, w), lambda i, j: (i, blk(j), 0))
    in_specs = [_const_spec(tri.shape), tok, tok, tok, tok]
    args = [tri, lf, q, k, v]
    if reverse:
        o_f, gate, gnorm = extra
        in_specs += [tok, tok, _const_spec((1, HG_V_DIM))]
        args += [o_f, gate, gnorm.reshape(1, HG_V_DIM)]
    return pl.pallas_call(
        _hgrn_bwd_kernel if reverse else _hgrn_fwd_kernel,
        grid=(b // group, nb),
        in_specs=in_specs,
        out_specs=tok,
        out_shape=jax.ShapeDtypeStruct((b, t, w), BF16 if reverse else F32),
        scratch_shapes=[pltpu.VMEM((group * HG_HEADS, HG_V_DIM, HG_K_DIM), F32)],
        compiler_params=_cparams(2),
    )(*args)


def _odd_out_kernel(x_ref, a_ref, g_ref, mod_ref, ng_ref, wo_ref, xo_ref, h2_ref):
    for g in range(x_ref.shape[0]):
        for r0 in range(0, TILE, TILE // 2):
            rows = slice(r0, r0 + TILE // 2)
            y = _dot(a_ref[g, rows], wo_ref[:MLA_WIDTH, :]) + _dot(g_ref[g, rows], wo_ref[MLA_WIDTH:, :])
            _residual_and_prenorm(g, x_ref, y, mod_ref, ng_ref, xo_ref, h2_ref, rows)


def _odd_out(x, att, hg, mod_g, ng_l, w_out, ctx_tiles, t_off, group):
    b, t, d = x.shape
    n_out = t // TILE - t_off

    def mod_map(i, j):
        return (jnp.where(j + t_off < ctx_tiles, b // group, i), 0, 0)

    out_tok = lambda w: pl.BlockSpec((group, TILE, w), lambda i, j: (i, j, 0))
    return pl.pallas_call(
        _odd_out_kernel,
        grid=(b // group, n_out),
        in_specs=[
            pl.BlockSpec((group, TILE, d), lambda i, j: (i, j + t_off, 0)),
            out_tok(MLA_WIDTH),
            pl.BlockSpec((group, TILE, HG_WIDTH), lambda i, j: (i, j + t_off, 0)),
            pl.BlockSpec((group, 6, d), mod_map),
            _const_spec((4, d)),
            _const_spec((MLA_WIDTH + HG_WIDTH, d)),
        ],
        out_specs=[out_tok(d), out_tok(d)],
        out_shape=[
            jax.ShapeDtypeStruct((b, n_out * TILE, d), F32),
            jax.ShapeDtypeStruct((b, n_out * TILE, d), BF16),
        ],
        compiler_params=_cparams(2),
    )(x, att, hg, mod_g, ng_l, w_out)


def _rope_tables(n_ctx, n_lat):
    rows = n_lat // GRID_W
    pos = jnp.stack([jnp.repeat(jnp.arange(rows), GRID_W), jnp.tile(jnp.arange(GRID_W), rows)], axis=-1)
    axis_dim = DIFF_QK_DIM // 2
    inv_freq = ROPE_THETA ** (-jnp.arange(0, axis_dim, 2, dtype=F32) / axis_dim)
    ang = pos.astype(F32)[..., None] * inv_freq
    cos = jnp.cos(ang)
    sin = jnp.sin(ang)
    cos64 = jnp.concatenate([cos[:, 0], cos[:, 0], cos[:, 1], cos[:, 1]], axis=-1)
    sin64 = jnp.concatenate([-sin[:, 0], sin[:, 0], -sin[:, 1], sin[:, 1]], axis=-1)
    cos64 = jnp.concatenate([jnp.ones((n_ctx, 64), F32), cos64], axis=0)
    sin64 = jnp.concatenate([jnp.zeros((n_ctx, 64), F32), sin64], axis=0)
    return cos64, sin64


def kernel(x, c, ctx, c_ctx, ada_w, ada_b, norm_g, mix_w_out, ffn_w_gate, ffn_w_up, ffn_conv_w, ffn_conv_b,
           ffn_w_down, ev_w_in, pool_w, pool_scale, diff_lambda, diff_subln, od_w_in, mla_q_norm, mla_w_uq,
           mla_kv_norm, mla_w_ukv, hgrn_norm, hgrn_lb):
    b, n_lat, d = x.shape
    n_ctx = ctx.shape[1]
    depth = ada_w.shape[0]
    assert d == D_MODEL and n_ctx % TILE == 0 and n_lat % TILE == 0 and n_lat % GRID_W == 0
    ctx_tiles = n_ctx // TILE
    group = _group_size(b, MAX_GROUP)
    group_odd_in = _group_size(b, ODD_IN_GROUP)

    rows = -(-(b + 1) // SUBLANES) * SUBLANES
    cond = jnp.concatenate([c, c_ctx[None, :], jnp.zeros((rows - b - 1, d), F32)], axis=0)
    mod = _modulation(cond, ada_w, ada_b).reshape(depth, rows, 6, d)

    cos64, sin64 = _rope_tables(n_ctx, n_lat)
    cos_diff = jnp.tile(cos64, (1, 2 * DIFF_HEADS))
    sin_diff = jnp.tile(sin64, (1, 2 * DIFF_HEADS))
    pad = ((0, 0), (0, LANES - MLA_ROPE_DIM))
    cos_mla = jnp.pad(cos64, pad, constant_values=1.0)
    sin_mla = jnp.pad(sin64, pad)

    xs, xs_ctx = x, ctx
    for layer in range(depth):
        last = layer == depth - 1
        j = layer // 2
        t_off = ctx_tiles if last else 0
        mod_g = _group_mod(mod[layer], b, group)
        ng_l = norm_g[layer]
        w_out = mix_w_out[layer].astype(BF16)
        if layer % 2 == 0:
            lam_init = 0.8 - 0.6 * math.exp(-0.3 * layer)
            u, qt, k, vt = _even_in(xs, xs_ctx, mod_g, ng_l, cos_diff, sin_diff, ev_w_in[j].astype(BF16),
                                    ctx_tiles, group)
            att = _diff_attn(qt, k, vt, diff_lambda[j], diff_subln[j], lam_init, n_ctx)
            x_mid, h2 = _even_out(xs, xs_ctx, u, att, mod_g, ng_l, pool_w[j].astype(BF16), pool_scale[j], w_out,
                                  ctx_tiles, t_off, group)
        else:
            assert xs_ctx is None
            (qt, kc, vt, hq, hk_f, hk_b, lf_f, lf_b, hv, hgate) = _odd_in(
                xs, _group_mod(mod[layer], b, group_odd_in), ng_l, cos_mla, sin_mla, od_w_in[j].astype(BF16),
                mla_q_norm[j], mla_w_uq[j].astype(BF16), mla_kv_norm[j], mla_w_ukv[j].astype(BF16), hgrn_lb,
                layer, ctx_tiles, group_odd_in)
            att = _mla_attn(qt, kc, vt, n_ctx, t_off * (TILE // MLA_TQ))
            o_f = _hgrn_scan(False, lf_f, hq, hk_f, hv, n_ctx // HG_BLOCK)
            hg = _hgrn_scan(True, lf_b, hq, hk_b, hv, n_ctx // HG_BLOCK, (o_f, hgate, hgrn_norm[j]))
            x_mid, h2 = _odd_out(xs, att, hg, mod_g, ng_l, w_out, ctx_tiles, t_off, group)
        xs = _ffn(x_mid, h2, mod_g, ng_l, ffn_w_gate[layer].astype(BF16), ffn_w_up[layer].astype(BF16),
                  ffn_conv_w[layer], ffn_conv_b[layer], ffn_w_down[layer].astype(BF16),
                  0 if last else ctx_tiles, group)
        xs_ctx = None
    return xs
```
